```python
import math
import jax, jax.numpy as jnp
from jax import lax
import numpy as np

D_MODEL = 2048
BATCH = 4
SEQ = 2048
DEPTH = 1
DEC_BATCH = 32
DEC_SEQ = 1
PAST_LEN = 16384
PAGE_SIZE = 128

D_MIX = D_MODEL
D_SSM = D_MIX // 2
D_DIFF = D_MIX // 4
D_MEMX = D_MIX - D_SSM - D_DIFF
SSM_HEAD_DIM = 64
SSM_HEADS = D_SSM // SSM_HEAD_DIM
SSM_GROUPS = 2
SSM_HPG = SSM_HEADS // SSM_GROUPS
D_STATE = 128
CONV_W = 4
CONV_DIM = D_SSM + 2 * SSM_GROUPS * D_STATE
SSD_CHUNK = 128
DIFF_HEADS = 4
DIFF_DV = D_DIFF // DIFF_HEADS
DIFF_DK = DIFF_DV // 2
Q_BLOCK = 128
MEM_TOKENS = 256
MEM_HEADS = 4
MEM_HD = D_MEMX // MEM_HEADS
D_FF = 4 * D_MODEL
EPS = 1e-6
IN_WIDTHS = (D_SSM, CONV_DIM, SSM_HEADS, D_DIFF, D_DIFF, D_DIFF, D_MEMX)
N_IN = D_SSM + CONV_DIM + SSM_HEADS + 3 * D_DIFF + D_MEMX

kernel_name = 'hymba_ssd_diffattn_memx_step'


def rmsnorm(x, g):
    xf = x.astype(jnp.float32)
    y = xf * lax.rsqrt(jnp.mean(xf * xf, axis=-1, keepdims=True) + EPS)
    return (y * g.astype(jnp.float32)).astype(x.dtype)


def project(x, g_norm, w_in):
    h = rmsnorm(x, g_norm)
    u = h @ w_in
    return jnp.split(u, np.cumsum(IN_WIDTHS)[:-1].tolist(), axis=-1)


def causal_dwconv(xbc, prev, conv_w, conv_b):
    L = xbc.shape[1]
    full = jnp.concatenate([prev, xbc], axis=1)
    out = conv_b
    for j in range(CONV_W):
        out = out + full[:, j:j + L] * conv_w[j]
    return jax.nn.silu(out), full[:, -(CONV_W - 1):]


def segsum(a):
    T = a.shape[-1]
    c = jnp.cumsum(a, axis=-1)
    d = c[..., :, None] - c[..., None, :]
    mask = jnp.tril(jnp.ones((T, T), dtype=bool))
    return jnp.where(mask, d, -jnp.inf)


def ssd_scan(x, dt, a, b, c, h0):
    Bsz, L, G, R, P = x.shape
    N = b.shape[-1]
    cs = math.gcd(L, SSD_CHUNK)
    nc = L // cs
    x = x.reshape(Bsz, nc, cs, G, R, P)
    dt = dt.reshape(Bsz, nc, cs, G, R)
    b = b.reshape(Bsz, nc, cs, G, N)
    c = c.reshape(Bsz, nc, cs, G, N)
    adt = (dt * a).transpose(0, 3, 4, 1, 2)
    a_cum = jnp.cumsum(adt, axis=-1)
    xdt = x * dt[..., None]
    lmat = jnp.exp(segsum(adt))
    cb = jnp.einsum('bclgn,bcsgn->bgcls', c, b)
    y_diag = jnp.einsum('bgrcls,bcsgrp->bclgrp', cb[:, :, None] * lmat, xdt)
    decay_states = jnp.exp(a_cum[..., -1:] - a_cum)
    states = jnp.einsum('bcsgn,bgrcs,bcsgrp->bcgrpn', b, decay_states, xdt)
    states = jnp.concatenate([h0[:, None], states], axis=1)
    chunk_decay = jnp.exp(segsum(jnp.pad(a_cum[..., -1], ((0, 0), (0, 0), (0, 0), (1, 0)))))
    new_states = jnp.einsum('bgrzc,bcgrpn->bzgrpn', chunk_decay, states)
    y_off = jnp.einsum('bclgn,bcgrpn,bgrcl->bclgrp', c, new_states[:, :-1], jnp.exp(a_cum))
    y = (y_diag + y_off).reshape(Bsz, L, G, R, P)
    return y, new_states[:, -1]


def ssm_group(z, xbc, dt_raw, conv_prev, h0, conv_w, conv_b, dt_bias, a_log, d_skip, g_ssm):
    Bsz, L, _ = z.shape
    f32 = jnp.float32
    xbc_act, conv_new = causal_dwconv(xbc, conv_prev, conv_w, conv_b)
    xs, bs, cs_ = jnp.split(xbc_act.astype(f32), [D_SSM, D_SSM + SSM_GROUPS * D_STATE], axis=-1)
    xs = xs.reshape(Bsz, L, SSM_GROUPS, SSM_HPG, SSM_HEAD_DIM)
    bs = bs.reshape(Bsz, L, SSM_GROUPS, D_STATE)
    cs_ = cs_.reshape(Bsz, L, SSM_GROUPS, D_STATE)
    dt = jax.nn.softplus(dt_raw.astype(f32) + dt_bias.astype(f32)).reshape(Bsz, L, SSM_GROUPS, SSM_HPG)
    a = -jnp.exp(a_log.astype(f32)).reshape(SSM_GROUPS, SSM_HPG)
    h0r = h0.astype(f32).reshape(Bsz, SSM_GROUPS, SSM_HPG, SSM_HEAD_DIM, D_STATE)
    y, h_new = ssd_scan(xs, dt, a, bs, cs_, h0r)
    y = y + d_skip.astype(f32).reshape(SSM_GROUPS, SSM_HPG)[..., None] * xs
    y = y.reshape(Bsz, L, D_SSM) * jax.nn.silu(z.astype(f32))
    y = rmsnorm(y.reshape(Bsz, L, SSM_GROUPS, D_SSM // SSM_GROUPS),
                g_ssm.reshape(SSM_GROUPS, D_SSM // SSM_GROUPS)).reshape(Bsz, L, D_SSM)
    h_new = h_new.reshape(Bsz, SSM_HEADS, SSM_HEAD_DIM, D_STATE)
    return y.astype(z.dtype), conv_new, h_new.astype(z.dtype)


def attn_views(q, k, v, mq, g_q, g_k, g_mq):
    Bsz, L = q.shape[:2]
    q = rmsnorm(q.reshape(Bsz, L, DIFF_HEADS, 2, DIFF_DK), g_q)
    k = rmsnorm(k.reshape(Bsz, L, DIFF_HEADS, 2, DIFF_DK), g_k)
    v = v.reshape(Bsz, L, DIFF_HEADS, DIFF_DV)
    mq = rmsnorm(mq.reshape(Bsz, L, MEM_HEADS, MEM_HD), g_mq)
    return q, k, v, mq


def diff_lambda(lq1, lk1, lq2, lk2, layer):
    lam_init = 0.8 - 0.6 * math.exp(-0.3 * layer)
    f32 = jnp.float32
    lam = (jnp.exp(jnp.sum(lq1.astype(f32) * lk1.astype(f32)))
           - jnp.exp(jnp.sum(lq2.astype(f32) * lk2.astype(f32))) + lam_init)
    return lam, lam_init


def diff_attn_prompt(q, k, v, lam):
    Bsz, S = q.shape[:2]
    scale = DIFF_DK ** -0.5
    kpos = jnp.arange(S)

    def one_block(i):
        qb = lax.dynamic_slice_in_dim(q, i * Q_BLOCK, Q_BLOCK, axis=1)
        s = jnp.einsum('bqhmd,bkhmd->bhmqk', qb, k).astype(jnp.float32) * scale
        qpos = i * Q_BLOCK + jnp.arange(Q_BLOCK)
        s = jnp.where(kpos[None, :] <= qpos[:, None], s, -jnp.inf)
        p = jax.nn.softmax(s, axis=-1)
        w = p[:, :, 0] - lam * p[:, :, 1]
        return jnp.einsum('bhqk,bkhd->bqhd', w.astype(v.dtype), v)

    o = lax.map(one_block, jnp.arange(S // Q_BLOCK))
    return o.transpose(1, 0, 2, 3, 4).reshape(Bsz, S, DIFF_HEADS, DIFF_DV)


def diff_attn_sample(q, k_new, v_new, k_past, v_past, lam):
    T = q.shape[1]
    P = k_past.shape[1]
    scale = DIFF_DK ** -0.5
    s_past = jnp.einsum('bqhmd,bkhmd->bhmqk', q, k_past).astype(jnp.float32) * scale
    s_new = jnp.einsum('bqhmd,bkhmd->bhmqk', q, k_new).astype(jnp.float32) * scale
    s_new = jnp.where(jnp.tril(jnp.ones((T, T), dtype=bool)), s_new, -jnp.inf)
    p = jax.nn.softmax(jnp.concatenate([s_past, s_new], axis=-1), axis=-1)
    w = (p[:, :, 0] - lam * p[:, :, 1]).astype(v_new.dtype)
    return (jnp.einsum('bhqk,bkhd->bqhd', w[..., :P], v_past)
            + jnp.einsum('bhqk,bkhd->bqhd', w[..., P:], v_new))


def mem_kv(mem, g_mem, w_mem_kv, g_mk):
    Bsz, M, _ = mem.shape
    m = rmsnorm(mem, g_mem) @ w_mem_kv
    mk, mv = jnp.split(m, 2, axis=-1)
    mk = rmsnorm(mk.reshape(Bsz, M, MEM_HEADS, MEM_HD), g_mk)
    return mk, mv.reshape(Bsz, M, MEM_HEADS, MEM_HD)


def mem_attn(mq, mk, mv):
    s = jnp.einsum('bqhd,bkhd->bhqk', mq, mk).astype(jnp.float32) * (MEM_HD ** -0.5)
    p = jax.nn.softmax(s, axis=-1)
    return jnp.einsum('bhqk,bkhd->bqhd', p.astype(mv.dtype), mv)


def finish(x, y_ssm, o_diff, o_mem, lam_init, g_subln, w_out, g_mlp, w_up, w_down):
    Bsz, L, _ = x.shape
    o_diff = rmsnorm(o_diff, g_subln) * (1.0 - lam_init)
    mix = jnp.concatenate([y_ssm, o_diff.reshape(Bsz, L, D_DIFF), o_mem.reshape(Bsz, L, D_MEMX)], axis=-1)
    x = x + mix @ w_out
    h = rmsnorm(x, g_mlp)
    return x + jnp.square(jax.nn.relu(h @ w_up)) @ w_down


def setup_inputs(seed: int = 0) -> dict:
    key = jax.random.key(seed)
    ks = iter(jax.random.split(key, 48))
    f32 = jnp.float32

    def nrm(shape, scale=1.0):
        return jax.random.normal(next(ks), shape, f32) * scale

    def gain(n):
        return 1.0 + nrm((DEPTH, n), 0.02)

    n_pages = PAST_LEN // PAGE_SIZE
    n_used = DEC_BATCH * n_pages
    n_pool = n_used + n_used // 4
    page_table = jax.random.permutation(next(ks), n_pool)[:n_used].reshape(DEC_BATCH, n_pages).astype(jnp.int32)
    dt0 = jnp.exp(jax.random.uniform(next(ks), (DEPTH, SSM_HEADS), f32, math.log(1e-3), math.log(1e-1)))
    dt_bias = dt0 + jnp.log(-jnp.expm1(-dt0))
    a_log = jnp.log(jax.random.uniform(next(ks), (DEPTH, SSM_HEADS), f32, 1.0, 16.0))
    return {
        'x_prompt': nrm((BATCH, SEQ, D_MODEL)),
        'x_sample': nrm((DEC_BATCH, DEC_SEQ, D_MODEL)),
        'mem_prompt': nrm((BATCH, MEM_TOKENS, D_MODEL)),
        'cache_diff_k': nrm((DEPTH, n_pool, PAGE_SIZE, DIFF_HEADS, 2 * DIFF_DK)),
        'cache_diff_v': nrm((DEPTH, n_pool, PAGE_SIZE, DIFF_HEADS, DIFF_DV)),
        'cache_mem_k': nrm((DEPTH, DEC_BATCH, MEM_TOKENS, MEM_HEADS, MEM_HD)),
        'cache_mem_v': nrm((DEPTH, DEC_BATCH, MEM_TOKENS, MEM_HEADS, MEM_HD)),
        'state_conv': nrm((DEPTH, DEC_BATCH, CONV_W - 1, CONV_DIM)),
        'state_ssm': nrm((DEPTH, DEC_BATCH, SSM_HEADS, SSM_HEAD_DIM, D_STATE), 0.1),
        'page_table': page_table,
        'norm_mix': gain(D_MODEL),
        'w_in': nrm((DEPTH, D_MODEL, N_IN), D_MODEL ** -0.5),
        'conv_w': nrm((DEPTH, CONV_W, CONV_DIM), CONV_W ** -0.5),
        'conv_b': nrm((DEPTH, CONV_DIM), 0.02),
        'dt_bias': dt_bias,
        'a_log': a_log,
        'd_skip': gain(SSM_HEADS),
        'g_ssm': gain(D_SSM),
        'g_q': gain(DIFF_DK),
        'g_k': gain(DIFF_DK),
        'lambda_q1': nrm((DEPTH, DIFF_DK), 0.1),
        'lambda_k1': nrm((DEPTH, DIFF_DK), 0.1),
        'lambda_q2': nrm((DEPTH, DIFF_DK), 0.1),
        'lambda_k2': nrm((DEPTH, DIFF_DK), 0.1),
        'g_subln': gain(DIFF_DV),
        'norm_mem': gain(D_MODEL),
        'w_mem_kv': nrm((DEPTH, D_MODEL, 2 * D_MEMX), D_MODEL ** -0.5),
        'g_mq': gain(MEM_HD),
        'g_mk': gain(MEM_HD),
        'w_out': nrm((DEPTH, D_MIX, D_MODEL), D_MIX ** -0.5),
        'norm_mlp': gain(D_MODEL),
        'w_up': nrm((DEPTH, D_MODEL, D_FF), D_MODEL ** -0.5),
        'w_down': nrm((DEPTH, D_FF, D_MODEL), D_FF ** -0.5),
    }


def reference(x_prompt, x_sample, mem_prompt, cache_diff_k, cache_diff_v, cache_mem_k, cache_mem_v,
              state_conv, state_ssm, page_table, norm_mix, w_in, conv_w, conv_b, dt_bias, a_log, d_skip,
              g_ssm, g_q, g_k, lambda_q1, lambda_k1, lambda_q2, lambda_k2, g_subln, norm_mem, w_mem_kv,
              g_mq, g_mk, w_out, norm_mlp, w_up, w_down):
    Bp = x_prompt.shape[0]
    Bd = x_sample.shape[0]
    past = page_table.shape[1] * cache_diff_k.shape[2]
    xp, xs = x_prompt, x_sample
    kp_l, vp_l, ks_l, vs_l, mkp_l, mvp_l, cp_l, hp_l, cs_l, hs_l = ([] for _ in range(10))
    for l in range(DEPTH):
        lam, lam_init = diff_lambda(lambda_q1[l], lambda_k1[l], lambda_q2[l], lambda_k2[l], l)
        ssm_w = (conv_w[l], conv_b[l], dt_bias[l], a_log[l], d_skip[l], g_ssm[l])
        mlp_w = (g_subln[l], w_out[l], norm_mlp[l], w_up[l], w_down[l])

        z, xbc, dt_raw, q, k, v, mq = project(xp, norm_mix[l], w_in[l])
        q, k, v, mq = attn_views(q, k, v, mq, g_q[l], g_k[l], g_mq[l])
        conv0 = jnp.zeros((Bp, CONV_W - 1, CONV_DIM), xp.dtype)
        h0 = jnp.zeros((Bp, SSM_HEADS, SSM_HEAD_DIM, D_STATE), jnp.float32)
        y_ssm, conv_p, h_p = ssm_group(z, xbc, dt_raw, conv0, h0, *ssm_w)
        o_diff = diff_attn_prompt(q, k, v, lam)
        mk, mv = mem_kv(mem_prompt, norm_mem[l], w_mem_kv[l], g_mk[l])
        o_mem = mem_attn(mq, mk, mv)
        xp = finish(xp, y_ssm, o_diff, o_mem, lam_init, *mlp_w)
        kp_l.append(k.reshape(Bp, -1, DIFF_HEADS, 2 * DIFF_DK))
        vp_l.append(v)
        mkp_l.append(mk)
        mvp_l.append(mv)
        cp_l.append(conv_p)
        hp_l.append(h_p)

        z, xbc, dt_raw, q, k, v, mq = project(xs, norm_mix[l], w_in[l])
        q, k, v, mq = attn_views(q, k, v, mq, g_q[l], g_k[l], g_mq[l])
        y_ssm, conv_s, h_s = ssm_group(z, xbc, dt_raw, state_conv[l], state_ssm[l], *ssm_w)
        k_past = cache_diff_k[l][page_table].reshape(Bd, past, DIFF_HEADS, 2, DIFF_DK)
        v_past = cache_diff_v[l][page_table].reshape(Bd, past, DIFF_HEADS, DIFF_DV)
        o_diff = diff_attn_sample(q, k, v, k_past, v_past, lam)
        o_mem = mem_attn(mq, cache_mem_k[l], cache_mem_v[l])
        xs = finish(xs, y_ssm, o_diff, o_mem, lam_init, *mlp_w)
        ks_l.append(k.reshape(Bd, -1, DIFF_HEADS, 2 * DIFF_DK))
        vs_l.append(v)
        cs_l.append(conv_s)
        hs_l.append(h_s)

    return (xp, xs, jnp.stack(kp_l), jnp.stack(vp_l), jnp.stack(ks_l), jnp.stack(vs_l),
            jnp.stack(mkp_l), jnp.stack(mvp_l), jnp.stack(cp_l), jnp.stack(hp_l),
            jnp.stack(cs_l), jnp.stack(hs_l))
```

```python
import functools
import math

import jax
import jax.numpy as jnp
from jax import lax
from jax.experimental import pallas as pl
from jax.experimental.pallas import tpu as pltpu

F32 = jnp.float32
BF16 = jnp.bfloat16

D_MODEL = 2048
D_SSM = 1024
D_DIFF = 512
D_MEMX = 512
SSM_HEADS = 16
SSM_HEAD_DIM = 64
SSM_GROUPS = 2
D_STATE = 128
CONV_W = 4
BC_DIM = 2 * SSM_GROUPS * D_STATE
CONV_DIM = D_SSM + BC_DIM
DIFF_HEADS = 4
DIFF_DK = 64
MEM_HEADS = 4
MEM_HD = 128
MEM_TOKENS = 256
D_FF = 4 * D_MODEL
EPS = 1e-6

LANES = 128
SUBLANES_BF16 = 16
VMEM_LIMIT_BYTES = 56 * 1024 * 1024

N_MAIN = D_SSM + D_SSM + BC_DIM + 3 * D_DIFF + D_MEMX
COL_XS, COL_Z, COL_BC, COL_Q, COL_K, COL_V, COL_MQ = 0, 1024, 2048, 2560, 3072, 3584, 4096
PROJ_TN = 512
SSD_T = 128
ROWS_S = SUBLANES_BF16


def _cparams(sem):
    return pltpu.CompilerParams(dimension_semantics=sem, vmem_limit_bytes=VMEM_LIMIT_BYTES)


def _dot(a, b):
    return jnp.dot(a, b, preferred_element_type=F32)


def _dot_nt(a, b):
    return lax.dot_general(a, b, (((1,), (1,)), ((), ())), preferred_element_type=F32)


def _silu(x):
    return x * (1.0 / (1.0 + jnp.exp(-x)))


def _group_rmsnorm_slab(a, gain, group):
    sq = a * a
    if group == LANES:
        r = lax.rsqrt(jnp.sum(sq, axis=-1, keepdims=True) * (1.0 / LANES) + EPS)
    else:
        lane = lax.broadcasted_iota(jnp.int32, a.shape, 1)
        lo = lane < group
        s_lo = jnp.sum(jnp.where(lo, sq, 0.0), axis=-1, keepdims=True)
        s_hi = jnp.sum(jnp.where(lo, 0.0, sq), axis=-1, keepdims=True)
        r = jnp.where(lo, lax.rsqrt(s_lo * (1.0 / group) + EPS),
                      lax.rsqrt(s_hi * (1.0 / group) + EPS))
    return a * r * gain


def _proj_kernel(*refs, modes, has_dt):
    if has_dt:
        x_ref, g_ref, w_ref, gain_ref, wdt_ref, u_ref, dt_ref, h_ref = refs
    else:
        x_ref, g_ref, w_ref, gain_ref, u_ref, h_ref = refs
    j = pl.program_id(1)

    @pl.when(j == 0)
    def _():
        x = x_ref[...]
        r = lax.rsqrt(jnp.mean(x * x, axis=-1, keepdims=True) + EPS)
        h_ref[...] = (x * r * g_ref[...]).astype(BF16)
        if has_dt:
            dt_ref[...] = _dot(h_ref[...], wdt_ref[...])

    acc = _dot(h_ref[...], w_ref[...])
    for mode in sorted(set(modes)):
        cond = None
        for jj, m in enumerate(modes):
            if m == mode:
                c = j == jj
                cond = c if cond is None else jnp.logical_or(cond, c)

        @pl.when(cond)
        def _(mode=mode):
            if mode == 0:
                u_ref[...] = acc
            else:
                gain = gain_ref[...]
                for s in range(acc.shape[1] // LANES):
                    sl = slice(s * LANES, (s + 1) * LANES)
                    u_ref[:, sl] = _group_rmsnorm_slab(acc[:, sl], gain[:, sl], mode)


def _norm_proj(x, g, w, gains, modes, w_dt=None, tm=1024):
    m, k = x.shape
    n = w.shape[1]
    tm = min(tm, m)
    assert m % tm == 0 and n == PROJ_TN * len(modes)
    has_dt = w_dt is not None
    in_specs = [
        pl.BlockSpec((tm, k), lambda i, j: (i, 0)),
        pl.BlockSpec((1, k), lambda i, j: (0, 0)),
        pl.BlockSpec((k, PROJ_TN), lambda i, j: (0, j)),
        pl.BlockSpec((1, PROJ_TN), lambda i, j: (0, j)),
    ]
    args = [x, g, w, gains]
    out_shape = [jax.ShapeDtypeStruct((m, n), F32)]
    out_specs = [pl.BlockSpec((tm, PROJ_TN), lambda i, j: (i, j))]
    if has_dt:
        in_specs.append(pl.BlockSpec((k, LANES), lambda i, j: (0, 0)))
        args.append(w_dt)
        out_shape.append(jax.ShapeDtypeStruct((m, LANES), F32))
        out_specs.append(pl.BlockSpec((tm, LANES), lambda i, j: (i, 0)))
    outs = pl.pallas_call(
        functools.partial(_proj_kernel, modes=tuple(modes), has_dt=has_dt),
        grid=(m // tm, len(modes)),
        in_specs=in_specs,
        out_specs=out_specs,
        out_shape=out_shape,
        scratch_shapes=[pltpu.VMEM((tm, k), BF16)],
        compiler_params=_cparams(("parallel", "arbitrary")),
        name="norm_proj",
    )(*args)
    return outs if has_dt else outs[0]


def _split3(x):
    hi = x.astype(BF16)
    r1 = x - hi.astype(F32)
    mid = r1.astype(BF16)
    lo = (r1 - mid.astype(F32)).astype(BF16)
    return hi, mid, lo


def _ssd_kernel(*refs, t_in, valid_last, nc, has_init):
    if has_init:
        (xs_ref, z_ref, bc_ref, dt_ref, cprev_ref, h0_ref, convw_ref, convb_ref, dtb_ref, alog_ref,
         dskip_ref, gssm_ref, y_ref, cout_ref, hout_ref, xpad, hst) = refs
    else:
        (xs_ref, z_ref, bc_ref, dt_ref, convw_ref, convb_ref, dtb_ref, alog_ref,
         dskip_ref, gssm_ref, y_ref, cout_ref, hout_ref, xpad, hst) = refs
    T = SSD_T
    c = pl.program_id(1)

    @pl.when(c == 0)
    def _():
        if has_init:
            xpad[0:8, :] = cprev_ref[0]
            hst[...] = h0_ref[0]
        else:
            xpad[0:8, :] = jnp.zeros((8, CONV_DIM), F32)
            hst[...] = jnp.zeros(hst.shape, F32)

    def rows(ref):
        v = ref[...]
        if t_in < T:
            v = jnp.concatenate([v, jnp.zeros((T - t_in, v.shape[1]), F32)], axis=0)
        return v

    xpad[8:8 + T, 0:D_SSM] = rows(xs_ref)
    xpad[8:8 + T, D_SSM:CONV_DIM] = rows(bc_ref)
    z = rows(z_ref)
    dt_raw = rows(dt_ref)

    convw = convw_ref[...]
    conv = convb_ref[...] + xpad[5:5 + T, :] * convw[0:1, :]
    for jtap in range(1, CONV_W):
        conv = conv + xpad[5 + jtap:5 + jtap + T, :] * convw[jtap:jtap + 1, :]
    xact = _silu(conv)

    row_i = lax.broadcasted_iota(jnp.int32, (T, LANES), 0)
    col_i = lax.broadcasted_iota(jnp.int32, (T, LANES), 1)
    xv = dt_raw + dtb_ref[...]
    dt = jnp.maximum(xv, 0.0) + jnp.log1p(jnp.exp(-jnp.abs(xv)))
    if valid_last < T:
        dt = jnp.where(row_i < valid_last, dt, 0.0)
    a_neg = -jnp.exp(alog_ref[...])
    adt = dt * a_neg

    tril = (row_i >= col_i)
    tril_bf = jnp.where(tril, 1.0, 0.0).astype(BF16)
    a_hi, a_mid, a_lo = _split3(adt)
    acum = _dot(tril_bf, a_hi) + _dot(tril_bf, a_mid) + _dot(tril_bf, a_lo)
    acum_t = acum.T
    dt_t = dt.T
    e_acum = jnp.exp(acum)
    a_last = acum[T - 1:T, :]
    w_state = jnp.exp(a_last - acum) * dt
    da_last = jnp.exp(a_last)

    lo_half = col_i < SSM_HEAD_DIM
    neg_big = jnp.float32(-1e30)

    def colb(tile, r):
        return jnp.broadcast_to(tile[:, r:r + 1], (T, LANES))

    y_slabs = []
    for g in range(SSM_GROUPS):
        b_g = xact[:, D_SSM + g * D_STATE:D_SSM + (g + 1) * D_STATE]
        c_g = xact[:, D_SSM + SSM_GROUPS * D_STATE + g * D_STATE:
                   D_SSM + SSM_GROUPS * D_STATE + (g + 1) * D_STATE]
        b_bf = b_g.astype(BF16)
        c_bf = c_g.astype(BF16)
        cb = _dot_nt(c_bf, b_bf)
        for pp in range(SSM_HEADS // SSM_GROUPS // 2):
            p = g * (SSM_HEADS // SSM_GROUPS // 2) + pp
            sl = slice(p * LANES, (p + 1) * LANES)
            xs_slab = xact[:, sl]
            y_acc = dskip_ref[:, sl] * xs_slab
            for hh in range(2):
                r = 2 * p + hh
                seg = colb(acum, r) - acum_t[r:r + 1, :]
                lmat = jnp.exp(jnp.where(tril, seg, neg_big))
                mr = (cb * lmat * dt_t[r:r + 1, :]).astype(BF16)
                xh = jnp.where(lo_half if hh == 0 else jnp.logical_not(lo_half), xs_slab, 0.0)
                y_acc = y_acc + _dot(mr, xh.astype(BF16))
            hpair = hst[sl, :]
            e_pair = jnp.where(lo_half, colb(e_acum, 2 * p), colb(e_acum, 2 * p + 1))
            y_acc = y_acc + e_pair * _dot_nt(c_bf, hpair.astype(BF16))
            w_pair = jnp.where(lo_half, colb(w_state, 2 * p), colb(w_state, 2 * p + 1))
            xw_t = (xs_slab * w_pair).T
            st = _dot(xw_t.astype(BF16), b_bf)
            da = jnp.concatenate(
                [jnp.broadcast_to(da_last[:, 2 * p:2 * p + 1], (SSM_HEAD_DIM, LANES)),
                 jnp.broadcast_to(da_last[:, 2 * p + 1:2 * p + 2], (SSM_HEAD_DIM, LANES))], axis=0)
            hst[sl, :] = da * hpair + st
            y_slabs.append(y_acc * _silu(z[:, sl]))

    per_group = D_SSM // SSM_GROUPS // LANES
    for g in range(SSM_GROUPS):
        slabs = y_slabs[g * per_group:(g + 1) * per_group]
        ssum = jnp.sum(slabs[0] * slabs[0], axis=-1, keepdims=True)
        for s in slabs[1:]:
            ssum = ssum + jnp.sum(s * s, axis=-1, keepdims=True)
        r = lax.rsqrt(ssum * (1.0 / (per_group * LANES)) + EPS)
        for k, s in enumerate(slabs):
            sl = slice((g * per_group + k) * LANES, (g * per_group + k + 1) * LANES)
            y_ref[:, sl] = (s * r * gssm_ref[:, sl])[0:t_in, :].astype(y_ref.dtype)

    if nc > 1:
        xpad[5:8, :] = xpad[5 + T:8 + T, :]

    @pl.when(c == nc - 1)
    def _():
        if nc > 1:
            cout_ref[0, 0:CONV_W - 1, :] = xpad[5:8, :]
        else:
            cout_ref[0, 0:CONV_W - 1, :] = xpad[5 + valid_last:8 + valid_last, :]
        hout_ref[0] = hst[...]


def _ssd(u, dt, n_batch, seq_rows, valid_len, ssm_p, conv_prev=None, h0=None):
    t_in = min(seq_rows, SSD_T)
    nc = max(seq_rows // SSD_T, 1)
    valid_last = valid_len - (nc - 1) * SSD_T
    has_init = conv_prev is not None
    convw, convb, dtb, alog, dskip, gssm = ssm_p
    ntb = N_MAIN // PROJ_TN

    def row(b, c):
        return b * nc + c

    in_specs = [
        pl.BlockSpec((t_in, D_SSM), lambda b, c: (row(b, c), COL_XS // D_SSM)),
        pl.BlockSpec((t_in, D_SSM), lambda b, c: (row(b, c), COL_Z // D_SSM)),
        pl.BlockSpec((t_in, BC_DIM), lambda b, c: (row(b, c), COL_BC // BC_DIM)),
        pl.BlockSpec((t_in, LANES), lambda b, c: (row(b, c), 0)),
    ]
    args = [u, u, u, dt]
    if has_init:
        in_specs += [pl.BlockSpec((1, 8, CONV_DIM), lambda b, c: (b, 0, 0)),
                     pl.BlockSpec((1, D_SSM, D_STATE), lambda b, c: (b, 0, 0))]
        args += [conv_prev, h0]
    in_specs += [
        pl.BlockSpec((8, CONV_DIM), lambda b, c: (0, 0)),
        pl.BlockSpec((1, CONV_DIM), lambda b, c: (0, 0)),
        pl.BlockSpec((1, LANES), lambda b, c: (0, 0)),
        pl.BlockSpec((1, LANES), lambda b, c: (0, 0)),
        pl.BlockSpec((1, D_SSM), lambda b, c: (0, 0)),
        pl.BlockSpec((1, D_SSM), lambda b, c: (0, 0)),
    ]
    args += [convw, convb, dtb, alog, dskip, gssm]
    del ntb
    return pl.pallas_call(
        functools.partial(_ssd_kernel, t_in=t_in, valid_last=valid_last, nc=nc, has_init=has_init),
        grid=(n_batch, nc),
        in_specs=in_specs,
        out_specs=[
            pl.BlockSpec((t_in, D_SSM), lambda b, c: (row(b, c), 0)),
            pl.BlockSpec((1, 8, CONV_DIM), lambda b, c: (b, 0, 0)),
            pl.BlockSpec((1, D_SSM, D_STATE), lambda b, c: (b, 0, 0)),
        ],
        out_shape=[
            jax.ShapeDtypeStruct((n_batch * seq_rows, D_SSM), BF16),
            jax.ShapeDtypeStruct((n_batch, 8, CONV_DIM), F32),
            jax.ShapeDtypeStruct((n_batch, D_SSM, D_STATE), F32),
        ],
        scratch_shapes=[pltpu.VMEM((8 + SSD_T, CONV_DIM), F32), pltpu.VMEM((D_SSM, D_STATE), F32)],
        compiler_params=_cparams(("parallel", "arbitrary")),
        name="ssd_scan",
    )(*args)


def _lambda_from(lam_ref, lam_init):
    lp = lam_ref[...]
    s1 = jnp.sum(lp[0:1, :] * lp[1:2, :], axis=-1, keepdims=True)
    s2 = jnp.sum(lp[2:3, :] * lp[3:4, :], axis=-1, keepdims=True)
    return jnp.exp(s1) - jnp.exp(s2) + lam_init


def _attn_prompt_kernel(q_ref, k_ref, v_ref, mq_ref, mk_ref, mv_ref, lam_ref, gsub_ref,
                        od_ref, om_ref, kb, vb, m_s, l_s, acc_s, *, tq, lam_init):
    qi = pl.program_id(2)

    @pl.when(qi == 0)
    def _():
        kb[...] = k_ref[...].astype(BF16)
        vb[...] = v_ref[...].astype(BF16)

    lane = lax.broadcasted_iota(jnp.int32, (tq, LANES), 1)
    lo = lane < DIFF_DK
    q = q_ref[...] * (DIFF_DK ** -0.5)
    qm = (jnp.where(lo, q, 0.0).astype(BF16), jnp.where(lo, 0.0, q).astype(BF16))
    m_s[...] = jnp.full(m_s.shape, -jnp.inf, F32)
    l_s[...] = jnp.zeros(l_s.shape, F32)
    acc_s[...] = jnp.zeros(acc_s.shape, F32)
    r_i = lax.broadcasted_iota(jnp.int32, (tq, tq), 0)
    c_i = lax.broadcasted_iota(jnp.int32, (tq, tq), 1)

    def step(j, masked):
        start = pl.multiple_of(j * tq, tq)
        k_blk = kb[pl.ds(start, tq), :]
        v_blk = vb[pl.ds(start, tq), :]
        for mi in range(2):
            s = _dot_nt(qm[mi], k_blk)
            if masked:
                s = jnp.where(r_i >= c_i, s, -jnp.inf)
            m_prev = m_s[mi]
            m_new = jnp.maximum(m_prev, jnp.max(s, axis=-1, keepdims=True))
            alpha = jnp.exp(m_prev - m_new)
            p = jnp.exp(s - m_new)
            l_s[mi] = alpha * l_s[mi] + jnp.sum(p, axis=-1, keepdims=True)
            acc_s[mi] = alpha * acc_s[mi] + _dot(p.astype(BF16), v_blk)
            m_s[mi] = m_new

    def body(j, carry):
        step(j, False)
        return carry

    lax.fori_loop(0, qi, body, 0)
    step(qi, True)

    lam = _lambda_from(lam_ref, lam_init)
    o = acc_s[0] / l_s[0] - lam * (acc_s[1] / l_s[1])
    r = lax.rsqrt(jnp.mean(o * o, axis=-1, keepdims=True) + EPS)
    od_ref[...] = ((o * r * gsub_ref[...]) * (1.0 - lam_init)).astype(od_ref.dtype)

    s = _dot_nt(mq_ref[...].astype(BF16), mk_ref[...].astype(BF16)) * (MEM_HD ** -0.5)
    e = jnp.exp(s - jnp.max(s, axis=-1, keepdims=True))
    om = _dot(e.astype(BF16), mv_ref[...].astype(BF16)) / jnp.sum(e, axis=-1, keepdims=True)
    om_ref[...] = om.astype(om_ref.dtype)


def _attn_prompt(u, mkv, lam_pack, gsub, n_batch, seq, lam_init, tq=256):
    nq = seq // tq
    cq, ck, cv, cmq = (COL_Q // LANES, COL_K // LANES, COL_V // LANES, COL_MQ // LANES)
    return pl.pallas_call(
        functools.partial(_attn_prompt_kernel, tq=tq, lam_init=lam_init),
        grid=(n_batch, DIFF_HEADS, nq),
        in_specs=[
            pl.BlockSpec((tq, LANES), lambda b, h, i: (b * nq + i, cq + h)),
            pl.BlockSpec((seq, LANES), lambda b, h, i: (b, ck + h)),
            pl.BlockSpec((seq, LANES), lambda b, h, i: (b, cv + h)),
            pl.BlockSpec((tq, LANES), lambda b, h, i: (b * nq + i, cmq + h)),
            pl.BlockSpec((MEM_TOKENS, LANES), lambda b, h, i: (b, h)),
            pl.BlockSpec((MEM_TOKENS, LANES), lambda b, h, i: (b, MEM_HEADS + h)),
            pl.BlockSpec((8, LANES), lambda b, h, i: (0, 0)),
            pl.BlockSpec((1, LANES), lambda b, h, i: (0, 0)),
        ],
        out_specs=[
            pl.BlockSpec((tq, LANES), lambda b, h, i: (b * nq + i, h)),
            pl.BlockSpec((tq, LANES), lambda b, h, i: (b * nq + i, h)),
        ],
        out_shape=[
            jax.ShapeDtypeStruct((n_batch * seq, D_DIFF), BF16),
            jax.ShapeDtypeStruct((n_batch * seq, D_MEMX), BF16),
        ],
        scratch_shapes=[
            pltpu.VMEM((seq, LANES), BF16),
            pltpu.VMEM((seq, LANES), BF16),
            pltpu.VMEM((2, tq, 1), F32),
            pltpu.VMEM((2, tq, 1), F32),
            pltpu.VMEM((2, tq, LANES), F32),
        ],
        compiler_params=_cparams(("parallel", "parallel", "arbitrary")),
        name="attn_prompt",
    )(u, u, u, u, mkv, mkv, lam_pack, gsub)


def _head_rows(x_row, n_maps):
    r_i = lax.broadcasted_iota(jnp.int32, (8, DIFF_HEADS * LANES), 0)
    c_i = lax.broadcasted_iota(jnp.int32, (8, DIFF_HEADS * LANES), 1)
    head_c = jnp.right_shift(c_i, 7)
    if n_maps == 2:
        half_c = jnp.bitwise_and(jnp.right_shift(c_i, 6), 1)
        keep = jnp.logical_and(head_c == jnp.right_shift(r_i, 1), half_c == jnp.bitwise_and(r_i, 1))
    else:
        keep = head_c == r_i
    return jnp.where(keep, jnp.broadcast_to(x_row, (8, DIFF_HEADS * LANES)), 0.0)


def _decode_kernel(pt_ref, tok_q, tok_k, tok_v, tok_mq, mk_ref, mv_ref, lam_ref, gsub_ref, *rest,
                   npp, lam_init):
    del pt_ref
    k_refs = rest[:npp]
    v_refs = rest[npp:2 * npp]
    od_ref, om_ref, m_s, l_s, acc_s = rest[2 * npp:]
    j = pl.program_id(1)
    nj = pl.num_programs(1)
    W = DIFF_HEADS * LANES

    q_row = tok_q[0:1, :] * (DIFF_DK ** -0.5)
    qmat = _head_rows(q_row, 2)
    qmat_bf = qmat.astype(BF16)

    @pl.when(j == 0)
    def _():
        s_new = jnp.sum(qmat * tok_k[0:1, :], axis=-1, keepdims=True)
        m_s[...] = s_new
        l_s[...] = jnp.ones(l_s.shape, F32)
        acc_s[...] = jnp.broadcast_to(tok_v[0:1, :], (8, W))
        mqm = _head_rows(tok_mq[0:1, :], 1)
        s = _dot_nt(mqm.astype(BF16), mk_ref[0].astype(BF16)) * (MEM_HD ** -0.5)
        e = jnp.exp(s - jnp.max(s, axis=-1, keepdims=True))
        om = _dot(e.astype(BF16), mv_ref[0].astype(BF16)) / jnp.sum(e, axis=-1, keepdims=True)
        om_row = jnp.concatenate(
            [om[h:h + 1, h * LANES:(h + 1) * LANES] for h in range(MEM_HEADS)], axis=1)
        om_ref[...] = jnp.broadcast_to(om_row, om_ref.shape)

    s_all = jnp.concatenate([_dot_nt(qmat_bf, k_refs[i][...].astype(BF16)) for i in range(npp)], axis=1)
    m_prev = m_s[...]
    m_new = jnp.maximum(m_prev, jnp.max(s_all, axis=-1, keepdims=True))
    alpha = jnp.exp(m_prev - m_new)
    p = jnp.exp(s_all - m_new)
    l_s[...] = alpha * l_s[...] + jnp.sum(p, axis=-1, keepdims=True)
    p_bf = p.astype(BF16)
    pv = _dot(p_bf[:, 0:PAGE], v_refs[0][...].astype(BF16))
    for i in range(1, npp):
        pv = pv + _dot(p_bf[:, i * PAGE:(i + 1) * PAGE], v_refs[i][...].astype(BF16))
    acc_s[...] = alpha * acc_s[...] + pv
    m_s[...] = m_new

    @pl.when(j == nj - 1)
    def _():
        lam = _lambda_from(lam_ref, lam_init)
        o_all = acc_s[...] / l_s[...]
        outs = []
        for h in range(DIFF_HEADS):
            sl = slice(h * LANES, (h + 1) * LANES)
            o = o_all[2 * h:2 * h + 1, sl] - lam * o_all[2 * h + 1:2 * h + 2, sl]
            r = lax.rsqrt(jnp.mean(o * o, axis=-1, keepdims=True) + EPS)
            outs.append((o * r * gsub_ref[...]) * (1.0 - lam_init))
        od_ref[...] = jnp.broadcast_to(jnp.concatenate(outs, axis=1), od_ref.shape)


PAGE = 128


def _attn_decode(u_s, cache_k, cache_v, page_table, mem_k, mem_v, lam_pack, gsub, layer, lam_init, npp=16):
    n_b, n_pages = page_table.shape
    assert n_pages % npp == 0
    W = DIFF_HEADS * LANES
    cq, ck, cv, cmq = (COL_Q // W, COL_K // W, COL_V // W, COL_MQ // W)

    def tok(col):
        return pl.BlockSpec((ROWS_S, W), lambda b, j, pt: (b, col))

    def page(i):
        return pl.BlockSpec((None, None, PAGE, W),
                            lambda b, j, pt, i=i: (layer, pt[b, j * npp + i], 0, 0))

    in_specs = [tok(cq), tok(ck), tok(cv), tok(cmq),
                pl.BlockSpec((None, 1, MEM_TOKENS, W), lambda b, j, pt: (layer, b, 0, 0)),
                pl.BlockSpec((None, 1, MEM_TOKENS, W), lambda b, j, pt: (layer, b, 0, 0)),
                pl.BlockSpec((8, LANES), lambda b, j, pt: (0, 0)),
                pl.BlockSpec((1, LANES), lambda b, j, pt: (0, 0))]
    in_specs += [page(i) for i in range(npp)] * 2
    grid_spec = pltpu.PrefetchScalarGridSpec(
        num_scalar_prefetch=1,
        grid=(n_b, n_pages // npp),
        in_specs=in_specs,
        out_specs=[pl.BlockSpec((8, W), lambda b, j, pt: (b, 0)),
                   pl.BlockSpec((8, W), lambda b, j, pt: (b, 0))],
        scratch_shapes=[pltpu.VMEM((8, 1), F32), pltpu.VMEM((8, 1), F32), pltpu.VMEM((8, W), F32)],
    )
    od, om = pl.pallas_call(
        functools.partial(_decode_kernel, npp=npp, lam_init=lam_init),
        grid_spec=grid_spec,
        out_shape=[jax.ShapeDtypeStruct((n_b * 8, W), F32), jax.ShapeDtypeStruct((n_b * 8, W), F32)],
        compiler_params=_cparams(("parallel", "arbitrary")),
        name="attn_decode",
    )(page_table, u_s, u_s, u_s, u_s, mem_k, mem_v, lam_pack, gsub,
      *([cache_k] * npp), *([cache_v] * npp))
    return od.reshape(n_b, 8, W)[:, 0], om.reshape(n_b, 8, W)[:, 0]


def _outproj_kernel(x_ref, y_ref, od_ref, om_ref, w1_ref, w2_ref, w3_ref, o_ref):
    o_ref[...] = (x_ref[...]
                  + _dot(y_ref[...].astype(BF16), w1_ref[...])
                  + _dot(od_ref[...].astype(BF16), w2_ref[...])
                  + _dot(om_ref[...].astype(BF16), w3_ref[...]))


def _outproj(x, y, od, om, w_out, tm=1024, tn=512):
    m = x.shape[0]
    tm = min(tm, m)
    assert m % tm == 0
    return pl.pallas_call(
        _outproj_kernel,
        grid=(m // tm, D_MODEL // tn),
        in_specs=[
            pl.BlockSpec((tm, tn), lambda i, j: (i, j)),
            pl.BlockSpec((tm, D_SSM), lambda i, j: (i, 0)),
            pl.BlockSpec((tm, D_DIFF), lambda i, j: (i, 0)),
            pl.BlockSpec((tm, D_MEMX), lambda i, j: (i, 0)),
            pl.BlockSpec((D_SSM, tn), lambda i, j: (0, j)),
            pl.BlockSpec((D_DIFF, tn), lambda i, j: (D_SSM // D_DIFF, j)),
            pl.BlockSpec((D_MEMX, tn), lambda i, j: ((D_SSM + D_DIFF) // D_MEMX, j)),
        ],
        out_specs=pl.BlockSpec((tm, tn), lambda i, j: (i, j)),
        out_shape=jax.ShapeDtypeStruct((m, D_MODEL), F32),
        compiler_params=_cparams(("parallel", "arbitrary")),
        name="out_proj",
    )(x, y, od, om, w_out, w_out, w_out)


def _mlp_kernel(x_ref, g_ref, wu_ref, wd_ref, o_ref, h_ref):
    f = pl.program_id(1)

    @pl.when(f == 0)
    def _():
        x = x_ref[...]
        r = lax.rsqrt(jnp.mean(x * x, axis=-1, keepdims=True) + EPS)
        h_ref[...] = (x * r * g_ref[...]).astype(BF16)
        o_ref[...] = x

    a = jnp.maximum(_dot(h_ref[...], wu_ref[...]), 0.0)
    o_ref[...] += _dot((a * a).astype(BF16), wd_ref[...])


def _mlp(x, g, w_up, w_down, tm=512, tf=1024):
    m = x.shape[0]
    tm = min(tm, m)
    assert m % tm == 0
    return pl.pallas_call(
        _mlp_kernel,
        grid=(m // tm, D_FF // tf),
        in_specs=[
            pl.BlockSpec((tm, D_MODEL), lambda i, f: (i, 0)),
            pl.BlockSpec((1, D_MODEL), lambda i, f: (0, 0)),
            pl.BlockSpec((D_MODEL, tf), lambda i, f: (0, f)),
            pl.BlockSpec((tf, D_MODEL), lambda i, f: (f, 0)),
        ],
        out_specs=pl.BlockSpec((tm, D_MODEL), lambda i, f: (i, 0)),
        out_shape=jax.ShapeDtypeStruct((m, D_MODEL), F32),
        scratch_shapes=[pltpu.VMEM((tm, D_MODEL), BF16)],
        compiler_params=_cparams(("parallel", "arbitrary")),
        name="mlp",
    )(x, g, w_up, w_down)


def _pad_lanes(v, width=LANES):
    v = v.reshape(1, -1).astype(F32)
    return jnp.pad(v, ((0, 0), (0, width - v.shape[1])))


def kernel(x_prompt, x_sample, mem_prompt, cache_diff_k, cache_diff_v, cache_mem_k, cache_mem_v, state_conv, state_ssm, page_table, norm_mix, w_in, conv_w, conv_b, dt_bias, a_log, d_skip, g_ssm, g_q, g_k, lambda_q1, lambda_k1, lambda_q2, lambda_k2, g_subln, norm_mem, w_mem_kv, g_mq, g_mk, w_out, norm_mlp, w_up, w_down):
    depth = w_in.shape[0]
    bp, seq, _ = x_prompt.shape
    bd, dec_seq, _ = x_sample.shape
    assert dec_seq == 1 and seq % SSD_T == 0
    n_pool = cache_diff_k.shape[1]
    W = DIFF_HEADS * LANES

    xp = x_prompt.reshape(bp * seq, D_MODEL)
    xs = x_sample.reshape(bd, D_MODEL)
    cache_k = cache_diff_k.reshape(depth, n_pool, PAGE, W)
    cache_v = cache_diff_v.reshape(depth, n_pool, PAGE, W)
    mem_k = cache_mem_k.reshape(depth, bd, MEM_TOKENS, W)
    mem_v = cache_mem_v.reshape(depth, bd, MEM_TOKENS, W)

    outs = [[] for _ in range(10)]
    for l in range(depth):
        lam_init = 0.8 - 0.6 * math.exp(-0.3 * l)
        wl = w_in[l]
        o_z, o_x, o_bc, o_dt = 0, D_SSM, 2 * D_SSM, D_SSM + CONV_DIM
        o_q = o_dt + SSM_HEADS
        w_main = jnp.concatenate(
            [wl[:, o_x:o_bc], wl[:, o_z:o_x], wl[:, o_bc:o_dt], wl[:, o_q:]], axis=1).astype(BF16)
        w_dt = jnp.pad(wl[:, o_dt:o_q], ((0, 0), (0, LANES - SSM_HEADS))).astype(BF16)
        ones = jnp.ones((PROJ_TN,), F32)
        gains = jnp.concatenate(
            [ones] * 5 + [jnp.tile(g_q[l], 2 * DIFF_HEADS), jnp.tile(g_k[l], 2 * DIFF_HEADS), ones,
                          jnp.tile(g_mq[l], MEM_HEADS)]).reshape(1, N_MAIN)
        modes = (0, 0, 0, 0, 0, DIFF_DK, DIFF_DK, 0, MEM_HD)
        g_mix = norm_mix[l].reshape(1, D_MODEL)

        ssm_p = (jnp.pad(conv_w[l], ((0, 8 - CONV_W), (0, 0))),
                 conv_b[l].reshape(1, CONV_DIM),
                 _pad_lanes(dt_bias[l]), _pad_lanes(a_log[l]),
                 jnp.repeat(d_skip[l], SSM_HEAD_DIM).reshape(1, D_SSM),
                 g_ssm[l].reshape(1, D_SSM))
        lam_pack = jnp.concatenate(
            [_pad_lanes(lambda_q1[l]), _pad_lanes(lambda_k1[l]), _pad_lanes(lambda_q2[l]),
             _pad_lanes(lambda_k2[l]), jnp.zeros((4, LANES), F32)], axis=0)
        gsub = g_subln[l].reshape(1, LANES)
        w_out_bf = w_out[l].astype(BF16)
        w_up_bf = w_up[l].astype(BF16)
        w_down_bf = w_down[l].astype(BF16)
        g_mlp = norm_mlp[l].reshape(1, D_MODEL)

        u_p, dt_p = _norm_proj(xp, g_mix, w_main, gains, modes, w_dt=w_dt)
        y_p, conv_p, h_p = _ssd(u_p, dt_p, bp, seq, seq, ssm_p)
        mem_gains = jnp.concatenate([jnp.tile(g_mk[l], MEM_HEADS), ones]).reshape(1, 2 * D_MEMX)
        mkv = _norm_proj(mem_prompt.reshape(bp * MEM_TOKENS, D_MODEL), norm_mem[l].reshape(1, D_MODEL),
                         w_mem_kv[l].astype(BF16), mem_gains, (MEM_HD, 0))
        od_p, om_p = _attn_prompt(u_p, mkv, lam_pack, gsub, bp, seq, lam_init)
        xp_mid = _outproj(xp, y_p, od_p, om_p, w_out_bf)
        xp_new = _mlp(xp_mid, g_mlp, w_up_bf, w_down_bf)

        outs[0].append(u_p[:, COL_K:COL_K + D_DIFF].reshape(bp, seq, DIFF_HEADS, LANES))
        outs[1].append(u_p[:, COL_V:COL_V + D_DIFF].reshape(bp, seq, DIFF_HEADS, LANES))
        outs[4].append(mkv[:, :D_MEMX].reshape(bp, MEM_TOKENS, MEM_HEADS, MEM_HD))
        outs[5].append(mkv[:, D_MEMX:].reshape(bp, MEM_TOKENS, MEM_HEADS, MEM_HD))
        outs[6].append(conv_p[:, :CONV_W - 1])
        outs[7].append(h_p.reshape(bp, SSM_HEADS, SSM_HEAD_DIM, D_STATE))

        xs_pad = jnp.pad(xs.reshape(bd, 1, D_MODEL), ((0, 0), (0, ROWS_S - 1), (0, 0))).reshape(bd * ROWS_S, D_MODEL)
        u_s, dt_s = _norm_proj(xs_pad, g_mix, w_main, gains, modes, w_dt=w_dt)
        conv_prev = jnp.pad(state_conv[l], ((0, 0), (8 - (CONV_W - 1), 0), (0, 0)))
        y_s, conv_s, h_s = _ssd(u_s, dt_s, bd, ROWS_S, 1, ssm_p, conv_prev=conv_prev,
                                h0=state_ssm[l].reshape(bd, D_SSM, D_STATE))
        od_s, om_s = _attn_decode(u_s, cache_k, cache_v, page_table, mem_k, mem_v, lam_pack, gsub, l, lam_init)
        y_s0 = y_s.reshape(bd, ROWS_S, D_SSM)[:, 0]
        xs_mid = _outproj(xs, y_s0, od_s, om_s, w_out_bf)
        xs_new = _mlp(xs_mid, g_mlp, w_up_bf, w_down_bf)
        u_s0 = u_s.reshape(bd, ROWS_S, N_MAIN)[:, 0]
        outs[2].append(u_s0[:, COL_K:COL_K + D_DIFF].reshape(bd, 1, DIFF_HEADS, LANES))
        outs[3].append(u_s0[:, COL_V:COL_V + D_DIFF].reshape(bd, 1, DIFF_HEADS, LANES))
        outs[8].append(conv_s[:, :CONV_W - 1])
        outs[9].append(h_s.reshape(bd, SSM_HEADS, SSM_HEAD_DIM, D_STATE))

        xp, xs = xp_new, xs_new

    st = [jnp.stack(o) for o in outs]
    return (xp.reshape(bp, seq, D_MODEL), xs.reshape(bd, 1, D_MODEL),
            st[0], st[1], st[2], st[3], st[4], st[5], st[6], st[7], st[8], st[9])
```

```python
import functools
import math

import jax
import jax.numpy as jnp
from jax import lax
from jax.experimental import pallas as pl
from jax.experimental.pallas import tpu as pltpu

F32 = jnp.float32
BF16 = jnp.bfloat16

D_MODEL = 2048
D_SSM = 1024
D_DIFF = 512
D_MEMX = 512
SSM_HEADS = 16
SSM_HEAD_DIM = 64
SSM_GROUPS = 2
D_STATE = 128
CONV_W = 4
BC_DIM = 2 * SSM_GROUPS * D_STATE
CONV_DIM = D_SSM + BC_DIM
DIFF_HEADS = 4
DIFF_DK = 64
MEM_HEADS = 4
MEM_HD = 128
MEM_TOKENS = 256
D_FF = 4 * D_MODEL
EPS = 1e-6

LANES = 128
SUBLANES_BF16 = 16
VMEM_LIMIT_BYTES = 56 * 1024 * 1024

N_MAIN = D_SSM + D_SSM + BC_DIM + 3 * D_DIFF + D_MEMX
COL_XS, COL_Z, COL_BC, COL_Q, COL_K, COL_V, COL_MQ = 0, 1024, 2048, 2560, 3072, 3584, 4096
PROJ_TN = 512
SSD_T = 128
ROWS_S = SUBLANES_BF16


def _cparams(sem):
    return pltpu.CompilerParams(dimension_semantics=sem, vmem_limit_bytes=VMEM_LIMIT_BYTES)


def _dot(a, b):
    return jnp.dot(a, b, preferred_element_type=F32)


def _dot_nt(a, b):
    return lax.dot_general(a, b, (((1,), (1,)), ((), ())), preferred_element_type=F32)


def _silu(x):
    return x * (1.0 / (1.0 + jnp.exp(-x)))


def _group_rmsnorm_slab(a, gain, group):
    sq = a * a
    if group == LANES:
        r = lax.rsqrt(jnp.sum(sq, axis=-1, keepdims=True) * (1.0 / LANES) + EPS)
    else:
        lane = lax.broadcasted_iota(jnp.int32, a.shape, 1)
        lo = lane < group
        s_lo = jnp.sum(jnp.where(lo, sq, 0.0), axis=-1, keepdims=True)
        s_hi = jnp.sum(jnp.where(lo, 0.0, sq), axis=-1, keepdims=True)
        r = jnp.where(lo, lax.rsqrt(s_lo * (1.0 / group) + EPS),
                      lax.rsqrt(s_hi * (1.0 / group) + EPS))
    return a * r * gain


def _proj_kernel(*refs, modes, has_dt):
    if has_dt:
        x_ref, g_ref, w_ref, gain_ref, wdt_ref, u_ref, dt_ref, h_ref = refs
    else:
        x_ref, g_ref, w_ref, gain_ref, u_ref, h_ref = refs
    j = pl.program_id(1)

    @pl.when(j == 0)
    def _():
        x = x_ref[...]
        r = lax.rsqrt(jnp.mean(x * x, axis=-1, keepdims=True) + EPS)
        h_ref[...] = (x * r * g_ref[...]).astype(BF16)
        if has_dt:
            dt_ref[...] = _dot(h_ref[...], wdt_ref[...])

    acc = _dot(h_ref[...], w_ref[...])
    for mode in sorted(set(modes)):
        cond = None
        for jj, m in enumerate(modes):
            if m == mode:
                c = j == jj
                cond = c if cond is None else jnp.logical_or(cond, c)

        @pl.when(cond)
        def _(mode=mode):
            if mode == 0:
                u_ref[...] = acc
            else:
                gain = gain_ref[...]
                for s in range(acc.shape[1] // LANES):
                    sl = slice(s * LANES, (s + 1) * LANES)
                    u_ref[:, sl] = _group_rmsnorm_slab(acc[:, sl], gain[:, sl], mode)


def _norm_proj(x, g, w, gains, modes, w_dt=None, tm=1024):
    m, k = x.shape
    n = w.shape[1]
    tm = min(tm, m)
    assert m % tm == 0 and n == PROJ_TN * len(modes)
    has_dt = w_dt is not None
    in_specs = [
        pl.BlockSpec((tm, k), lambda i, j: (i, 0)),
        pl.BlockSpec((1, k), lambda i, j: (0, 0)),
        pl.BlockSpec((k, PROJ_TN), lambda i, j: (0, j)),
        pl.BlockSpec((1, PROJ_TN), lambda i, j: (0, j)),
    ]
    args = [x, g, w, gains]
    out_shape = [jax.ShapeDtypeStruct((m, n), F32)]
    out_specs = [pl.BlockSpec((tm, PROJ_TN), lambda i, j: (i, j))]
    if has_dt:
        in_specs.append(pl.BlockSpec((k, LANES), lambda i, j: (0, 0)))
        args.append(w_dt)
        out_shape.append(jax.ShapeDtypeStruct((m, LANES), F32))
        out_specs.append(pl.BlockSpec((tm, LANES), lambda i, j: (i, 0)))
    outs = pl.pallas_call(
        functools.partial(_proj_kernel, modes=tuple(modes), has_dt=has_dt),
        grid=(m // tm, len(modes)),
        in_specs=in_specs,
        out_specs=out_specs,
        out_shape=out_shape,
        scratch_shapes=[pltpu.VMEM((tm, k), BF16)],
        compiler_params=_cparams(("parallel", "arbitrary")),
        name="norm_proj",
    )(*args)
    return outs if has_dt else outs[0]


def _split3(x):
    hi = x.astype(BF16)
    r1 = x - hi.astype(F32)
    mid = r1.astype(BF16)
    lo = (r1 - mid.astype(F32)).astype(BF16)
    return hi, mid, lo


def _ssd_kernel(*refs, t_in, valid_last, nc, has_init):
    if has_init:
        (xs_ref, z_ref, bc_ref, dt_ref, cprev_ref, h0_ref, convw_ref, convb_ref, dtb_ref, alog_ref,
         dskip_ref, gssm_ref, y_ref, cout_ref, hout_ref, xpad, hst) = refs
    else:
        (xs_ref, z_ref, bc_ref, dt_ref, convw_ref, convb_ref, dtb_ref, alog_ref,
         dskip_ref, gssm_ref, y_ref, cout_ref, hout_ref, xpad, hst) = refs
    T = SSD_T
    c = pl.program_id(1)

    @pl.when(c == 0)
    def _():
        if has_init:
            xpad[0:8, :] = cprev_ref[0]
            hst[...] = h0_ref[0]
        else:
            xpad[0:8, :] = jnp.zeros((8, CONV_DIM), F32)
            hst[...] = jnp.zeros(hst.shape, F32)

    def rows(ref):
        v = ref[...]
        if t_in < T:
            v = jnp.concatenate([v, jnp.zeros((T - t_in, v.shape[1]), F32)], axis=0)
        return v

    xpad[8:8 + T, 0:D_SSM] = rows(xs_ref)
    xpad[8:8 + T, D_SSM:CONV_DIM] = rows(bc_ref)
    z = rows(z_ref)
    dt_raw = rows(dt_ref)

    convw = convw_ref[...]
    conv = convb_ref[...] + xpad[5:5 + T, :] * convw[0:1, :]
    for jtap in range(1, CONV_W):
        conv = conv + xpad[5 + jtap:5 + jtap + T, :] * convw[jtap:jtap + 1, :]
    xact = _silu(conv)

    row_i = lax.broadcasted_iota(jnp.int32, (T, LANES), 0)
    col_i = lax.broadcasted_iota(jnp.int32, (T, LANES), 1)
    xv = dt_raw + dtb_ref[...]
    dt = jnp.maximum(xv, 0.0) + jnp.log1p(jnp.exp(-jnp.abs(xv)))
    if valid_last < T:
        dt = jnp.where(row_i < valid_last, dt, 0.0)
    a_neg = -jnp.exp(alog_ref[...])
    adt = dt * a_neg

    tril = (row_i >= col_i)
    tril_bf = jnp.where(tril, 1.0, 0.0).astype(BF16)
    a_hi, a_mid, a_lo = _split3(adt)
    acum = _dot(tril_bf, a_hi) + _dot(tril_bf, a_mid) + _dot(tril_bf, a_lo)
    acum_t = acum.T
    dt_t = dt.T
    e_acum = jnp.exp(acum)
    a_last = acum[T - 1:T, :]
    w_state = jnp.exp(a_last - acum) * dt
    da_last = jnp.exp(a_last)

    lo_half = col_i < SSM_HEAD_DIM
    neg_big = jnp.float32(-1e30)

    def colb(tile, r):
        return jnp.broadcast_to(tile[:, r:r + 1], (T, LANES))

    y_slabs = []
    for g in range(SSM_GROUPS):
        b_g = xact[:, D_SSM + g * D_STATE:D_SSM + (g + 1) * D_STATE]
        c_g = xact[:, D_SSM + SSM_GROUPS * D_STATE + g * D_STATE:
                   D_SSM + SSM_GROUPS * D_STATE + (g + 1) * D_STATE]
        b_bf = b_g.astype(BF16)
        c_bf = c_g.astype(BF16)
        cb = _dot_nt(c_bf, b_bf)
        for pp in range(SSM_HEADS // SSM_GROUPS // 2):
            p = g * (SSM_HEADS // SSM_GROUPS // 2) + pp
            sl = slice(p * LANES, (p + 1) * LANES)
            xs_slab = xact[:, sl]
            y_acc = dskip_ref[:, sl] * xs_slab
            for hh in range(2):
                r = 2 * p + hh
                seg = colb(acum, r) - acum_t[r:r + 1, :]
                lmat = jnp.exp(jnp.where(tril, seg, neg_big))
                mr = (cb * lmat * dt_t[r:r + 1, :]).astype(BF16)
                xh = jnp.where(lo_half if hh == 0 else jnp.logical_not(lo_half), xs_slab, 0.0)
                y_acc = y_acc + _dot(mr, xh.astype(BF16))
            hpair = hst[sl, :]
            e_pair = jnp.where(lo_half, colb(e_acum, 2 * p), colb(e_acum, 2 * p + 1))
            y_acc = y_acc + e_pair * _dot_nt(c_bf, hpair.astype(BF16))
            w_pair = jnp.where(lo_half, colb(w_state, 2 * p), colb(w_state, 2 * p + 1))
            xw_t = (xs_slab * w_pair).T
            st = _dot(xw_t.astype(BF16), b_bf)
            da = jnp.concatenate(
                [jnp.broadcast_to(da_last[:, 2 * p:2 * p + 1], (SSM_HEAD_DIM, LANES)),
                 jnp.broadcast_to(da_last[:, 2 * p + 1:2 * p + 2], (SSM_HEAD_DIM, LANES))], axis=0)
            hst[sl, :] = da * hpair + st
            y_slabs.append(y_acc * _silu(z[:, sl]))

    per_group = D_SSM // SSM_GROUPS // LANES
    for g in range(SSM_GROUPS):
        slabs = y_slabs[g * per_group:(g + 1) * per_group]
        ssum = jnp.sum(slabs[0] * slabs[0], axis=-1, keepdims=True)
        for s in slabs[1:]:
            ssum = ssum + jnp.sum(s * s, axis=-1, keepdims=True)
        r = lax.rsqrt(ssum * (1.0 / (per_group * LANES)) + EPS)
        for k, s in enumerate(slabs):
            sl = slice((g * per_group + k) * LANES, (g * per_group + k + 1) * LANES)
            y_ref[:, sl] = (s * r * gssm_ref[:, sl])[0:t_in, :].astype(y_ref.dtype)

    if nc > 1:
        xpad[5:8, :] = xpad[5 + T:8 + T, :]

    @pl.when(c == nc - 1)
    def _():
        if nc > 1:
            cout_ref[0, 0:CONV_W - 1, :] = xpad[5:8, :]
        else:
            cout_ref[0, 0:CONV_W - 1, :] = xpad[5 + valid_last:8 + valid_last, :]
        hout_ref[0] = hst[...]


def _ssd(u, dt, n_batch, seq_rows, valid_len, ssm_p, conv_prev=None, h0=None):
    t_in = min(seq_rows, SSD_T)
    nc = max(seq_rows // SSD_T, 1)
    valid_last = valid_len - (nc - 1) * SSD_T
    has_init = conv_prev is not None
    convw, convb, dtb, alog, dskip, gssm = ssm_p
    ntb = N_MAIN // PROJ_TN

    def row(b, c):
        return b * nc + c

    in_specs = [
        pl.BlockSpec((t_in, D_SSM), lambda b, c: (row(b, c), COL_XS // D_SSM)),
        pl.BlockSpec((t_in, D_SSM), lambda b, c: (row(b, c), COL_Z // D_SSM)),
        pl.BlockSpec((t_in, BC_DIM), lambda b, c: (row(b, c), COL_BC // BC_DIM)),
        pl.BlockSpec((t_in, LANES), lambda b, c: (row(b, c), 0)),
    ]
    args = [u, u, u, dt]
    if has_init:
        in_specs += [pl.BlockSpec((1, 8, CONV_DIM), lambda b, c: (b, 0, 0)),
                     pl.BlockSpec((1, D_SSM, D_STATE), lambda b, c: (b, 0, 0))]
        args += [conv_prev, h0]
    in_specs += [
        pl.BlockSpec((8, CONV_DIM), lambda b, c: (0, 0)),
        pl.BlockSpec((1, CONV_DIM), lambda b, c: (0, 0)),
        pl.BlockSpec((1, LANES), lambda b, c: (0, 0)),
        pl.BlockSpec((1, LANES), lambda b, c: (0, 0)),
        pl.BlockSpec((1, D_SSM), lambda b, c: (0, 0)),
        pl.BlockSpec((1, D_SSM), lambda b, c: (0, 0)),
    ]
    args += [convw, convb, dtb, alog, dskip, gssm]
    del ntb
    return pl.pallas_call(
        functools.partial(_ssd_kernel, t_in=t_in, valid_last=valid_last, nc=nc, has_init=has_init),
        grid=(n_batch, nc),
        in_specs=in_specs,
        out_specs=[
            pl.BlockSpec((t_in, D_SSM), lambda b, c: (row(b, c), 0)),
            pl.BlockSpec((1, 8, CONV_DIM), lambda b, c: (b, 0, 0)),
            pl.BlockSpec((1, D_SSM, D_STATE), lambda b, c: (b, 0, 0)),
        ],
        out_shape=[
            jax.ShapeDtypeStruct((n_batch * seq_rows, D_SSM), BF16),
            jax.ShapeDtypeStruct((n_batch, 8, CONV_DIM), F32),
            jax.ShapeDtypeStruct((n_batch, D_SSM, D_STATE), F32),
        ],
        scratch_shapes=[pltpu.VMEM((8 + SSD_T, CONV_DIM), F32), pltpu.VMEM((D_SSM, D_STATE), F32)],
        compiler_params=_cparams(("parallel", "arbitrary")),
        name="ssd_scan",
    )(*args)


def _lambda_from(lam_ref, lam_init):
    lp = lam_ref[...]
    s1 = jnp.sum(lp[0:1, :] * lp[1:2, :], axis=-1, keepdims=True)
    s2 = jnp.sum(lp[2:3, :] * lp[3:4, :], axis=-1, keepdims=True)
    return jnp.exp(s1) - jnp.exp(s2) + lam_init


def _attn_prompt_kernel(q_ref, k_ref, v_ref, mq_ref, mk_ref, mv_ref, lam_ref, gsub_ref,
                        od_ref, om_ref, kb, vb, s_s, m_s, l_s, acc_s, *, tq, lam_init):
    qi = pl.program_id(2)

    @pl.when(qi == 0)
    def _():
        kb[...] = k_ref[...].astype(BF16)
        vb[...] = v_ref[...].astype(BF16)

    lane = lax.broadcasted_iota(jnp.int32, (tq, LANES), 1)
    lo = lane < DIFF_DK
    q = q_ref[...] * (DIFF_DK ** -0.5)
    qm = (jnp.where(lo, q, 0.0).astype(BF16), jnp.where(lo, 0.0, q).astype(BF16))
    m_s[...] = jnp.full(m_s.shape, -jnp.inf, F32)
    l_s[...] = jnp.zeros(l_s.shape, F32)
    acc_s[...] = jnp.zeros(acc_s.shape, F32)
    reps = tq // LANES

    def scores(j, masked):
        start = pl.multiple_of(j * tq, tq)
        k_blk = kb[pl.ds(start, tq), :]
        for mi in range(2):
            s = _dot_nt(qm[mi], k_blk)
            if masked:
                r_i = lax.broadcasted_iota(jnp.int32, (tq, tq), 0)
                c_i = lax.broadcasted_iota(jnp.int32, (tq, tq), 1)
                s = jnp.where(r_i >= c_i, s, -jnp.inf)
            s_s[mi, :, pl.ds(start, tq)] = s
            m_s[mi] = jnp.maximum(m_s[mi], jnp.max(s, axis=-1, keepdims=True))

    def weighted(j):
        start = pl.multiple_of(j * tq, tq)
        v_blk = vb[pl.ds(start, tq), :]
        for mi in range(2):
            p = jnp.exp(s_s[mi, :, pl.ds(start, tq)] - pltpu.repeat(m_s[mi], reps, axis=1))
            l_s[mi] += jnp.sum(p, axis=-1, keepdims=True)
            acc_s[mi] += _dot(p.astype(BF16), v_blk)

    def body1(j, carry):
        scores(j, False)
        return carry

    def body2(j, carry):
        weighted(j)
        return carry

    lax.fori_loop(0, qi, body1, 0)
    scores(qi, True)
    lax.fori_loop(0, qi + 1, body2, 0)

    lam = _lambda_from(lam_ref, lam_init)
    o = acc_s[0] / l_s[0] - lam * (acc_s[1] / l_s[1])
    r = lax.rsqrt(jnp.mean(o * o, axis=-1, keepdims=True) + EPS)
    od_ref[...] = ((o * r * gsub_ref[...]) * (1.0 - lam_init)).astype(od_ref.dtype)

    s = _dot_nt(mq_ref[...].astype(BF16), mk_ref[...].astype(BF16)) * (MEM_HD ** -0.5)
    e = jnp.exp(s - jnp.max(s, axis=-1, keepdims=True))
    om = _dot(e.astype(BF16), mv_ref[...].astype(BF16)) / jnp.sum(e, axis=-1, keepdims=True)
    om_ref[...] = om.astype(om_ref.dtype)


def _attn_prompt(u, mkv, lam_pack, gsub, n_batch, seq, lam_init, tq=512):
    nq = seq // tq
    cq, ck, cv, cmq = (COL_Q // LANES, COL_K // LANES, COL_V // LANES, COL_MQ // LANES)
    return pl.pallas_call(
        functools.partial(_attn_prompt_kernel, tq=tq, lam_init=lam_init),
        grid=(n_batch, DIFF_HEADS, nq),
        in_specs=[
            pl.BlockSpec((tq, LANES), lambda b, h, i: (b * nq + i, cq + h)),
            pl.BlockSpec((seq, LANES), lambda b, h, i: (b, ck + h)),
            pl.BlockSpec((seq, LANES), lambda b, h, i: (b, cv + h)),
            pl.BlockSpec((tq, LANES), lambda b, h, i: (b * nq + i, cmq + h)),
            pl.BlockSpec((MEM_TOKENS, LANES), lambda b, h, i: (b, h)),
            pl.BlockSpec((MEM_TOKENS, LANES), lambda b, h, i: (b, MEM_HEADS + h)),
            pl.BlockSpec((8, LANES), lambda b, h, i: (0, 0)),
            pl.BlockSpec((1, LANES), lambda b, h, i: (0, 0)),
        ],
        out_specs=[
            pl.BlockSpec((tq, LANES), lambda b, h, i: (b * nq + i, h)),
            pl.BlockSpec((tq, LANES), lambda b, h, i: (b * nq + i, h)),
        ],
        out_shape=[
            jax.ShapeDtypeStruct((n_batch * seq, D_DIFF), BF16),
            jax.ShapeDtypeStruct((n_batch * seq, D_MEMX), BF16),
        ],
        scratch_shapes=[
            pltpu.VMEM((seq, LANES), BF16),
            pltpu.VMEM((seq, LANES), BF16),
            pltpu.VMEM((2, tq, seq), F32),
            pltpu.VMEM((2, tq, LANES), F32),
            pltpu.VMEM((2, tq, LANES), F32),
            pltpu.VMEM((2, tq, LANES), F32),
        ],
        compiler_params=_cparams(("parallel", "parallel", "arbitrary")),
        name="attn_prompt",
    )(u, u, u, u, mkv, mkv, lam_pack, gsub)


PAGE = 128
PAGE_ROWS = PAGE * DIFF_HEADS


def _per_head_rows(x_row, split_maps):
    lane = lax.broadcasted_iota(jnp.int32, (1, LANES), 1)
    rows = []
    for r in range(2 * DIFF_HEADS):
        xh = x_row[:, (r // 2) * LANES:(r // 2 + 1) * LANES]
        if split_maps:
            xh = jnp.where((lane < DIFF_DK) if r % 2 == 0 else (lane >= DIFF_DK), xh, 0.0)
        rows.append(xh)
    return jnp.concatenate(rows, axis=0)


def _head_match(n_cols):
    r_i = lax.broadcasted_iota(jnp.int32, (8, n_cols), 0)
    c_i = lax.broadcasted_iota(jnp.int32, (8, n_cols), 1)
    return jnp.bitwise_and(c_i, DIFF_HEADS - 1) == jnp.right_shift(r_i, 1)


def _decode_kernel(pt_ref, tok_q, tok_k, tok_v, tok_mq, mk_ref, mv_ref, lam_ref, gsub_ref, *rest,
                   npp, lam_init):
    del pt_ref
    k_refs = rest[:npp]
    v_refs = rest[npp:2 * npp]
    od_ref, om_ref, m_s, l_s, acc_s = rest[2 * npp:]
    j = pl.program_id(1)
    nj = pl.num_programs(1)

    q_row = tok_q[0:1, :] * (DIFF_DK ** -0.5)
    qmat = _per_head_rows(q_row, True)
    qmat_bf = qmat.astype(BF16)

    @pl.when(j == 0)
    def _():
        s_new = jnp.sum(qmat * _per_head_rows(tok_k[0:1, :], False), axis=-1, keepdims=True)
        m_s[...] = jnp.broadcast_to(s_new, m_s.shape)
        l_s[...] = jnp.ones(l_s.shape, F32)
        acc_s[...] = _per_head_rows(tok_v[0:1, :], False)
        mqm = _per_head_rows(tok_mq[0:1, :], False)
        s = _dot_nt(mqm.astype(BF16), mk_ref[...].astype(BF16)) * (MEM_HD ** -0.5)
        s = jnp.where(_head_match(s.shape[1]), s, -jnp.inf)
        e = jnp.exp(s - jnp.max(s, axis=-1, keepdims=True))
        om = _dot(e.astype(BF16), mv_ref[...].astype(BF16)) / jnp.sum(e, axis=-1, keepdims=True)
        om_row = jnp.concatenate([om[2 * h:2 * h + 1, :] for h in range(MEM_HEADS)], axis=1)
        om_ref[...] = jnp.broadcast_to(om_row, om_ref.shape)

    s_all = jnp.concatenate([_dot_nt(qmat_bf, k_refs[i][...].astype(BF16)) for i in range(npp)], axis=1)
    s_all = jnp.where(_head_match(npp * PAGE_ROWS), s_all, -jnp.inf)
    m_prev = m_s[...]
    m_new = jnp.maximum(m_prev, jnp.max(s_all, axis=-1, keepdims=True))
    alpha = jnp.exp(m_prev - m_new)
    p = jnp.exp(s_all - m_new[:, 0:1])
    l_s[...] = alpha * l_s[...] + jnp.sum(p, axis=-1, keepdims=True)
    p_bf = p.astype(BF16)
    pv = _dot(p_bf[:, 0:PAGE_ROWS], v_refs[0][...].astype(BF16))
    for i in range(1, npp):
        pv = pv + _dot(p_bf[:, i * PAGE_ROWS:(i + 1) * PAGE_ROWS], v_refs[i][...].astype(BF16))
    acc_s[...] = alpha * acc_s[...] + pv
    m_s[...] = m_new

    @pl.when(j == nj - 1)
    def _():
        lam = _lambda_from(lam_ref, lam_init)
        o_all = acc_s[...] / l_s[...]
        outs = []
        for h in range(DIFF_HEADS):
            o = o_all[2 * h:2 * h + 1, :] - lam * o_all[2 * h + 1:2 * h + 2, :]
            r = lax.rsqrt(jnp.mean(o * o, axis=-1, keepdims=True) + EPS)
            outs.append((o * r * gsub_ref[...]) * (1.0 - lam_init))
        od_ref[...] = jnp.broadcast_to(jnp.concatenate(outs, axis=1), od_ref.shape)


def _attn_decode(u_s, cache_k, cache_v, page_table, mem_k, mem_v, lam_pack, gsub, layer, n_pool, lam_init, npp=16):
    n_b, n_pages = page_table.shape
    assert n_pages % npp == 0
    W = DIFF_HEADS * LANES
    cq, ck, cv, cmq = (COL_Q // W, COL_K // W, COL_V // W, COL_MQ // W)
    mem_rows = MEM_TOKENS * MEM_HEADS

    def tok(col):
        return pl.BlockSpec((ROWS_S, W), lambda b, j, pt: (b, col))

    def page(i):
        return pl.BlockSpec((PAGE_ROWS, LANES),
                            lambda b, j, pt, i=i: (layer * n_pool + pt[b, j * npp + i], 0))

    in_specs = [tok(cq), tok(ck), tok(cv), tok(cmq),
                pl.BlockSpec((mem_rows, LANES), lambda b, j, pt: (layer * n_b + b, 0)),
                pl.BlockSpec((mem_rows, LANES), lambda b, j, pt: (layer * n_b + b, 0)),
                pl.BlockSpec((8, LANES), lambda b, j, pt: (0, 0)),
                pl.BlockSpec((1, LANES), lambda b, j, pt: (0, 0))]
    in_specs += [page(i) for i in range(npp)] * 2
    grid_spec = pltpu.PrefetchScalarGridSpec(
        num_scalar_prefetch=1,
        grid=(n_b, n_pages // npp),
        in_specs=in_specs,
        out_specs=[pl.BlockSpec((8, W), lambda b, j, pt: (b, 0)),
                   pl.BlockSpec((8, W), lambda b, j, pt: (b, 0))],
        scratch_shapes=[pltpu.VMEM((8, LANES), F32), pltpu.VMEM((8, LANES), F32), pltpu.VMEM((8, LANES), F32)],
    )
    od, om = pl.pallas_call(
        functools.partial(_decode_kernel, npp=npp, lam_init=lam_init),
        grid_spec=grid_spec,
        out_shape=[jax.ShapeDtypeStruct((n_b * 8, W), F32), jax.ShapeDtypeStruct((n_b * 8, W), F32)],
        compiler_params=_cparams(("parallel", "arbitrary")),
        name="attn_decode",
    )(page_table, u_s, u_s, u_s, u_s, mem_k, mem_v, lam_pack, gsub,
      *([cache_k] * npp), *([cache_v] * npp))
    return od.reshape(n_b, 8, W)[:, 0], om.reshape(n_b, 8, W)[:, 0]


def _outproj_kernel(x_ref, y_ref, od_ref, om_ref, w1_ref, w2_ref, w3_ref, o_ref):
    o_ref[...] = (x_ref[...]
                  + _dot(y_ref[...].astype(BF16), w1_ref[...])
                  + _dot(od_ref[...].astype(BF16), w2_ref[...])
                  + _dot(om_ref[...].astype(BF16), w3_ref[...]))


def _outproj(x, y, od, om, w_out, tm=1024, tn=512):
    m = x.shape[0]
    tm = min(tm, m)
    assert m % tm == 0
    return pl.pallas_call(
        _outproj_kernel,
        grid=(m // tm, D_MODEL // tn),
        in_specs=[
            pl.BlockSpec((tm, tn), lambda i, j: (i, j)),
            pl.BlockSpec((tm, D_SSM), lambda i, j: (i, 0)),
            pl.BlockSpec((tm, D_DIFF), lambda i, j: (i, 0)),
            pl.BlockSpec((tm, D_MEMX), lambda i, j: (i, 0)),
            pl.BlockSpec((D_SSM, tn), lambda i, j: (0, j)),
            pl.BlockSpec((D_DIFF, tn), lambda i, j: (D_SSM // D_DIFF, j)),
            pl.BlockSpec((D_MEMX, tn), lambda i, j: ((D_SSM + D_DIFF) // D_MEMX, j)),
        ],
        out_specs=pl.BlockSpec((tm, tn), lambda i, j: (i, j)),
        out_shape=jax.ShapeDtypeStruct((m, D_MODEL), F32),
        compiler_params=_cparams(("parallel", "arbitrary")),
        name="out_proj",
    )(x, y, od, om, w_out, w_out, w_out)


def _mlp_kernel(x_ref, g_ref, wu_ref, wd_ref, o_ref, h_ref):
    f = pl.program_id(1)

    @pl.when(f == 0)
    def _():
        x = x_ref[...]
        r = lax.rsqrt(jnp.mean(x * x, axis=-1, keepdims=True) + EPS)
        h_ref[...] = (x * r * g_ref[...]).astype(BF16)
        o_ref[...] = x

    a = jnp.maximum(_dot(h_ref[...], wu_ref[...]), 0.0)
    o_ref[...] += _dot((a * a).astype(BF16), wd_ref[...])


def _mlp(x, g, w_up, w_down, tm=512, tf=1024):
    m = x.shape[0]
    tm = min(tm, m)
    assert m % tm == 0
    return pl.pallas_call(
        _mlp_kernel,
        grid=(m // tm, D_FF // tf),
        in_specs=[
            pl.BlockSpec((tm, D_MODEL), lambda i, f: (i, 0)),
            pl.BlockSpec((1, D_MODEL), lambda i, f: (0, 0)),
            pl.BlockSpec((D_MODEL, tf), lambda i, f: (0, f)),
            pl.BlockSpec((tf, D_MODEL), lambda i, f: (f, 0)),
        ],
        out_specs=pl.BlockSpec((tm, D_MODEL), lambda i, f: (i, 0)),
        out_shape=jax.ShapeDtypeStruct((m, D_MODEL), F32),
        scratch_shapes=[pltpu.VMEM((tm, D_MODEL), BF16)],
        compiler_params=_cparams(("parallel", "arbitrary")),
        name="mlp",
    )(x, g, w_up, w_down)


def _pad_lanes(v, width=LANES):
    v = v.reshape(1, -1).astype(F32)
    return jnp.pad(v, ((0, 0), (0, width - v.shape[1])))


def kernel(x_prompt, x_sample, mem_prompt, cache_diff_k, cache_diff_v, cache_mem_k, cache_mem_v, state_conv, state_ssm, page_table, norm_mix, w_in, conv_w, conv_b, dt_bias, a_log, d_skip, g_ssm, g_q, g_k, lambda_q1, lambda_k1, lambda_q2, lambda_k2, g_subln, norm_mem, w_mem_kv, g_mq, g_mk, w_out, norm_mlp, w_up, w_down):
    depth = w_in.shape[0]
    bp, seq, _ = x_prompt.shape
    bd, dec_seq, _ = x_sample.shape
    assert dec_seq == 1 and seq % SSD_T == 0
    n_pool = cache_diff_k.shape[1]

    xp = x_prompt.reshape(bp * seq, D_MODEL)
    xs = x_sample.reshape(bd, D_MODEL)
    cache_k = cache_diff_k.reshape(depth * n_pool * PAGE_ROWS, LANES)
    cache_v = cache_diff_v.reshape(depth * n_pool * PAGE_ROWS, LANES)
    mem_k = cache_mem_k.reshape(depth * bd * MEM_TOKENS * MEM_HEADS, MEM_HD)
    mem_v = cache_mem_v.reshape(depth * bd * MEM_TOKENS * MEM_HEADS, MEM_HD)

    outs = [[] for _ in range(10)]
    for l in range(depth):
        lam_init = 0.8 - 0.6 * math.exp(-0.3 * l)
        wl = w_in[l]
        o_z, o_x, o_bc, o_dt = 0, D_SSM, 2 * D_SSM, D_SSM + CONV_DIM
        o_q = o_dt + SSM_HEADS
        w_main = jnp.concatenate(
            [wl[:, o_x:o_bc], wl[:, o_z:o_x], wl[:, o_bc:o_dt], wl[:, o_q:]], axis=1).astype(BF16)
        w_dt = jnp.pad(wl[:, o_dt:o_q], ((0, 0), (0, LANES - SSM_HEADS))).astype(BF16)
        ones = jnp.ones((PROJ_TN,), F32)
        gains = jnp.concatenate(
            [ones] * 5 + [jnp.tile(g_q[l], 2 * DIFF_HEADS), jnp.tile(g_k[l], 2 * DIFF_HEADS), ones,
                          jnp.tile(g_mq[l], MEM_HEADS)]).reshape(1, N_MAIN)
        modes = (0, 0, 0, 0, 0, DIFF_DK, DIFF_DK, 0, MEM_HD)
        g_mix = norm_mix[l].reshape(1, D_MODEL)

        ssm_p = (jnp.pad(conv_w[l], ((0, 8 - CONV_W), (0, 0))),
                 conv_b[l].reshape(1, CONV_DIM),
                 _pad_lanes(dt_bias[l]), _pad_lanes(a_log[l]),
                 jnp.repeat(d_skip[l], SSM_HEAD_DIM).reshape(1, D_SSM),
                 g_ssm[l].reshape(1, D_SSM))
        lam_pack = jnp.concatenate(
            [_pad_lanes(lambda_q1[l]), _pad_lanes(lambda_k1[l]), _pad_lanes(lambda_q2[l]),
             _pad_lanes(lambda_k2[l]), jnp.zeros((4, LANES), F32)], axis=0)
        gsub = g_subln[l].reshape(1, LANES)
        w_out_bf = w_out[l].astype(BF16)
        w_up_bf = w_up[l].astype(BF16)
        w_down_bf = w_down[l].astype(BF16)
        g_mlp = norm_mlp[l].reshape(1, D_MODEL)

        u_p, dt_p = _norm_proj(xp, g_mix, w_main, gains, modes, w_dt=w_dt)
        y_p, conv_p, h_p = _ssd(u_p, dt_p, bp, seq, seq, ssm_p)
        mem_gains = jnp.concatenate([jnp.tile(g_mk[l], MEM_HEADS), ones]).reshape(1, 2 * D_MEMX)
        mkv = _norm_proj(mem_prompt.reshape(bp * MEM_TOKENS, D_MODEL), norm_mem[l].reshape(1, D_MODEL),
                         w_mem_kv[l].astype(BF16), mem_gains, (MEM_HD, 0))
        od_p, om_p = _attn_prompt(u_p, mkv, lam_pack, gsub, bp, seq, lam_init)
        xp_mid = _outproj(xp, y_p, od_p, om_p, w_out_bf)
        xp_new = _mlp(xp_mid, g_mlp, w_up_bf, w_down_bf)

        outs[0].append(u_p[:, COL_K:COL_K + D_DIFF].reshape(bp, seq, DIFF_HEADS, LANES))
        outs[1].append(u_p[:, COL_V:COL_V + D_DIFF].reshape(bp, seq, DIFF_HEADS, LANES))
        outs[4].append(mkv[:, :D_MEMX].reshape(bp, MEM_TOKENS, MEM_HEADS, MEM_HD))
        outs[5].append(mkv[:, D_MEMX:].reshape(bp, MEM_TOKENS, MEM_HEADS, MEM_HD))
        outs[6].append(conv_p[:, :CONV_W - 1])
        outs[7].append(h_p.reshape(bp, SSM_HEADS, SSM_HEAD_DIM, D_STATE))

        xs_pad = jnp.pad(xs.reshape(bd, 1, D_MODEL), ((0, 0), (0, ROWS_S - 1), (0, 0))).reshape(bd * ROWS_S, D_MODEL)
        u_s, dt_s = _norm_proj(xs_pad, g_mix, w_main, gains, modes, w_dt=w_dt)
        conv_prev = jnp.pad(state_conv[l], ((0, 0), (8 - (CONV_W - 1), 0), (0, 0)))
        y_s, conv_s, h_s = _ssd(u_s, dt_s, bd, ROWS_S, 1, ssm_p, conv_prev=conv_prev,
                                h0=state_ssm[l].reshape(bd, D_SSM, D_STATE))
        od_s, om_s = _attn_decode(u_s, cache_k, cache_v, page_table, mem_k, mem_v, lam_pack, gsub, l, n_pool, lam_init)
        y_s0 = y_s.reshape(bd, ROWS_S, D_SSM)[:, 0]
        xs_mid = _outproj(xs, y_s0, od_s, om_s, w_out_bf)
        xs_new = _mlp(xs_mid, g_mlp, w_up_bf, w_down_bf)
        u_s0 = u_s.reshape(bd, ROWS_S, N_MAIN)[:, 0]
        outs[2].append(u_s0[:, COL_K:COL_K + D_DIFF].reshape(bd, 1, DIFF_HEADS, LANES))
        outs[3].append(u_s0[:, COL_V:COL_V + D_DIFF].reshape(bd, 1, DIFF_HEADS, LANES))
        outs[8].append(conv_s[:, :CONV_W - 1])
        outs[9].append(h_s.reshape(bd, SSM_HEADS, SSM_HEAD_DIM, D_STATE))

        xp, xs = xp_new, xs_new

    st = [jnp.stack(o) for o in outs]
    return (xp.reshape(bp, seq, D_MODEL), xs.reshape(bd, 1, D_MODEL),
            st[0], st[1], st[2], st[3], st[4], st[5], st[6], st[7], st[8], st[9])
```

```python
import functools
import math

import jax
import jax.numpy as jnp
from jax import lax
from jax.experimental import pallas as pl
from jax.experimental.pallas import tpu as pltpu

F32 = jnp.float32
BF16 = jnp.bfloat16

D_MODEL = 2048
D_SSM = 1024
D_DIFF = 512
D_MEMX = 512
SSM_HEADS = 16
SSM_HEAD_DIM = 64
SSM_GROUPS = 2
D_STATE = 128
CONV_W = 4
BC_DIM = 2 * SSM_GROUPS * D_STATE
CONV_DIM = D_SSM + BC_DIM
DIFF_HEADS = 4
DIFF_DK = 64
MEM_HEADS = 4
MEM_HD = 128
MEM_TOKENS = 256
D_FF = 4 * D_MODEL
EPS = 1e-6

LANES = 128
SUBLANES_BF16 = 16
VMEM_LIMIT_BYTES = 56 * 1024 * 1024

N_MAIN = D_SSM + D_SSM + BC_DIM + 3 * D_DIFF + D_MEMX
COL_Z, COL_XS, COL_BC, COL_Q, COL_K, COL_V, COL_MQ = 0, 1024, 2048, 2560, 3072, 3584, 4096
PROJ_TN = 512
SSD_T = 128
ROWS_S = SUBLANES_BF16


def _cparams(sem):
    return pltpu.CompilerParams(dimension_semantics=sem, vmem_limit_bytes=VMEM_LIMIT_BYTES)


def _dot(a, b):
    return jnp.dot(a, b, preferred_element_type=F32)


def _dot_nt(a, b):
    return lax.dot_general(a, b, (((1,), (1,)), ((), ())), preferred_element_type=F32)


def _silu(x):
    return x * (1.0 / (1.0 + jnp.exp(-x)))


def _group_rmsnorm_slab(a, gain, group):
    sq = a * a
    if group == LANES:
        r = lax.rsqrt(jnp.sum(sq, axis=-1, keepdims=True) * (1.0 / LANES) + EPS)
    else:
        lane = lax.broadcasted_iota(jnp.int32, a.shape, 1)
        lo = lane < group
        s_lo = jnp.sum(jnp.where(lo, sq, 0.0), axis=-1, keepdims=True)
        s_hi = jnp.sum(jnp.where(lo, 0.0, sq), axis=-1, keepdims=True)
        r = jnp.where(lo, lax.rsqrt(s_lo * (1.0 / group) + EPS),
                      lax.rsqrt(s_hi * (1.0 / group) + EPS))
    return a * r * gain


def _proj_kernel(*refs, modes, seg_of, n_w, has_dt, rows_blocks):
    x_ref, g_ref = refs[:2]
    w_refs = refs[2:2 + n_w]
    gain_ref = refs[2 + n_w]
    pos = 3 + n_w
    if has_dt:
        wdt_ref = refs[pos]
        pos += 1
    u_ref = refs[pos]
    pos += 1
    if has_dt:
        dt_ref = refs[pos]
        pos += 1
    rows_refs = refs[pos:pos + len(rows_blocks)]
    h_ref = refs[pos + len(rows_blocks)]
    j = pl.program_id(1)
    tm = x_ref.shape[0]

    @pl.when(j == 0)
    def _():
        x = x_ref[...]
        r = lax.rsqrt(jnp.mean(x * x, axis=-1, keepdims=True) + EPS)
        h_ref[...] = (x * r * g_ref[...]).astype(BF16)
        if has_dt:
            dt_ref[...] = _dot(h_ref[...], wdt_ref[...])

    keys = {}
    for jj, m in enumerate(modes):
        ri = rows_blocks.index(jj) if jj in rows_blocks else None
        keys.setdefault((seg_of[jj], m, ri), []).append(jj)
    for (seg, mode, ri), jjs in keys.items():
        cond = j == jjs[0]
        for jj in jjs[1:]:
            cond = jnp.logical_or(cond, j == jj)

        @pl.when(cond)
        def _(seg=seg, mode=mode, ri=ri):
            acc = _dot(h_ref[...], w_refs[seg][...])
            if mode == 0 and ri is None:
                u_ref[...] = acc
                return
            gain = gain_ref[...]
            for s in range(acc.shape[1] // LANES):
                sl = slice(s * LANES, (s + 1) * LANES)
                slab = acc[:, sl]
                if mode:
                    slab = _group_rmsnorm_slab(slab, gain[:, sl], mode)
                u_ref[:, sl] = slab
                if ri is not None:
                    rows_refs[ri][pl.ds(s, tm, stride=PROJ_TN // LANES), :] = slab


def _norm_proj(x, g, ws, gains, modes, w_dt=None, rows_blocks=(), tm=1024):
    m, k = x.shape
    tm = min(tm, m)
    nblk = [w.shape[1] // PROJ_TN for w in ws]
    assert m % tm == 0 and sum(nblk) == len(modes) and all(w.shape[1] % PROJ_TN == 0 for w in ws)
    seg_of, offs = [], []
    for a, nb in enumerate(nblk):
        offs.append(len(seg_of))
        seg_of += [a] * nb
    n = PROJ_TN * len(modes)
    has_dt = w_dt is not None

    def w_spec(a):
        return pl.BlockSpec(
            (k, PROJ_TN), lambda i, j, a=a: (0, jnp.minimum(jnp.maximum(j - offs[a], 0), nblk[a] - 1)))

    in_specs = [pl.BlockSpec((tm, k), lambda i, j: (i, 0)), pl.BlockSpec((1, k), lambda i, j: (0, 0))]
    in_specs += [w_spec(a) for a in range(len(ws))]
    in_specs.append(pl.BlockSpec((1, PROJ_TN), lambda i, j: (0, j)))
    args = [x, g, *ws, gains]
    out_shape = [jax.ShapeDtypeStruct((m, n), F32)]
    out_specs = [pl.BlockSpec((tm, PROJ_TN), lambda i, j: (i, j))]
    if has_dt:
        in_specs.append(pl.BlockSpec((k, LANES), lambda i, j: (0, 0)))
        args.append(w_dt)
        out_shape.append(jax.ShapeDtypeStruct((m, LANES), F32))
        out_specs.append(pl.BlockSpec((tm, LANES), lambda i, j: (i, 0)))
    heads = PROJ_TN // LANES
    for _ in rows_blocks:
        out_shape.append(jax.ShapeDtypeStruct((m * heads, LANES), F32))
        out_specs.append(pl.BlockSpec((tm * heads, LANES), lambda i, j: (i, 0)))
    return pl.pallas_call(
        functools.partial(_proj_kernel, modes=tuple(modes), seg_of=tuple(seg_of), n_w=len(ws), has_dt=has_dt,
                          rows_blocks=tuple(rows_blocks)),
        grid=(m // tm, len(modes)),
        in_specs=in_specs,
        out_specs=out_specs,
        out_shape=out_shape,
        scratch_shapes=[pltpu.VMEM((tm, k), BF16)],
        compiler_params=_cparams(("parallel", "arbitrary")),
        name="norm_proj",
    )(*args)


def _split3(x):
    hi = x.astype(BF16)
    r1 = x - hi.astype(F32)
    mid = r1.astype(BF16)
    lo = (r1 - mid.astype(F32)).astype(BF16)
    return hi, mid, lo


def _ssd_kernel(*refs, t_in, valid_last, nc, has_init):
    if has_init:
        (xs_ref, z_ref, bc_ref, dt_ref, cprev_ref, h0_ref, convw_ref, convb_ref, dtb_ref, alog_ref,
         dskip_ref, gssm_ref, y_ref, cout_ref, hout_ref, xpad, hst) = refs
    else:
        (xs_ref, z_ref, bc_ref, dt_ref, convw_ref, convb_ref, dtb_ref, alog_ref,
         dskip_ref, gssm_ref, y_ref, cout_ref, hout_ref, xpad, hst) = refs
    T = SSD_T
    c = pl.program_id(1)

    @pl.when(c == 0)
    def _():
        if has_init:
            xpad[0:8, :] = cprev_ref[0]
            hst[...] = h0_ref[0]
        else:
            xpad[0:8, :] = jnp.zeros((8, CONV_DIM), F32)
            hst[...] = jnp.zeros(hst.shape, F32)

    def rows(ref):
        v = ref[...]
        if t_in < T:
            v = jnp.concatenate([v, jnp.zeros((T - t_in, v.shape[1]), F32)], axis=0)
        return v

    xpad[8:8 + T, 0:D_SSM] = rows(xs_ref)
    xpad[8:8 + T, D_SSM:CONV_DIM] = rows(bc_ref)
    z = rows(z_ref)
    dt_raw = rows(dt_ref)

    convw = convw_ref[...]
    conv = convb_ref[...] + xpad[5:5 + T, :] * convw[0:1, :]
    for jtap in range(1, CONV_W):
        conv = conv + xpad[5 + jtap:5 + jtap + T, :] * convw[jtap:jtap + 1, :]
    xact = _silu(conv)

    row_i = lax.broadcasted_iota(jnp.int32, (T, LANES), 0)
    col_i = lax.broadcasted_iota(jnp.int32, (T, LANES), 1)
    xv = dt_raw + dtb_ref[...]
    dt = jnp.maximum(xv, 0.0) + jnp.log1p(jnp.exp(-jnp.abs(xv)))
    if valid_last < T:
        dt = jnp.where(row_i < valid_last, dt, 0.0)
    a_neg = -jnp.exp(alog_ref[...])
    adt = dt * a_neg

    tril = (row_i >= col_i)
    tril_bf = jnp.where(tril, 1.0, 0.0).astype(BF16)
    a_hi, a_mid, a_lo = _split3(adt)
    acum = _dot(tril_bf, a_hi) + _dot(tril_bf, a_mid) + _dot(tril_bf, a_lo)
    acum_t = acum.T
    dt_t = dt.T
    e_acum = jnp.exp(acum)
    a_last = acum[T - 1:T, :]
    w_state = jnp.exp(a_last - acum) * dt
    da_last = jnp.exp(a_last)

    lo_half = col_i < SSM_HEAD_DIM
    neg_big = jnp.float32(-1e30)

    def colb(tile, r):
        return jnp.broadcast_to(tile[:, r:r + 1], (T, LANES))

    y_slabs = []
    for g in range(SSM_GROUPS):
        b_g = xact[:, D_SSM + g * D_STATE:D_SSM + (g + 1) * D_STATE]
        c_g = xact[:, D_SSM + SSM_GROUPS * D_STATE + g * D_STATE:
                   D_SSM + SSM_GROUPS * D_STATE + (g + 1) * D_STATE]
        b_bf = b_g.astype(BF16)
        c_bf = c_g.astype(BF16)
        cb = _dot_nt(c_bf, b_bf)
        for pp in range(SSM_HEADS // SSM_GROUPS // 2):
            p = g * (SSM_HEADS // SSM_GROUPS // 2) + pp
            sl = slice(p * LANES, (p + 1) * LANES)
            xs_slab = xact[:, sl]
            y_acc = dskip_ref[:, sl] * xs_slab
            for hh in range(2):
                r = 2 * p + hh
                seg = colb(acum, r) - acum_t[r:r + 1, :]
                lmat = jnp.exp(jnp.where(tril, seg, neg_big))
                mr = (cb * lmat * dt_t[r:r + 1, :]).astype(BF16)
                xh = jnp.where(lo_half if hh == 0 else jnp.logical_not(lo_half), xs_slab, 0.0)
                y_acc = y_acc + _dot(mr, xh.astype(BF16))
            hpair = hst[sl, :]
            e_pair = jnp.where(lo_half, colb(e_acum, 2 * p), colb(e_acum, 2 * p + 1))
            y_acc = y_acc + e_pair * _dot_nt(c_bf, hpair.astype(BF16))
            w_pair = jnp.where(lo_half, colb(w_state, 2 * p), colb(w_state, 2 * p + 1))
            xw_t = (xs_slab * w_pair).T
            st = _dot(xw_t.astype(BF16), b_bf)
            da = jnp.concatenate(
                [jnp.broadcast_to(da_last[:, 2 * p:2 * p + 1], (SSM_HEAD_DIM, LANES)),
                 jnp.broadcast_to(da_last[:, 2 * p + 1:2 * p + 2], (SSM_HEAD_DIM, LANES))], axis=0)
            hst[sl, :] = da * hpair + st
            y_slabs.append(y_acc * _silu(z[:, sl]))

    per_group = D_SSM // SSM_GROUPS // LANES
    for g in range(SSM_GROUPS):
        slabs = y_slabs[g * per_group:(g + 1) * per_group]
        ssum = jnp.sum(slabs[0] * slabs[0], axis=-1, keepdims=True)
        for s in slabs[1:]:
            ssum = ssum + jnp.sum(s * s, axis=-1, keepdims=True)
        r = lax.rsqrt(ssum * (1.0 / (per_group * LANES)) + EPS)
        for k, s in enumerate(slabs):
            sl = slice((g * per_group + k) * LANES, (g * per_group + k + 1) * LANES)
            y_ref[:, sl] = (s * r * gssm_ref[:, sl])[0:t_in, :].astype(y_ref.dtype)

    if nc > 1:
        xpad[5:8, :] = xpad[5 + T:8 + T, :]

    @pl.when(c == nc - 1)
    def _():
        if nc > 1:
            cout_ref[0, 0:CONV_W - 1, :] = xpad[5:8, :]
        else:
            cout_ref[0, 0:CONV_W - 1, :] = xpad[5 + valid_last:8 + valid_last, :]
        hout_ref[0] = hst[...]


def _ssd(u, dt, n_batch, seq_rows, valid_len, ssm_p, conv_prev=None, h0=None):
    t_in = min(seq_rows, SSD_T)
    nc = max(seq_rows // SSD_T, 1)
    valid_last = valid_len - (nc - 1) * SSD_T
    has_init = conv_prev is not None
    convw, convb, dtb, alog, dskip, gssm = ssm_p
    ntb = N_MAIN // PROJ_TN

    def row(b, c):
        return b * nc + c

    in_specs = [
        pl.BlockSpec((t_in, D_SSM), lambda b, c: (row(b, c), COL_XS // D_SSM)),
        pl.BlockSpec((t_in, D_SSM), lambda b, c: (row(b, c), COL_Z // D_SSM)),
        pl.BlockSpec((t_in, BC_DIM), lambda b, c: (row(b, c), COL_BC // BC_DIM)),
        pl.BlockSpec((t_in, LANES), lambda b, c: (row(b, c), 0)),
    ]
    args = [u, u, u, dt]
    if has_init:
        in_specs += [pl.BlockSpec((1, 8, CONV_DIM), lambda b, c: (b, 0, 0)),
                     pl.BlockSpec((1, D_SSM, D_STATE), lambda b, c: (b, 0, 0))]
        args += [conv_prev, h0]
    in_specs += [
        pl.BlockSpec((8, CONV_DIM), lambda b, c: (0, 0)),
        pl.BlockSpec((1, CONV_DIM), lambda b, c: (0, 0)),
        pl.BlockSpec((1, LANES), lambda b, c: (0, 0)),
        pl.BlockSpec((1, LANES), lambda b, c: (0, 0)),
        pl.BlockSpec((1, D_SSM), lambda b, c: (0, 0)),
        pl.BlockSpec((1, D_SSM), lambda b, c: (0, 0)),
    ]
    args += [convw, convb, dtb, alog, dskip, gssm]
    del ntb
    return pl.pallas_call(
        functools.partial(_ssd_kernel, t_in=t_in, valid_last=valid_last, nc=nc, has_init=has_init),
        grid=(n_batch, nc),
        in_specs=in_specs,
        out_specs=[
            pl.BlockSpec((t_in, D_SSM), lambda b, c: (row(b, c), 0)),
            pl.BlockSpec((1, 8, CONV_DIM), lambda b, c: (b, 0, 0)),
            pl.BlockSpec((1, D_SSM, D_STATE), lambda b, c: (b, 0, 0)),
        ],
        out_shape=[
            jax.ShapeDtypeStruct((n_batch * seq_rows, D_SSM), BF16),
            jax.ShapeDtypeStruct((n_batch, 8, CONV_DIM), F32),
            jax.ShapeDtypeStruct((n_batch, D_SSM, D_STATE), F32),
        ],
        scratch_shapes=[pltpu.VMEM((8 + SSD_T, CONV_DIM), F32), pltpu.VMEM((D_SSM, D_STATE), F32)],
        compiler_params=_cparams(("parallel", "arbitrary")),
        name="ssd_scan",
    )(*args)


def _lambda_from(lam_ref, lam_init):
    lp = lam_ref[...]
    s1 = jnp.sum(lp[0:1, :] * lp[1:2, :], axis=-1, keepdims=True)
    s2 = jnp.sum(lp[2:3, :] * lp[3:4, :], axis=-1, keepdims=True)
    return jnp.exp(s1) - jnp.exp(s2) + lam_init


def _attn_prompt_kernel(q_ref, k_ref, v_ref, mq_ref, mk_ref, mv_ref, lam_ref, gsub_ref,
                        od_ref, om_ref, kb, vb, s_s, m_s, l_s, acc_s, *, tq, lam_init):
    qi = pl.program_id(2)

    @pl.when(qi == 0)
    def _():
        kb[...] = k_ref[...].astype(BF16)
        vb[...] = v_ref[...].astype(BF16)

    lane = lax.broadcasted_iota(jnp.int32, (tq, LANES), 1)
    lo = lane < DIFF_DK
    q = q_ref[...] * (DIFF_DK ** -0.5)
    qm = (jnp.where(lo, q, 0.0).astype(BF16), jnp.where(lo, 0.0, q).astype(BF16))
    m_s[...] = jnp.full(m_s.shape, -jnp.inf, F32)
    l_s[...] = jnp.zeros(l_s.shape, F32)
    acc_s[...] = jnp.zeros(acc_s.shape, F32)
    reps = tq // LANES

    def scores(j, masked):
        start = pl.multiple_of(j * tq, tq)
        k_blk = kb[pl.ds(start, tq), :]
        for mi in range(2):
            s = _dot_nt(qm[mi], k_blk)
            if masked:
                r_i = lax.broadcasted_iota(jnp.int32, (tq, tq), 0)
                c_i = lax.broadcasted_iota(jnp.int32, (tq, tq), 1)
                s = jnp.where(r_i >= c_i, s, -jnp.inf)
            s_s[mi, :, pl.ds(start, tq)] = s
            m_s[mi] = jnp.maximum(m_s[mi], jnp.max(s, axis=-1, keepdims=True))

    def weighted(j):
        start = pl.multiple_of(j * tq, tq)
        v_blk = vb[pl.ds(start, tq), :]
        for mi in range(2):
            p = jnp.exp(s_s[mi, :, pl.ds(start, tq)] - pltpu.repeat(m_s[mi], reps, axis=1))
            l_s[mi] += jnp.sum(p, axis=-1, keepdims=True)
            acc_s[mi] += _dot(p.astype(BF16), v_blk)

    def body1(j, carry):
        scores(j, False)
        return carry

    def body2(j, carry):
        weighted(j)
        return carry

    lax.fori_loop(0, qi, body1, 0)
    scores(qi, True)
    lax.fori_loop(0, qi + 1, body2, 0)

    lam = _lambda_from(lam_ref, lam_init)
    o = acc_s[0] / l_s[0] - lam * (acc_s[1] / l_s[1])
    r = lax.rsqrt(jnp.mean(o * o, axis=-1, keepdims=True) + EPS)
    od_ref[...] = ((o * r * gsub_ref[...]) * (1.0 - lam_init)).astype(od_ref.dtype)

    s = _dot_nt(mq_ref[...].astype(BF16), mk_ref[...].astype(BF16)) * (MEM_HD ** -0.5)
    e = jnp.exp(s - jnp.max(s, axis=-1, keepdims=True))
    om = _dot(e.astype(BF16), mv_ref[...].astype(BF16)) / jnp.sum(e, axis=-1, keepdims=True)
    om_ref[...] = om.astype(om_ref.dtype)


def _attn_prompt(u, mkv, lam_pack, gsub, n_batch, seq, lam_init, tq=512):
    nq = seq // tq
    cq, ck, cv, cmq = (COL_Q // LANES, COL_K // LANES, COL_V // LANES, COL_MQ // LANES)
    return pl.pallas_call(
        functools.partial(_attn_prompt_kernel, tq=tq, lam_init=lam_init),
        grid=(n_batch, DIFF_HEADS, nq),
        in_specs=[
            pl.BlockSpec((tq, LANES), lambda b, h, i: (b * nq + i, cq + h)),
            pl.BlockSpec((seq, LANES), lambda b, h, i: (b, ck + h)),
            pl.BlockSpec((seq, LANES), lambda b, h, i: (b, cv + h)),
            pl.BlockSpec((tq, LANES), lambda b, h, i: (b * nq + i, cmq + h)),
            pl.BlockSpec((MEM_TOKENS, LANES), lambda b, h, i: (b, h)),
            pl.BlockSpec((MEM_TOKENS, LANES), lambda b, h, i: (b, MEM_HEADS + h)),
            pl.BlockSpec((8, LANES), lambda b, h, i: (0, 0)),
            pl.BlockSpec((1, LANES), lambda b, h, i: (0, 0)),
        ],
        out_specs=[
            pl.BlockSpec((tq, LANES), lambda b, h, i: (b * nq + i, h)),
            pl.BlockSpec((tq, LANES), lambda b, h, i: (b * nq + i, h)),
        ],
        out_shape=[
            jax.ShapeDtypeStruct((n_batch * seq, D_DIFF), BF16),
            jax.ShapeDtypeStruct((n_batch * seq, D_MEMX), BF16),
        ],
        scratch_shapes=[
            pltpu.VMEM((seq, LANES), BF16),
            pltpu.VMEM((seq, LANES), BF16),
            pltpu.VMEM((2, tq, seq), F32),
            pltpu.VMEM((2, tq, LANES), F32),
            pltpu.VMEM((2, tq, LANES), F32),
            pltpu.VMEM((2, tq, LANES), F32),
        ],
        compiler_params=_cparams(("parallel", "parallel", "arbitrary")),
        name="attn_prompt",
    )(u, u, u, u, mkv, mkv, lam_pack, gsub)


PAGE = 128
PAGE_ROWS = PAGE * DIFF_HEADS


def _per_head_rows(x_row, split_maps):
    lane = lax.broadcasted_iota(jnp.int32, (1, LANES), 1)
    rows = []
    for r in range(2 * DIFF_HEADS):
        xh = x_row[:, (r // 2) * LANES:(r // 2 + 1) * LANES]
        if split_maps:
            xh = jnp.where((lane < DIFF_DK) if r % 2 == 0 else (lane >= DIFF_DK), xh, 0.0)
        rows.append(xh)
    return jnp.concatenate(rows, axis=0)


def _head_match(n_cols):
    r_i = lax.broadcasted_iota(jnp.int32, (8, n_cols), 0)
    c_i = lax.broadcasted_iota(jnp.int32, (8, n_cols), 1)
    return jnp.bitwise_and(c_i, DIFF_HEADS - 1) == jnp.right_shift(r_i, 1)


def _rider_specs(step_fn, spb, npp, page_lo, layer, n_pool):
    W = DIFF_HEADS * LANES

    def seq(idx):
        return step_fn(*idx[:-1]) // spb

    def page(i):
        def imap(*idx):
            s = step_fn(*idx[:-1])
            return (layer * n_pool + idx[-1][s // spb, page_lo + (s % spb) * npp + i], 0)
        return pl.BlockSpec((PAGE_ROWS, LANES), imap)

    tok_q = pl.BlockSpec((ROWS_S, W), lambda *idx: (seq(idx), COL_Q // W))
    pages = [page(i) for i in range(npp)]
    part = pl.BlockSpec((8, LANES), lambda *idx: (seq(idx), 0))
    return [tok_q] + pages + pages, [part, part, part]


def _rider_parts(step, spb, tok_q, k_refs, v_refs, pm_ref, pl_ref, pa_ref, m_s, l_s, acc_s):
    npp = len(k_refs)
    j = lax.rem(step, spb)

    def first():
        m_s[...] = jnp.full(m_s.shape, -jnp.inf, F32)
        l_s[...] = jnp.zeros(l_s.shape, F32)
        acc_s[...] = jnp.zeros(acc_s.shape, F32)

    st = {"s": [], "pv": None}

    def scores(lo, hi):
        if not st["s"]:
            st["q"] = _per_head_rows(tok_q[0:1, :] * (DIFF_DK ** -0.5), True).astype(BF16)
        st["s"] += [_dot_nt(st["q"], k_refs[i][...].astype(BF16)) for i in range(lo, hi)]

    def softmax():
        s_all = jnp.concatenate(st["s"], axis=1)
        s_all = jnp.where(_head_match(npp * PAGE_ROWS), s_all, -jnp.inf)
        m_prev = m_s[...]
        m_new = jnp.maximum(m_prev, jnp.max(s_all, axis=-1, keepdims=True))
        st["alpha"] = jnp.exp(m_prev - m_new)
        p = jnp.exp(s_all - m_new[:, 0:1])
        l_s[...] = st["alpha"] * l_s[...] + jnp.sum(p, axis=-1, keepdims=True)
        m_s[...] = m_new
        st["p"] = p.astype(BF16)

    def weighted(lo, hi):
        for i in range(lo, hi):
            pv = _dot(st["p"][:, i * PAGE_ROWS:(i + 1) * PAGE_ROWS], v_refs[i][...].astype(BF16))
            st["pv"] = pv if st["pv"] is None else st["pv"] + pv
        if hi == npp:
            acc_s[...] = st["alpha"] * acc_s[...] + st["pv"]

    def last():
        pm_ref[...] = m_s[...]
        pl_ref[...] = l_s[...]
        pa_ref[...] = acc_s[...]

    return j, first, (scores, softmax, weighted), last


def _decode_finish_kernel(tok_q, tok_k, tok_v, tok_mq, mk_ref, mv_ref, lam_ref, gsub_ref, *rest, n_parts, lam_init):
    parts = rest[:3 * n_parts]
    od_ref, om_ref = rest[3 * n_parts:]
    qmat = _per_head_rows(tok_q[0:1, :] * (DIFF_DK ** -0.5), True)
    s_new = jnp.sum(qmat * _per_head_rows(tok_k[0:1, :], False), axis=-1, keepdims=True)
    m_tot = jnp.broadcast_to(s_new, (8, LANES))
    for h in range(n_parts):
        m_tot = jnp.maximum(m_tot, parts[3 * h][...])
    w_new = jnp.exp(s_new - m_tot)
    l_tot = w_new
    acc = w_new * _per_head_rows(tok_v[0:1, :], False)
    for h in range(n_parts):
        w = jnp.exp(parts[3 * h][...] - m_tot)
        l_tot = l_tot + w * parts[3 * h + 1][...]
        acc = acc + w * parts[3 * h + 2][...]
    lam = _lambda_from(lam_ref, lam_init)
    o_all = acc / l_tot
    outs = []
    for h in range(DIFF_HEADS):
        o = o_all[2 * h:2 * h + 1, :] - lam * o_all[2 * h + 1:2 * h + 2, :]
        r = lax.rsqrt(jnp.mean(o * o, axis=-1, keepdims=True) + EPS)
        outs.append((o * r * gsub_ref[...]) * (1.0 - lam_init))
    od_ref[...] = jnp.broadcast_to(jnp.concatenate(outs, axis=1), od_ref.shape)

    mqm = _per_head_rows(tok_mq[0:1, :], False)
    s = _dot_nt(mqm.astype(BF16), mk_ref[...].astype(BF16)) * (MEM_HD ** -0.5)
    s = jnp.where(_head_match(s.shape[1]), s, -jnp.inf)
    e = jnp.exp(s - jnp.max(s, axis=-1, keepdims=True))
    om = _dot(e.astype(BF16), mv_ref[...].astype(BF16)) / jnp.sum(e, axis=-1, keepdims=True)
    om_row = jnp.concatenate([om[2 * h:2 * h + 1, :] for h in range(MEM_HEADS)], axis=1)
    om_ref[...] = jnp.broadcast_to(om_row, om_ref.shape)


def _decode_finish(u_s, parts, mem_k, mem_v, lam_pack, gsub, layer, n_b, lam_init):
    W = DIFF_HEADS * LANES
    mem_rows = MEM_TOKENS * MEM_HEADS
    flat = [a for tri in parts for a in tri]

    def tok(col):
        return pl.BlockSpec((ROWS_S, W), lambda b: (b, col // W))

    od, om = pl.pallas_call(
        functools.partial(_decode_finish_kernel, n_parts=len(parts), lam_init=lam_init),
        grid=(n_b,),
        in_specs=[tok(COL_Q), tok(COL_K), tok(COL_V), tok(COL_MQ),
                  pl.BlockSpec((mem_rows, LANES), lambda b: (layer * n_b + b, 0)),
                  pl.BlockSpec((mem_rows, LANES), lambda b: (layer * n_b + b, 0)),
                  pl.BlockSpec((8, LANES), lambda b: (0, 0)),
                  pl.BlockSpec((1, LANES), lambda b: (0, 0))]
                 + [pl.BlockSpec((8, LANES), lambda b: (b, 0))] * len(flat),
        out_specs=[pl.BlockSpec((8, W), lambda b: (b, 0)), pl.BlockSpec((8, W), lambda b: (b, 0))],
        out_shape=[jax.ShapeDtypeStruct((n_b * 8, W), F32), jax.ShapeDtypeStruct((n_b * 8, W), F32)],
        compiler_params=_cparams(("parallel",)),
        name="decode_finish",
    )(u_s, u_s, u_s, u_s, mem_k, mem_v, lam_pack, gsub, *flat)
    return od.reshape(n_b, 8, W)[:, 0], om.reshape(n_b, 8, W)[:, 0]


def _outproj_kernel(x_ref, y_ref, od_ref, om_ref, w1_ref, w2_ref, w3_ref, o_ref):
    o_ref[...] = (x_ref[...]
                  + _dot(y_ref[...].astype(BF16), w1_ref[...])
                  + _dot(od_ref[...].astype(BF16), w2_ref[...])
                  + _dot(om_ref[...].astype(BF16), w3_ref[...]))


def _outproj(x, y, od, om, w_out, tm=1024, tn=512):
    m = x.shape[0]
    tm = min(tm, m)
    assert m % tm == 0
    return pl.pallas_call(
        _outproj_kernel,
        grid=(m // tm, D_MODEL // tn),
        in_specs=[
            pl.BlockSpec((tm, tn), lambda i, j: (i, j)),
            pl.BlockSpec((tm, D_SSM), lambda i, j: (i, 0)),
            pl.BlockSpec((tm, D_DIFF), lambda i, j: (i, 0)),
            pl.BlockSpec((tm, D_MEMX), lambda i, j: (i, 0)),
            pl.BlockSpec((D_SSM, tn), lambda i, j: (0, j)),
            pl.BlockSpec((D_DIFF, tn), lambda i, j: (D_SSM // D_DIFF, j)),
            pl.BlockSpec((D_MEMX, tn), lambda i, j: ((D_SSM + D_DIFF) // D_MEMX, j)),
        ],
        out_specs=pl.BlockSpec((tm, tn), lambda i, j: (i, j)),
        out_shape=jax.ShapeDtypeStruct((m, D_MODEL), F32),
        compiler_params=_cparams(("parallel", "arbitrary")),
        name="out_proj",
    )(x, y, od, om, w_out, w_out, w_out)


def _mlp_kernel(*refs, npp, spb):
    if npp:
        _, x_ref, g_ref, wu_ref, wd_ref, tok_q = refs[:6]
        k_refs, v_refs = refs[6:6 + npp], refs[6 + npp:6 + 2 * npp]
        o_ref, pm_ref, pl_ref, pa_ref, h_ref, m_s, l_s, acc_s = refs[6 + 2 * npp:]
    else:
        x_ref, g_ref, wu_ref, wd_ref, o_ref, h_ref = refs
    f = pl.program_id(1)

    @pl.when(f == 0)
    def _():
        x = x_ref[...]
        r = lax.rsqrt(jnp.mean(x * x, axis=-1, keepdims=True) + EPS)
        h_ref[...] = (x * r * g_ref[...]).astype(BF16)
        o_ref[...] = x

    if npp:
        step = pl.program_id(0) * pl.num_programs(1) + f
        j, first, main, last = _rider_parts(step, spb, tok_q, k_refs, v_refs, pm_ref, pl_ref, pa_ref,
                                            m_s, l_s, acc_s)
        pl.when(j == 0)(first)

    a = jnp.maximum(_dot(h_ref[...], wu_ref[...]), 0.0)
    o_ref[...] += _dot((a * a).astype(BF16), wd_ref[...])
    if npp:
        scores, softmax, weighted = main
        scores(0, npp)
        softmax()
        weighted(0, npp)
        pl.when(j == spb - 1)(last)


def _mlp(x, g, w_up, w_down, tm=512, tf=1024, rider=None):
    m = x.shape[0]
    d_ff = w_up.shape[1]
    tm = min(tm, m)
    assert m % tm == 0 and d_ff % tf == 0
    grid = (m // tm, d_ff // tf)
    in_specs = [
        pl.BlockSpec((tm, D_MODEL), lambda i, f: (i, 0)),
        pl.BlockSpec((1, D_MODEL), lambda i, f: (0, 0)),
        pl.BlockSpec((D_MODEL, tf), lambda i, f: (0, f)),
        pl.BlockSpec((tf, D_MODEL), lambda i, f: (f, 0)),
    ]
    out_specs = [pl.BlockSpec((tm, D_MODEL), lambda i, f: (i, 0))]
    out_shape = [jax.ShapeDtypeStruct((m, D_MODEL), F32)]
    scratch = [pltpu.VMEM((tm, D_MODEL), BF16)]
    args = [x, g, w_up, w_down]
    if rider is None:
        out = pl.pallas_call(
            functools.partial(_mlp_kernel, npp=0, spb=0),
            grid=grid, in_specs=in_specs, out_specs=out_specs, out_shape=out_shape, scratch_shapes=scratch,
            compiler_params=_cparams(("parallel", "arbitrary")), name="mlp",
        )(*args)
        return out[0]
    u_s, cache_k, cache_v, page_table, layer, n_pool, npp, page_lo, n_pages = rider
    n_b = page_table.shape[0]
    spb = n_pages // npp
    assert n_pages % npp == 0 and grid[0] * grid[1] == n_b * spb
    nf = grid[1]
    r_in, r_out = _rider_specs(lambda i, f: i * nf + f, spb, npp, page_lo, layer, n_pool)
    host_in = [pl.BlockSpec(s.block_shape, lambda i, f, pt, im=s.index_map: im(i, f)) for s in in_specs]
    host_out = [pl.BlockSpec(s.block_shape, lambda i, f, pt, im=s.index_map: im(i, f)) for s in out_specs]
    grid_spec = pltpu.PrefetchScalarGridSpec(
        num_scalar_prefetch=1, grid=grid,
        in_specs=host_in + r_in, out_specs=host_out + r_out,
        scratch_shapes=scratch + [pltpu.VMEM((8, LANES), F32)] * 3,
    )
    part_shape = jax.ShapeDtypeStruct((n_b * 8, LANES), F32)
    out, pm, pl_, pa = pl.pallas_call(
        functools.partial(_mlp_kernel, npp=npp, spb=spb),
        grid_spec=grid_spec,
        out_shape=out_shape + [part_shape] * 3,
        compiler_params=_cparams(("arbitrary", "arbitrary")),
        name="mlp_rider",
    )(page_table, *args, u_s, *([cache_k] * npp), *([cache_v] * npp))
    return out, (pm, pl_, pa)


def _pad_lanes(v, width=LANES):
    v = v.reshape(1, -1).astype(F32)
    return jnp.pad(v, ((0, 0), (0, width - v.shape[1])))


def kernel(x_prompt, x_sample, mem_prompt, cache_diff_k, cache_diff_v, cache_mem_k, cache_mem_v, state_conv, state_ssm, page_table, norm_mix, w_in, conv_w, conv_b, dt_bias, a_log, d_skip, g_ssm, g_q, g_k, lambda_q1, lambda_k1, lambda_q2, lambda_k2, g_subln, norm_mem, w_mem_kv, g_mq, g_mk, w_out, norm_mlp, w_up, w_down):
    depth = w_in.shape[0]
    bp, seq, _ = x_prompt.shape
    bd, dec_seq, _ = x_sample.shape
    assert dec_seq == 1 and seq % SSD_T == 0
    n_pool = cache_diff_k.shape[1]

    xp = x_prompt.reshape(bp * seq, D_MODEL)
    xs = x_sample.reshape(bd, D_MODEL)
    cache_k = cache_diff_k.reshape(depth * n_pool * PAGE_ROWS, LANES)
    cache_v = cache_diff_v.reshape(depth * n_pool * PAGE_ROWS, LANES)
    mem_k = cache_mem_k.reshape(depth * bd * MEM_TOKENS * MEM_HEADS, MEM_HD)
    mem_v = cache_mem_v.reshape(depth * bd * MEM_TOKENS * MEM_HEADS, MEM_HD)

    outs = [[] for _ in range(10)]
    for l in range(depth):
        lam_init = 0.8 - 0.6 * math.exp(-0.3 * l)
        wl = w_in[l]
        o_dt = D_SSM + CONV_DIM
        o_q = o_dt + SSM_HEADS
        w_main = [wl[:, :o_dt].astype(BF16), wl[:, o_q:].astype(BF16)]
        w_dt = jnp.pad(wl[:, o_dt:o_q], ((0, 0), (0, LANES - SSM_HEADS))).astype(BF16)
        ones = jnp.ones((PROJ_TN,), F32)
        gains = jnp.concatenate(
            [ones] * 5 + [jnp.tile(g_q[l], 2 * DIFF_HEADS), jnp.tile(g_k[l], 2 * DIFF_HEADS), ones,
                          jnp.tile(g_mq[l], MEM_HEADS)]).reshape(1, N_MAIN)
        modes = (0, 0, 0, 0, 0, DIFF_DK, DIFF_DK, 0, MEM_HD)
        g_mix = norm_mix[l].reshape(1, D_MODEL)

        ssm_p = (jnp.pad(conv_w[l], ((0, 8 - CONV_W), (0, 0))),
                 conv_b[l].reshape(1, CONV_DIM),
                 _pad_lanes(dt_bias[l]), _pad_lanes(a_log[l]),
                 jnp.repeat(d_skip[l], SSM_HEAD_DIM).reshape(1, D_SSM),
                 g_ssm[l].reshape(1, D_SSM))
        lam_pack = jnp.concatenate(
            [_pad_lanes(lambda_q1[l]), _pad_lanes(lambda_k1[l]), _pad_lanes(lambda_q2[l]),
             _pad_lanes(lambda_k2[l]), jnp.zeros((4, LANES), F32)], axis=0)
        gsub = g_subln[l].reshape(1, LANES)
        w_out_bf = w_out[l].astype(BF16)
        w_up_bf = w_up[l].astype(BF16)
        w_down_bf = w_down[l].astype(BF16)
        g_mlp = norm_mlp[l].reshape(1, D_MODEL)

        xs_pad = jnp.pad(xs.reshape(bd, 1, D_MODEL), ((0, 0), (0, ROWS_S - 1), (0, 0))).reshape(bd * ROWS_S, D_MODEL)
        u_s, dt_s = _norm_proj(xs_pad, g_mix, w_main, gains, modes, w_dt=w_dt)

        u_p, dt_p, k_rows, v_rows = _norm_proj(xp, g_mix, w_main, gains, modes, w_dt=w_dt,
                                               rows_blocks=(COL_K // PROJ_TN, COL_V // PROJ_TN))
        y_p, conv_p, h_p = _ssd(u_p, dt_p, bp, seq, seq, ssm_p)
        mem_gains = jnp.concatenate([jnp.tile(g_mk[l], MEM_HEADS), ones]).reshape(1, 2 * D_MEMX)
        mkv, = _norm_proj(mem_prompt.reshape(bp * MEM_TOKENS, D_MODEL), norm_mem[l].reshape(1, D_MODEL),
                          [w_mem_kv[l].astype(BF16)], mem_gains, (MEM_HD, 0))
        od_p, om_p = _attn_prompt(u_p, mkv, lam_pack, gsub, bp, seq, lam_init)
        xp_mid = _outproj(xp, y_p, od_p, om_p, w_out_bf)
        n_pages = page_table.shape[1]
        xp_new, part = _mlp(xp_mid, g_mlp, w_up_bf, w_down_bf, tf=512,
                            rider=(u_s, cache_k, cache_v, page_table, l, n_pool, 16, 0, n_pages))

        outs[0].append(k_rows.reshape(bp, seq, DIFF_HEADS, LANES))
        outs[1].append(v_rows.reshape(bp, seq, DIFF_HEADS, LANES))
        outs[4].append(mkv[:, :D_MEMX].reshape(bp, MEM_TOKENS, MEM_HEADS, MEM_HD))
        outs[5].append(mkv[:, D_MEMX:].reshape(bp, MEM_TOKENS, MEM_HEADS, MEM_HD))
        outs[6].append(conv_p[:, :CONV_W - 1])
        outs[7].append(h_p.reshape(bp, SSM_HEADS, SSM_HEAD_DIM, D_STATE))

        conv_prev = jnp.pad(state_conv[l], ((0, 0), (8 - (CONV_W - 1), 0), (0, 0)))
        y_s, conv_s, h_s = _ssd(u_s, dt_s, bd, ROWS_S, 1, ssm_p, conv_prev=conv_prev,
                                h0=state_ssm[l].reshape(bd, D_SSM, D_STATE))
        od_s, om_s = _decode_finish(u_s, [part], mem_k, mem_v, lam_pack, gsub, l, bd, lam_init)
        y_s0 = y_s.reshape(bd, ROWS_S, D_SSM)[:, 0]
        xs_mid = _outproj(xs, y_s0, od_s, om_s, w_out_bf)
        xs_new = _mlp(xs_mid, g_mlp, w_up_bf, w_down_bf)
        u_s0 = u_s.reshape(bd, ROWS_S, N_MAIN)[:, 0]
        outs[2].append(u_s0[:, COL_K:COL_K + D_DIFF].reshape(bd, 1, DIFF_HEADS, LANES))
        outs[3].append(u_s0[:, COL_V:COL_V + D_DIFF].reshape(bd, 1, DIFF_HEADS, LANES))
        outs[8].append(conv_s[:, :CONV_W - 1])
        outs[9].append(h_s.reshape(bd, SSM_HEADS, SSM_HEAD_DIM, D_STATE))

        xp, xs = xp_new, xs_new

    st = [jnp.stack(o) for o in outs]
    return (xp.reshape(bp, seq, D_MODEL), xs.reshape(bd, 1, D_MODEL),
            st[0], st[1], st[2], st[3], st[4], st[5], st[6], st[7], st[8], st[9])
```

```python
import functools
import math

import jax
import jax.numpy as jnp
from jax import lax
from jax.experimental import pallas as pl
from jax.experimental.pallas import tpu as pltpu

F32 = jnp.float32
BF16 = jnp.bfloat16

D_MODEL = 2048
D_SSM = 1024
D_DIFF = 512
D_MEMX = 512
SSM_HEADS = 16
SSM_HEAD_DIM = 64
SSM_GROUPS = 2
D_STATE = 128
CONV_W = 4
BC_DIM = 2 * SSM_GROUPS * D_STATE
CONV_DIM = D_SSM + BC_DIM
DIFF_HEADS = 4
DIFF_DK = 64
MEM_HEADS = 4
MEM_HD = 128
MEM_TOKENS = 256
D_FF = 4 * D_MODEL
EPS = 1e-6
LOG2E = math.log2(math.e)

LANES = 128
SUBLANES_BF16 = 16
VMEM_LIMIT_BYTES = 56 * 1024 * 1024

N_MAIN = D_SSM + D_SSM + BC_DIM + 3 * D_DIFF + D_MEMX
COL_Z, COL_XS, COL_BC, COL_Q, COL_K, COL_V, COL_MQ = 0, 1024, 2048, 2560, 3072, 3584, 4096
PROJ_TN = 512
SSD_T = 128
ROWS_S = SUBLANES_BF16


def _cparams(sem):
    return pltpu.CompilerParams(dimension_semantics=sem, vmem_limit_bytes=VMEM_LIMIT_BYTES)


def _dot(a, b):
    return jnp.dot(a, b, preferred_element_type=F32)


def _dot_nt(a, b):
    return lax.dot_general(a, b, (((1,), (1,)), ((), ())), preferred_element_type=F32)


def _silu(x):
    return x * (1.0 / (1.0 + jnp.exp(-x)))


def _group_rmsnorm_slab(a, gain, group):
    sq = a * a
    if group == LANES:
        r = lax.rsqrt(jnp.sum(sq, axis=-1, keepdims=True) * (1.0 / LANES) + EPS)
    else:
        lane = lax.broadcasted_iota(jnp.int32, a.shape, 1)
        lo = lane < group
        s_lo = jnp.sum(jnp.where(lo, sq, 0.0), axis=-1, keepdims=True)
        s_hi = jnp.sum(jnp.where(lo, 0.0, sq), axis=-1, keepdims=True)
        r = jnp.where(lo, lax.rsqrt(s_lo * (1.0 / group) + EPS),
                      lax.rsqrt(s_hi * (1.0 / group) + EPS))
    return a * r * gain


def _proj_kernel(*refs, modes, seg_of, n_w, has_dt, rows_blocks):
    x_ref, g_ref = refs[:2]
    w_refs = refs[2:2 + n_w]
    gain_ref = refs[2 + n_w]
    pos = 3 + n_w
    if has_dt:
        wdt_ref = refs[pos]
        pos += 1
    u_ref = refs[pos]
    pos += 1
    if has_dt:
        dt_ref = refs[pos]
        pos += 1
    rows_refs = refs[pos:pos + len(rows_blocks)]
    h_ref = refs[pos + len(rows_blocks)]
    j = pl.program_id(1)
    tm = x_ref.shape[0]

    @pl.when(j == 0)
    def _():
        x = x_ref[...]
        r = lax.rsqrt(jnp.mean(x * x, axis=-1, keepdims=True) + EPS)
        h_ref[...] = (x * r * g_ref[...]).astype(BF16)
        if has_dt:
            dt_ref[...] = _dot(h_ref[...], wdt_ref[...])

    keys = {}
    for jj, m in enumerate(modes):
        ri = rows_blocks.index(jj) if jj in rows_blocks else None
        keys.setdefault((seg_of[jj], m, ri), []).append(jj)
    for (seg, mode, ri), jjs in keys.items():
        cond = j == jjs[0]
        for jj in jjs[1:]:
            cond = jnp.logical_or(cond, j == jj)

        @pl.when(cond)
        def _(seg=seg, mode=mode, ri=ri):
            acc = _dot(h_ref[...], w_refs[seg][...].astype(BF16))
            if mode == 0 and ri is None:
                u_ref[...] = acc
                return
            gain = gain_ref[...]
            for s in range(acc.shape[1] // LANES):
                sl = slice(s * LANES, (s + 1) * LANES)
                slab = acc[:, sl]
                if mode:
                    slab = _group_rmsnorm_slab(slab, gain[:, sl], mode)
                u_ref[:, sl] = slab
                if ri is not None:
                    rows_refs[ri][pl.ds(s, tm, stride=PROJ_TN // LANES), :] = slab


def _norm_proj(x, g, ws, gains, modes, w_dt=None, rows_blocks=(), tm=1024):
    m, k = x.shape
    tm = min(tm, m)
    nblk = [nb for _, nb in ws]
    ws = [w for w, _ in ws]
    assert m % tm == 0 and sum(nblk) == len(modes) and all(w.shape[1] >= nb * PROJ_TN for w, nb in zip(ws, nblk))
    seg_of, offs = [], []
    for a, nb in enumerate(nblk):
        offs.append(len(seg_of))
        seg_of += [a] * nb
    n = PROJ_TN * len(modes)
    has_dt = w_dt is not None

    def w_spec(a):
        return pl.BlockSpec(
            (k, PROJ_TN), lambda i, j, a=a: (0, jnp.minimum(jnp.maximum(j - offs[a], 0), nblk[a] - 1)))

    in_specs = [pl.BlockSpec((tm, k), lambda i, j: (i, 0)), pl.BlockSpec((1, k), lambda i, j: (0, 0))]
    in_specs += [w_spec(a) for a in range(len(ws))]
    in_specs.append(pl.BlockSpec((1, PROJ_TN), lambda i, j: (0, j)))
    args = [x, g, *ws, gains]
    out_shape = [jax.ShapeDtypeStruct((m, n), F32)]
    out_specs = [pl.BlockSpec((tm, PROJ_TN), lambda i, j: (i, j))]
    if has_dt:
        in_specs.append(pl.BlockSpec((k, LANES), lambda i, j: (0, 0)))
        args.append(w_dt)
        out_shape.append(jax.ShapeDtypeStruct((m, LANES), F32))
        out_specs.append(pl.BlockSpec((tm, LANES), lambda i, j: (i, 0)))
    heads = PROJ_TN // LANES
    for _ in rows_blocks:
        out_shape.append(jax.ShapeDtypeStruct((m * heads, LANES), F32))
        out_specs.append(pl.BlockSpec((tm * heads, LANES), lambda i, j: (i, 0)))
    return pl.pallas_call(
        functools.partial(_proj_kernel, modes=tuple(modes), seg_of=tuple(seg_of), n_w=len(ws), has_dt=has_dt,
                          rows_blocks=tuple(rows_blocks)),
        grid=(m // tm, len(modes)),
        in_specs=in_specs,
        out_specs=out_specs,
        out_shape=out_shape,
        scratch_shapes=[pltpu.VMEM((tm, k), BF16)],
        compiler_params=_cparams(("parallel", "arbitrary")),
        name="norm_proj",
    )(*args)


def _split3(x):
    hi = x.astype(BF16)
    r1 = x - hi.astype(F32)
    mid = r1.astype(BF16)
    lo = (r1 - mid.astype(F32)).astype(BF16)
    return hi, mid, lo


def _ssd_kernel(*refs, t_in, valid_last, nc, has_init):
    if has_init:
        (xs_ref, z_ref, bc_ref, dt_ref, cprev_ref, h0_ref, convw_ref, convb_ref, dtb_ref, alog_ref,
         dskip_ref, gssm_ref, y_ref, cout_ref, hout_ref, xpad, hst) = refs
    else:
        (xs_ref, z_ref, bc_ref, dt_ref, convw_ref, convb_ref, dtb_ref, alog_ref,
         dskip_ref, gssm_ref, y_ref, cout_ref, hout_ref, xpad, hst) = refs
    T = SSD_T
    c = pl.program_id(1)

    @pl.when(c == 0)
    def _():
        if has_init:
            xpad[0:8, :] = cprev_ref[0]
            hst[...] = h0_ref[0]
        else:
            xpad[0:8, :] = jnp.zeros((8, CONV_DIM), F32)
            hst[...] = jnp.zeros(hst.shape, F32)

    def rows(ref):
        v = ref[...]
        if t_in < T:
            v = jnp.concatenate([v, jnp.zeros((T - t_in, v.shape[1]), F32)], axis=0)
        return v

    xpad[8:8 + T, 0:D_SSM] = rows(xs_ref)
    xpad[8:8 + T, D_SSM:CONV_DIM] = rows(bc_ref)
    z = rows(z_ref)
    dt_raw = rows(dt_ref)

    convw = convw_ref[...]
    conv = convb_ref[...] + xpad[5:5 + T, :] * convw[0:1, :]
    for jtap in range(1, CONV_W):
        conv = conv + xpad[5 + jtap:5 + jtap + T, :] * convw[jtap:jtap + 1, :]
    xact = _silu(conv)

    row_i = lax.broadcasted_iota(jnp.int32, (T, LANES), 0)
    col_i = lax.broadcasted_iota(jnp.int32, (T, LANES), 1)
    xv = dt_raw + dtb_ref[...]
    dt = jnp.maximum(xv, 0.0) + jnp.log1p(jnp.exp(-jnp.abs(xv)))
    if valid_last < T:
        dt = jnp.where(row_i < valid_last, dt, 0.0)
    a_neg = -jnp.exp(alog_ref[...])
    adt = dt * a_neg

    tril = (row_i >= col_i)
    tril_bf = jnp.where(tril, 1.0, 0.0).astype(BF16)
    a_hi, a_mid, a_lo = _split3(adt)
    acum = _dot(tril_bf, a_hi) + _dot(tril_bf, a_mid) + _dot(tril_bf, a_lo)
    acum_t = acum.T
    dt_t = dt.T
    e_acum = jnp.exp(acum)
    a_last = acum[T - 1:T, :]
    w_state = jnp.exp(a_last - acum) * dt
    da_last = jnp.exp(a_last)

    lo_half = col_i < SSM_HEAD_DIM
    neg_big = jnp.float32(-1e30)

    def colb(tile, r):
        return jnp.broadcast_to(tile[:, r:r + 1], (T, LANES))

    y_slabs = []
    for g in range(SSM_GROUPS):
        b_g = xact[:, D_SSM + g * D_STATE:D_SSM + (g + 1) * D_STATE]
        c_g = xact[:, D_SSM + SSM_GROUPS * D_STATE + g * D_STATE:
                   D_SSM + SSM_GROUPS * D_STATE + (g + 1) * D_STATE]
        b_bf = b_g.astype(BF16)
        c_bf = c_g.astype(BF16)
        cb = _dot_nt(c_bf, b_bf)
        for pp in range(SSM_HEADS // SSM_GROUPS // 2):
            p = g * (SSM_HEADS // SSM_GROUPS // 2) + pp
            sl = slice(p * LANES, (p + 1) * LANES)
            xs_slab = xact[:, sl]
            y_acc = dskip_ref[:, sl] * xs_slab
            for hh in range(2):
                r = 2 * p + hh
                seg = colb(acum, r) - acum_t[r:r + 1, :]
                lmat = jnp.exp(jnp.where(tril, seg, neg_big))
                mr = (cb * lmat * dt_t[r:r + 1, :]).astype(BF16)
                xh = jnp.where(lo_half if hh == 0 else jnp.logical_not(lo_half), xs_slab, 0.0)
                y_acc = y_acc + _dot(mr, xh.astype(BF16))
            hpair = hst[sl, :]
            e_pair = jnp.where(lo_half, colb(e_acum, 2 * p), colb(e_acum, 2 * p + 1))
            y_acc = y_acc + e_pair * _dot_nt(c_bf, hpair.astype(BF16))
            w_pair = jnp.where(lo_half, colb(w_state, 2 * p), colb(w_state, 2 * p + 1))
            xw_t = (xs_slab * w_pair).T
            st = _dot(xw_t.astype(BF16), b_bf)
            da = jnp.concatenate(
                [jnp.broadcast_to(da_last[:, 2 * p:2 * p + 1], (SSM_HEAD_DIM, LANES)),
                 jnp.broadcast_to(da_last[:, 2 * p + 1:2 * p + 2], (SSM_HEAD_DIM, LANES))], axis=0)
            hst[sl, :] = da * hpair + st
            y_slabs.append(y_acc * _silu(z[:, sl]))

    per_group = D_SSM // SSM_GROUPS // LANES
    for g in range(SSM_GROUPS):
        slabs = y_slabs[g * per_group:(g + 1) * per_group]
        ssum = jnp.sum(slabs[0] * slabs[0], axis=-1, keepdims=True)
        for s in slabs[1:]:
            ssum = ssum + jnp.sum(s * s, axis=-1, keepdims=True)
        r = lax.rsqrt(ssum * (1.0 / (per_group * LANES)) + EPS)
        for k, s in enumerate(slabs):
            sl = slice((g * per_group + k) * LANES, (g * per_group + k + 1) * LANES)
            y_ref[:, sl] = (s * r * gssm_ref[:, sl])[0:t_in, :].astype(y_ref.dtype)

    if nc > 1:
        xpad[5:8, :] = xpad[5 + T:8 + T, :]

    @pl.when(c == nc - 1)
    def _():
        cout_ref[...] = jnp.zeros(cout_ref.shape, F32)
        if nc > 1:
            cout_ref[0, 0:CONV_W - 1, :] = xpad[5:8, :]
        else:
            cout_ref[0, 0:CONV_W - 1, :] = xpad[5 + valid_last:8 + valid_last, :]
        hout_ref[0] = hst[...]


def _ssd(u, dt, n_batch, seq_rows, valid_len, ssm_p, conv_prev=None, h0=None):
    t_in = min(seq_rows, SSD_T)
    nc = max(seq_rows // SSD_T, 1)
    valid_last = valid_len - (nc - 1) * SSD_T
    has_init = conv_prev is not None
    convw, convb, dtb, alog, dskip, gssm = ssm_p
    ntb = N_MAIN // PROJ_TN

    def row(b, c):
        return b * nc + c

    in_specs = [
        pl.BlockSpec((t_in, D_SSM), lambda b, c: (row(b, c), COL_XS // D_SSM)),
        pl.BlockSpec((t_in, D_SSM), lambda b, c: (row(b, c), COL_Z // D_SSM)),
        pl.BlockSpec((t_in, BC_DIM), lambda b, c: (row(b, c), COL_BC // BC_DIM)),
        pl.BlockSpec((t_in, LANES), lambda b, c: (row(b, c), 0)),
    ]
    args = [u, u, u, dt]
    if has_init:
        in_specs += [pl.BlockSpec((1, 8, CONV_DIM), lambda b, c: (b, 0, 0)),
                     pl.BlockSpec((1, D_SSM, D_STATE), lambda b, c: (b, 0, 0))]
        args += [conv_prev, h0]
    in_specs += [
        pl.BlockSpec((8, CONV_DIM), lambda b, c: (0, 0)),
        pl.BlockSpec((1, CONV_DIM), lambda b, c: (0, 0)),
        pl.BlockSpec((1, LANES), lambda b, c: (0, 0)),
        pl.BlockSpec((1, LANES), lambda b, c: (0, 0)),
        pl.BlockSpec((1, D_SSM), lambda b, c: (0, 0)),
        pl.BlockSpec((1, D_SSM), lambda b, c: (0, 0)),
    ]
    args += [convw, convb, dtb, alog, dskip, gssm]
    del ntb
    return pl.pallas_call(
        functools.partial(_ssd_kernel, t_in=t_in, valid_last=valid_last, nc=nc, has_init=has_init),
        grid=(n_batch, nc),
        in_specs=in_specs,
        out_specs=[
            pl.BlockSpec((t_in, D_SSM), lambda b, c: (row(b, c), 0)),
            pl.BlockSpec((1, 8, CONV_DIM), lambda b, c: (b, 0, 0)),
            pl.BlockSpec((1, D_SSM, D_STATE), lambda b, c: (b, 0, 0)),
        ],
        out_shape=[
            jax.ShapeDtypeStruct((n_batch * seq_rows, D_SSM), BF16),
            jax.ShapeDtypeStruct((n_batch, 8, CONV_DIM), F32),
            jax.ShapeDtypeStruct((n_batch, D_SSM, D_STATE), F32),
        ],
        scratch_shapes=[pltpu.VMEM((8 + SSD_T, CONV_DIM), F32), pltpu.VMEM((D_SSM, D_STATE), F32)],
        compiler_params=_cparams(("parallel", "arbitrary")),
        name="ssd_scan",
    )(*args)


def _lambda_from(lam_ref, lam_init):
    lp = lam_ref[...]
    s1 = jnp.sum(lp[0:1, :] * lp[1:2, :], axis=-1, keepdims=True)
    s2 = jnp.sum(lp[2:3, :] * lp[3:4, :], axis=-1, keepdims=True)
    return jnp.exp(s1) - jnp.exp(s2) + lam_init


def _attn_prompt_kernel(q_ref, k_ref, v_ref, mq_ref, mk_ref, mv_ref, lam_ref, gsub_ref,
                        od_ref, om_ref, kb, vb, s_s, m_s, l_s, acc_s, *, tq, lam_init):
    qi = pl.program_id(2)

    @pl.when(qi == 0)
    def _():
        kb[...] = k_ref[...].astype(BF16)
        vb[...] = v_ref[...].astype(BF16)

    lane = lax.broadcasted_iota(jnp.int32, (tq, LANES), 1)
    lo = lane < DIFF_DK
    q = q_ref[...] * (DIFF_DK ** -0.5 * LOG2E)
    qm = (jnp.where(lo, q, 0.0).astype(BF16), jnp.where(lo, 0.0, q).astype(BF16))
    m_s[...] = jnp.full(m_s.shape, -jnp.inf, F32)
    l_s[...] = jnp.zeros(l_s.shape, F32)
    acc_s[...] = jnp.zeros(acc_s.shape, F32)
    reps = tq // LANES

    def scores(j, masked):
        start = pl.multiple_of(j * tq, tq)
        k_blk = kb[pl.ds(start, tq), :]
        for mi in range(2):
            s = _dot_nt(qm[mi], k_blk)
            if masked:
                r_i = lax.broadcasted_iota(jnp.int32, (tq, tq), 0)
                c_i = lax.broadcasted_iota(jnp.int32, (tq, tq), 1)
                s = jnp.where(r_i >= c_i, s, -jnp.inf)
            s_s[mi, :, pl.ds(start, tq)] = s
            m_s[mi] = jnp.maximum(m_s[mi], jnp.max(s, axis=-1, keepdims=True))

    def weighted(j):
        start = pl.multiple_of(j * tq, tq)
        v_blk = vb[pl.ds(start, tq), :]
        for mi in range(2):
            m_rep = jnp.concatenate([m_s[mi]] * reps, axis=1)
            p = jnp.exp2(s_s[mi, :, pl.ds(start, tq)] - m_rep)
            l_s[mi] += jnp.sum(p, axis=-1, keepdims=True)
            acc_s[mi] += _dot(p.astype(BF16), v_blk)

    def body1(j, carry):
        scores(j, False)
        return carry

    def body2(j, carry):
        weighted(j)
        return carry

    lax.fori_loop(0, qi, body1, 0)
    scores(qi, True)
    lax.fori_loop(0, qi + 1, body2, 0)

    lam = _lambda_from(lam_ref, lam_init)
    o = acc_s[0] / l_s[0] - lam * (acc_s[1] / l_s[1])
    r = lax.rsqrt(jnp.mean(o * o, axis=-1, keepdims=True) + EPS)
    od_ref[...] = ((o * r * gsub_ref[...]) * (1.0 - lam_init)).astype(od_ref.dtype)

    s = _dot_nt(mq_ref[...].astype(BF16), mk_ref[...].astype(BF16)) * (MEM_HD ** -0.5 * LOG2E)
    e = jnp.exp2(s - jnp.max(s, axis=-1, keepdims=True))
    om = _dot(e.astype(BF16), mv_ref[...].astype(BF16)) / jnp.sum(e, axis=-1, keepdims=True)
    om_ref[...] = om.astype(om_ref.dtype)


def _attn_prompt(u, mkv, lam_pack, gsub, n_batch, seq, lam_init, tq=512):
    nq = seq // tq
    cq, ck, cv, cmq = (COL_Q // LANES, COL_K // LANES, COL_V // LANES, COL_MQ // LANES)
    return pl.pallas_call(
        functools.partial(_attn_prompt_kernel, tq=tq, lam_init=lam_init),
        grid=(n_batch, DIFF_HEADS, nq),
        in_specs=[
            pl.BlockSpec((tq, LANES), lambda b, h, i: (b * nq + i, cq + h)),
            pl.BlockSpec((seq, LANES), lambda b, h, i: (b, ck + h)),
            pl.BlockSpec((seq, LANES), lambda b, h, i: (b, cv + h)),
            pl.BlockSpec((tq, LANES), lambda b, h, i: (b * nq + i, cmq + h)),
            pl.BlockSpec((MEM_TOKENS, LANES), lambda b, h, i: (b, h)),
            pl.BlockSpec((MEM_TOKENS, LANES), lambda b, h, i: (b, MEM_HEADS + h)),
            pl.BlockSpec((8, LANES), lambda b, h, i: (0, 0)),
            pl.BlockSpec((1, LANES), lambda b, h, i: (0, 0)),
        ],
        out_specs=[
            pl.BlockSpec((tq, LANES), lambda b, h, i: (b * nq + i, h)),
            pl.BlockSpec((tq, LANES), lambda b, h, i: (b * nq + i, h)),
        ],
        out_shape=[
            jax.ShapeDtypeStruct((n_batch * seq, D_DIFF), BF16),
            jax.ShapeDtypeStruct((n_batch * seq, D_MEMX), BF16),
        ],
        scratch_shapes=[
            pltpu.VMEM((seq, LANES), BF16),
            pltpu.VMEM((seq, LANES), BF16),
            pltpu.VMEM((2, tq, seq), F32),
            pltpu.VMEM((2, tq, LANES), F32),
            pltpu.VMEM((2, tq, LANES), F32),
            pltpu.VMEM((2, tq, LANES), F32),
        ],
        compiler_params=_cparams(("parallel", "parallel", "arbitrary")),
        name="attn_prompt",
    )(u, u, u, u, mkv, mkv, lam_pack, gsub)


PAGE = 128
PAGE_ROWS = PAGE * DIFF_HEADS


def _per_head_rows(x_row, split_maps):
    lane = lax.broadcasted_iota(jnp.int32, (1, LANES), 1)
    rows = []
    for r in range(2 * DIFF_HEADS):
        xh = x_row[:, (r // 2) * LANES:(r // 2 + 1) * LANES]
        if split_maps:
            xh = jnp.where((lane < DIFF_DK) if r % 2 == 0 else (lane >= DIFF_DK), xh, 0.0)
        rows.append(xh)
    return jnp.concatenate(rows, axis=0)


def _head_match(n_cols):
    r_i = lax.broadcasted_iota(jnp.int32, (8, n_cols), 0)
    c_i = lax.broadcasted_iota(jnp.int32, (8, n_cols), 1)
    return jnp.bitwise_and(c_i, DIFF_HEADS - 1) == jnp.right_shift(r_i, 1)


def _rider_specs(step_fn, spb):
    W = DIFF_HEADS * LANES
    assert spb & (spb - 1) == 0, "steps per sequence must be a power of two (shift/mask indexing)"
    shift = spb.bit_length() - 1

    def seq(idx):
        return jnp.right_shift(step_fn(*idx[:-1]), shift)

    tok_q = pl.BlockSpec((ROWS_S, W), lambda *idx: (seq(idx), COL_Q // W))
    hbm = pl.BlockSpec(memory_space=pl.ANY)
    part = pl.BlockSpec((8, LANES), lambda *idx: (seq(idx), 0))
    return [tok_q, hbm, hbm], [part, part, part]


def _rider_scratch(npp):
    buf = pltpu.VMEM((2, npp, PAGE_ROWS, LANES), F32)
    return [buf, buf, pltpu.SemaphoreType.DMA((2, 2))]


def _rider_pages(pt_ref, ck_hbm, cv_hbm, kbuf, vbuf, sem, *, spb, npp, page_lo, page_base):
    shift = spb.bit_length() - 1

    def copies(slot, rows):
        out = []
        for i in range(npp):
            out.append(pltpu.make_async_copy(ck_hbm.at[pl.ds(rows[i], PAGE_ROWS), :], kbuf.at[slot, i],
                                             sem.at[0, slot]))
            out.append(pltpu.make_async_copy(cv_hbm.at[pl.ds(rows[i], PAGE_ROWS), :], vbuf.at[slot, i],
                                             sem.at[1, slot]))
        return out

    def start(step, slot):
        seq = jnp.right_shift(step, shift)
        col = page_lo + jnp.bitwise_and(step, spb - 1) * npp
        rows = [pl.multiple_of((page_base + pt_ref[seq, col + i]) * PAGE_ROWS, PAGE_ROWS) for i in range(npp)]
        for c in copies(slot, rows):
            c.start()

    def wait(slot):
        for c in copies(slot, [0] * npp):
            c.wait()

    return start, wait


def _rider_parts(step, spb, tok_q, k_refs, v_refs, pm_ref, pl_ref, pa_ref, m_s, l_s, acc_s):
    npp = len(k_refs)
    j = jnp.bitwise_and(step, spb - 1)

    def first():
        m_s[...] = jnp.full(m_s.shape, -jnp.inf, F32)
        l_s[...] = jnp.zeros(l_s.shape, F32)
        acc_s[...] = jnp.zeros(acc_s.shape, F32)

    st = {"s": [], "pv": None}

    def scores(lo, hi):
        if not st["s"]:
            st["q"] = _per_head_rows(tok_q[0:1, :] * (DIFF_DK ** -0.5), True).astype(BF16)
        st["s"] += [_dot_nt(st["q"], k_refs[i][...].astype(BF16)) for i in range(lo, hi)]

    def softmax():
        s_all = jnp.concatenate(st["s"], axis=1)
        s_all = jnp.where(_head_match(npp * PAGE_ROWS), s_all, -jnp.inf)
        m_prev = m_s[...]
        m_new = jnp.maximum(m_prev, jnp.max(s_all, axis=-1, keepdims=True))
        st["alpha"] = jnp.exp(m_prev - m_new)
        p = jnp.exp(s_all - m_new[:, 0:1])
        l_s[...] = st["alpha"] * l_s[...] + jnp.sum(p, axis=-1, keepdims=True)
        m_s[...] = m_new
        st["p"] = p.astype(BF16)

    def weighted(lo, hi):
        for i in range(lo, hi):
            pv = _dot(st["p"][:, i * PAGE_ROWS:(i + 1) * PAGE_ROWS], v_refs[i][...].astype(BF16))
            st["pv"] = pv if st["pv"] is None else st["pv"] + pv
        if hi == npp:
            acc_s[...] = st["alpha"] * acc_s[...] + st["pv"]

    def last():
        pm_ref[...] = m_s[...]
        pl_ref[...] = l_s[...]
        pa_ref[...] = acc_s[...]

    return j, first, (scores, softmax, weighted), last


def _decode_finish_kernel(tok_q, tok_k, tok_v, tok_mq, mk_ref, mv_ref, lam_ref, gsub_ref, *rest, n_parts, lam_init):
    parts = rest[:3 * n_parts]
    od_ref, om_ref = rest[3 * n_parts:]
    qmat = _per_head_rows(tok_q[0:1, :] * (DIFF_DK ** -0.5), True)
    s_new = jnp.sum(qmat * _per_head_rows(tok_k[0:1, :], False), axis=-1, keepdims=True)
    m_tot = jnp.broadcast_to(s_new, (8, LANES))
    for h in range(n_parts):
        m_tot = jnp.maximum(m_tot, parts[3 * h][...])
    w_new = jnp.exp(s_new - m_tot)
    l_tot = w_new
    acc = w_new * _per_head_rows(tok_v[0:1, :], False)
    for h in range(n_parts):
        w = jnp.exp(parts[3 * h][...] - m_tot)
        l_tot = l_tot + w * parts[3 * h + 1][...]
        acc = acc + w * parts[3 * h + 2][...]
    lam = _lambda_from(lam_ref, lam_init)
    o_all = acc / l_tot
    outs = []
    for h in range(DIFF_HEADS):
        o = o_all[2 * h:2 * h + 1, :] - lam * o_all[2 * h + 1:2 * h + 2, :]
        r = lax.rsqrt(jnp.mean(o * o, axis=-1, keepdims=True) + EPS)
        outs.append((o * r * gsub_ref[...]) * (1.0 - lam_init))
    od_ref[...] = jnp.broadcast_to(jnp.concatenate(outs, axis=1), od_ref.shape)

    mqm = _per_head_rows(tok_mq[0:1, :], False)
    s = _dot_nt(mqm.astype(BF16), mk_ref[...].astype(BF16)) * (MEM_HD ** -0.5)
    s = jnp.where(_head_match(s.shape[1]), s, -jnp.inf)
    e = jnp.exp(s - jnp.max(s, axis=-1, keepdims=True))
    om = _dot(e.astype(BF16), mv_ref[...].astype(BF16)) / jnp.sum(e, axis=-1, keepdims=True)
    om_row = jnp.concatenate([om[2 * h:2 * h + 1, :] for h in range(MEM_HEADS)], axis=1)
    om_ref[...] = jnp.broadcast_to(om_row, om_ref.shape)


def _decode_finish(u_s, parts, mem_k, mem_v, lam_pack, gsub, layer, n_b, lam_init):
    W = DIFF_HEADS * LANES
    mem_rows = MEM_TOKENS * MEM_HEADS
    flat = [a for tri in parts for a in tri]

    def tok(col):
        return pl.BlockSpec((ROWS_S, W), lambda b: (b, col // W))

    od, om = pl.pallas_call(
        functools.partial(_decode_finish_kernel, n_parts=len(parts), lam_init=lam_init),
        grid=(n_b,),
        in_specs=[tok(COL_Q), tok(COL_K), tok(COL_V), tok(COL_MQ),
                  pl.BlockSpec((mem_rows, LANES), lambda b: (layer * n_b + b, 0)),
                  pl.BlockSpec((mem_rows, LANES), lambda b: (layer * n_b + b, 0)),
                  pl.BlockSpec((8, LANES), lambda b: (0, 0)),
                  pl.BlockSpec((1, LANES), lambda b: (0, 0))]
                 + [pl.BlockSpec((8, LANES), lambda b: (b, 0))] * len(flat),
        out_specs=[pl.BlockSpec((8, W), lambda b: (b, 0)), pl.BlockSpec((8, W), lambda b: (b, 0))],
        out_shape=[jax.ShapeDtypeStruct((n_b * 8, W), F32), jax.ShapeDtypeStruct((n_b * 8, W), F32)],
        compiler_params=_cparams(("parallel",)),
        name="decode_finish",
    )(u_s, u_s, u_s, u_s, mem_k, mem_v, lam_pack, gsub, *flat)
    return od.reshape(n_b, 8, W)[:, 0], om.reshape(n_b, 8, W)[:, 0]


def _outproj_kernel(x_ref, y_ref, od_ref, om_ref, w1_ref, w2_ref, w3_ref, o_ref):
    o_ref[...] = (x_ref[...]
                  + _dot(y_ref[...].astype(BF16), w1_ref[...])
                  + _dot(od_ref[...].astype(BF16), w2_ref[...])
                  + _dot(om_ref[...].astype(BF16), w3_ref[...]))


def _outproj(x, y, od, om, w_out, tm=512, tn=D_MODEL):
    m = x.shape[0]
    tm = min(tm, m)
    assert m % tm == 0
    return pl.pallas_call(
        _outproj_kernel,
        grid=(m // tm, D_MODEL // tn),
        in_specs=[
            pl.BlockSpec((tm, tn), lambda i, j: (i, j)),
            pl.BlockSpec((tm, D_SSM), lambda i, j: (i, 0)),
            pl.BlockSpec((tm, D_DIFF), lambda i, j: (i, 0)),
            pl.BlockSpec((tm, D_MEMX), lambda i, j: (i, 0)),
            pl.BlockSpec((D_SSM, tn), lambda i, j: (0, j)),
            pl.BlockSpec((D_DIFF, tn), lambda i, j: (D_SSM // D_DIFF, j)),
            pl.BlockSpec((D_MEMX, tn), lambda i, j: ((D_SSM + D_DIFF) // D_MEMX, j)),
        ],
        out_specs=pl.BlockSpec((tm, tn), lambda i, j: (i, j)),
        out_shape=jax.ShapeDtypeStruct((m, D_MODEL), F32),
        compiler_params=_cparams(("parallel", "arbitrary")),
        name="out_proj",
    )(x, y, od, om, w_out, w_out, w_out)


def _mlp_kernel(*refs, rider):
    if rider:
        npp, spb = rider["npp"], rider["spb"]
        pt_ref, x_ref, g_ref, wu_ref, wd_ref, tok_q, ck_hbm, cv_hbm = refs[:8]
        o_ref, pm_ref, pl_ref, pa_ref, h_ref, m_s, l_s, acc_s, kbuf, vbuf, sem = refs[8:]
    else:
        x_ref, g_ref, wu_ref, wd_ref, o_ref, h_ref = refs
    f = pl.program_id(1)

    @pl.when(f == 0)
    def _():
        x = x_ref[...]
        r = lax.rsqrt(jnp.mean(x * x, axis=-1, keepdims=True) + EPS)
        h_ref[...] = (x * r * g_ref[...]).astype(BF16)
        o_ref[...] = x

    if rider:
        n_steps = pl.num_programs(0) * pl.num_programs(1)
        step = pl.program_id(0) * pl.num_programs(1) + f
        slot = jnp.bitwise_and(step, 1)
        start, wait = _rider_pages(pt_ref, ck_hbm, cv_hbm, kbuf, vbuf, sem, spb=spb, npp=npp,
                                   page_lo=rider["page_lo"], page_base=rider["page_base"])
        k_refs = [kbuf.at[slot, i] for i in range(npp)]
        v_refs = [vbuf.at[slot, i] for i in range(npp)]
        j, first, main, last = _rider_parts(step, spb, tok_q, k_refs, v_refs, pm_ref, pl_ref, pa_ref,
                                            m_s, l_s, acc_s)
        pl.when(step == 0)(lambda: start(step, slot))
        pl.when(j == 0)(first)
        start(jnp.where(step + 1 == n_steps, 0, step + 1), 1 - slot)

    a = jnp.maximum(_dot(h_ref[...], wu_ref[...]), 0.0)
    o_ref[...] += _dot((a * a).astype(BF16), wd_ref[...])
    if rider:
        wait(slot)
        scores, softmax, weighted = main
        scores(0, npp)
        softmax()
        weighted(0, npp)
        pl.when(j == spb - 1)(last)
        pl.when(step == n_steps - 1)(lambda: wait(1 - slot))


def _mlp(x, g, w_up, w_down, tm=512, tf=1024, rider=None):
    m = x.shape[0]
    d_ff = w_up.shape[1]
    tm = min(tm, m)
    assert m % tm == 0 and d_ff % tf == 0
    grid = (m // tm, d_ff // tf)
    in_specs = [
        pl.BlockSpec((tm, D_MODEL), lambda i, f: (i, 0)),
        pl.BlockSpec((1, D_MODEL), lambda i, f: (0, 0)),
        pl.BlockSpec((D_MODEL, tf), lambda i, f: (0, f)),
        pl.BlockSpec((tf, D_MODEL), lambda i, f: (f, 0)),
    ]
    out_specs = [pl.BlockSpec((tm, D_MODEL), lambda i, f: (i, 0))]
    out_shape = [jax.ShapeDtypeStruct((m, D_MODEL), F32)]
    scratch = [pltpu.VMEM((tm, D_MODEL), BF16)]
    args = [x, g, w_up, w_down]
    if rider is None:
        out = pl.pallas_call(
            functools.partial(_mlp_kernel, rider=None),
            grid=grid, in_specs=in_specs, out_specs=out_specs, out_shape=out_shape, scratch_shapes=scratch,
            compiler_params=_cparams(("parallel", "arbitrary")), name="mlp",
        )(*args)
        return out[0]
    u_s, cache_k, cache_v, page_table, layer, n_pool, npp, page_lo, n_pages = rider
    n_b = page_table.shape[0]
    spb = n_pages // npp
    assert n_pages % npp == 0 and grid[0] * grid[1] == n_b * spb
    nf = grid[1]
    r_in, r_out = _rider_specs(lambda i, f: i * nf + f, spb)
    host_in = [pl.BlockSpec(s.block_shape, lambda i, f, pt, im=s.index_map: im(i, f)) for s in in_specs]
    host_out = [pl.BlockSpec(s.block_shape, lambda i, f, pt, im=s.index_map: im(i, f)) for s in out_specs]
    grid_spec = pltpu.PrefetchScalarGridSpec(
        num_scalar_prefetch=1, grid=grid,
        in_specs=host_in + r_in, out_specs=host_out + r_out,
        scratch_shapes=scratch + [pltpu.VMEM((8, LANES), F32)] * 3 + _rider_scratch(npp),
    )
    part_shape = jax.ShapeDtypeStruct((n_b * 8, LANES), F32)
    rider_cfg = dict(npp=npp, spb=spb, page_lo=page_lo, page_base=layer * n_pool)
    out, pm, pl_, pa = pl.pallas_call(
        functools.partial(_mlp_kernel, rider=rider_cfg),
        grid_spec=grid_spec,
        out_shape=out_shape + [part_shape] * 3,
        compiler_params=_cparams(("arbitrary", "arbitrary")),
        name="mlp_rider",
    )(page_table, *args, u_s, cache_k, cache_v)
    return out, (pm, pl_, pa)


def _pad_lanes(v, width=LANES):
    v = v.reshape(1, -1).astype(F32)
    return jnp.pad(v, ((0, 0), (0, width - v.shape[1])))


def kernel(x_prompt, x_sample, mem_prompt, cache_diff_k, cache_diff_v, cache_mem_k, cache_mem_v, state_conv, state_ssm, page_table, norm_mix, w_in, conv_w, conv_b, dt_bias, a_log, d_skip, g_ssm, g_q, g_k, lambda_q1, lambda_k1, lambda_q2, lambda_k2, g_subln, norm_mem, w_mem_kv, g_mq, g_mk, w_out, norm_mlp, w_up, w_down):
    depth = w_in.shape[0]
    bp, seq, _ = x_prompt.shape
    bd, dec_seq, _ = x_sample.shape
    assert dec_seq == 1 and seq % SSD_T == 0
    n_pool = cache_diff_k.shape[1]

    xp = x_prompt.reshape(bp * seq, D_MODEL)
    xs = x_sample.reshape(bd, D_MODEL)
    cache_k = cache_diff_k.reshape(depth * n_pool * PAGE_ROWS, LANES)
    cache_v = cache_diff_v.reshape(depth * n_pool * PAGE_ROWS, LANES)
    mem_k = cache_mem_k.reshape(depth * bd * MEM_TOKENS * MEM_HEADS, MEM_HD)
    mem_v = cache_mem_v.reshape(depth * bd * MEM_TOKENS * MEM_HEADS, MEM_HD)

    outs = [[] for _ in range(10)]
    for l in range(depth):
        lam_init = 0.8 - 0.6 * math.exp(-0.3 * l)
        wl = w_in[l]
        o_dt = D_SSM + CONV_DIM
        o_q = o_dt + SSM_HEADS
        w_main = [(wl, o_dt // PROJ_TN), (wl[:, o_q:].astype(BF16), (N_MAIN - o_dt) // PROJ_TN)]
        w_dt = jnp.pad(wl[:, o_dt:o_q], ((0, 0), (0, LANES - SSM_HEADS))).astype(BF16)
        ones = jnp.ones((PROJ_TN,), F32)
        gains = jnp.concatenate(
            [ones] * 5 + [jnp.tile(g_q[l], 2 * DIFF_HEADS), jnp.tile(g_k[l], 2 * DIFF_HEADS), ones,
                          jnp.tile(g_mq[l], MEM_HEADS)]).reshape(1, N_MAIN)
        modes = (0, 0, 0, 0, 0, DIFF_DK, DIFF_DK, 0, MEM_HD)
        g_mix = norm_mix[l].reshape(1, D_MODEL)

        ssm_p = (jnp.pad(conv_w[l], ((0, 8 - CONV_W), (0, 0))),
                 conv_b[l].reshape(1, CONV_DIM),
                 _pad_lanes(dt_bias[l]), _pad_lanes(a_log[l]),
                 jnp.repeat(d_skip[l], SSM_HEAD_DIM).reshape(1, D_SSM),
                 g_ssm[l].reshape(1, D_SSM))
        lam_pack = jnp.concatenate(
            [_pad_lanes(lambda_q1[l]), _pad_lanes(lambda_k1[l]), _pad_lanes(lambda_q2[l]),
             _pad_lanes(lambda_k2[l]), jnp.zeros((4, LANES), F32)], axis=0)
        gsub = g_subln[l].reshape(1, LANES)
        w_out_bf = w_out[l].astype(BF16)
        w_up_bf = w_up[l].astype(BF16)
        w_down_bf = w_down[l].astype(BF16)
        g_mlp = norm_mlp[l].reshape(1, D_MODEL)

        xs_pad = jnp.pad(xs.reshape(bd, 1, D_MODEL), ((0, 0), (0, ROWS_S - 1), (0, 0))).reshape(bd * ROWS_S, D_MODEL)
        u_s, dt_s = _norm_proj(xs_pad, g_mix, w_main, gains, modes, w_dt=w_dt)

        u_p, dt_p, k_rows, v_rows = _norm_proj(xp, g_mix, w_main, gains, modes, w_dt=w_dt,
                                               rows_blocks=(COL_K // PROJ_TN, COL_V // PROJ_TN))
        y_p, conv_p, h_p = _ssd(u_p, dt_p, bp, seq, seq, ssm_p)
        mem_gains = jnp.concatenate([jnp.tile(g_mk[l], MEM_HEADS), ones]).reshape(1, 2 * D_MEMX)
        mkv, = _norm_proj(mem_prompt.reshape(bp * MEM_TOKENS, D_MODEL), norm_mem[l].reshape(1, D_MODEL),
                          [(w_mem_kv[l], 2 * D_MEMX // PROJ_TN)], mem_gains, (MEM_HD, 0))
        od_p, om_p = _attn_prompt(u_p, mkv, lam_pack, gsub, bp, seq, lam_init)
        xp_mid = _outproj(xp, y_p, od_p, om_p, w_out_bf)
        n_pages = page_table.shape[1]
        xp_new, part = _mlp(xp_mid, g_mlp, w_up_bf, w_down_bf, tf=512,
                            rider=(u_s, cache_k, cache_v, page_table, l, n_pool, 16, 0, n_pages))

        outs[0].append(k_rows.reshape(bp, seq, DIFF_HEADS, LANES))
        outs[1].append(v_rows.reshape(bp, seq, DIFF_HEADS, LANES))
        outs[4].append(mkv[:, :D_MEMX].reshape(bp, MEM_TOKENS, MEM_HEADS, MEM_HD))
        outs[5].append(mkv[:, D_MEMX:].reshape(bp, MEM_TOKENS, MEM_HEADS, MEM_HD))
        outs[6].append(conv_p[:, :CONV_W - 1])
        outs[7].append(h_p.reshape(bp, SSM_HEADS, SSM_HEAD_DIM, D_STATE))

        conv_prev = jnp.pad(state_conv[l], ((0, 0), (8 - (CONV_W - 1), 0), (0, 0)))
        y_s, conv_s, h_s = _ssd(u_s, dt_s, bd, ROWS_S, 1, ssm_p, conv_prev=conv_prev,
                                h0=state_ssm[l].reshape(bd, D_SSM, D_STATE))
        od_s, om_s = _decode_finish(u_s, [part], mem_k, mem_v, lam_pack, gsub, l, bd, lam_init)
        y_s0 = y_s.reshape(bd, ROWS_S, D_SSM)[:, 0]
        xs_mid = _outproj(xs, y_s0, od_s, om_s, w_out_bf)
        xs_new = _mlp(xs_mid, g_mlp, w_up_bf, w_down_bf)
        u_s0 = u_s.reshape(bd, ROWS_S, N_MAIN)[:, 0]
        outs[2].append(u_s0[:, COL_K:COL_K + D_DIFF].reshape(bd, 1, DIFF_HEADS, LANES))
        outs[3].append(u_s0[:, COL_V:COL_V + D_DIFF].reshape(bd, 1, DIFF_HEADS, LANES))
        outs[8].append(conv_s[:, :CONV_W - 1])
        outs[9].append(h_s.reshape(bd, SSM_HEADS, SSM_HEAD_DIM, D_STATE))

        xp, xs = xp_new, xs_new

    st = [jnp.stack(o) for o in outs]
    return (xp.reshape(bp, seq, D_MODEL), xs.reshape(bd, 1, D_MODEL),
            st[0], st[1], st[2], st[3], st[4], st[5], st[6], st[7], st[8], st[9])
```

```python
import functools
import math

import jax
import jax.numpy as jnp
from jax import lax
from jax.experimental import pallas as pl
from jax.experimental.pallas import tpu as pltpu

F32 = jnp.float32
BF16 = jnp.bfloat16

D_MODEL = 2048
D_SSM = 1024
D_DIFF = 512
D_MEMX = 512
SSM_HEADS = 16
SSM_HEAD_DIM = 64
SSM_GROUPS = 2
D_STATE = 128
CONV_W = 4
BC_DIM = 2 * SSM_GROUPS * D_STATE
CONV_DIM = D_SSM + BC_DIM
DIFF_HEADS = 4
DIFF_DK = 64
MEM_HEADS = 4
MEM_HD = 128
MEM_TOKENS = 256
D_FF = 4 * D_MODEL
EPS = 1e-6
LOG2E = math.log2(math.e)

LANES = 128
SUBLANES_BF16 = 16
VMEM_LIMIT_BYTES = 56 * 1024 * 1024

N_MAIN = D_SSM + D_SSM + BC_DIM + 3 * D_DIFF + D_MEMX
COL_Z, COL_XS, COL_BC, COL_Q, COL_K, COL_V, COL_MQ = 0, 1024, 2048, 2560, 3072, 3584, 4096
PROJ_TN = 512
SSD_T = 128
ROWS_S = SUBLANES_BF16


def _cparams(sem):
    return pltpu.CompilerParams(dimension_semantics=sem, vmem_limit_bytes=VMEM_LIMIT_BYTES)


def _dot(a, b):
    return jnp.dot(a, b, preferred_element_type=F32)


def _dot_nt(a, b):
    return lax.dot_general(a, b, (((1,), (1,)), ((), ())), preferred_element_type=F32)


def _silu(x):
    return x * (1.0 / (1.0 + jnp.exp(-x)))


def _group_rmsnorm_slab(a, gain, group):
    sq = a * a
    if group == LANES:
        r = lax.rsqrt(jnp.sum(sq, axis=-1, keepdims=True) * (1.0 / LANES) + EPS)
    else:
        lane = lax.broadcasted_iota(jnp.int32, a.shape, 1)
        lo = lane < group
        s_lo = jnp.sum(jnp.where(lo, sq, 0.0), axis=-1, keepdims=True)
        s_hi = jnp.sum(jnp.where(lo, 0.0, sq), axis=-1, keepdims=True)
        r = jnp.where(lo, lax.rsqrt(s_lo * (1.0 / group) + EPS),
                      lax.rsqrt(s_hi * (1.0 / group) + EPS))
    return a * r * gain


def _proj_kernel(*refs, modes, seg_of, w_trans, has_dt, rows_blocks):
    n_w = len(w_trans)
    x_ref, g_ref = refs[:2]
    w_refs = refs[2:2 + n_w]
    gain_ref = refs[2 + n_w]
    pos = 3 + n_w
    if has_dt:
        wdt_ref = refs[pos]
        pos += 1
    u_ref = refs[pos]
    pos += 1
    if has_dt:
        dt_ref = refs[pos]
        pos += 1
    rows_refs = refs[pos:pos + len(rows_blocks)]
    h_ref = refs[pos + len(rows_blocks)]
    j = pl.program_id(1)
    tm = x_ref.shape[0]

    @pl.when(j == 0)
    def _():
        x = x_ref[...]
        r = lax.rsqrt(jnp.mean(x * x, axis=-1, keepdims=True) + EPS)
        h_ref[...] = (x * r * g_ref[...]).astype(BF16)
        if has_dt:
            dt_ref[...] = _dot_nt(h_ref[...], wdt_ref[...])

    keys = {}
    for jj, m in enumerate(modes):
        ri = rows_blocks.index(jj) if jj in rows_blocks else None
        keys.setdefault((seg_of[jj], m, ri), []).append(jj)
    for (seg, mode, ri), jjs in keys.items():
        cond = j == jjs[0]
        for jj in jjs[1:]:
            cond = jnp.logical_or(cond, j == jj)

        @pl.when(cond)
        def _(seg=seg, mode=mode, ri=ri):
            w = w_refs[seg][...].astype(BF16)
            acc = _dot_nt(h_ref[...], w) if w_trans[seg] else _dot(h_ref[...], w)
            if mode == 0 and ri is None:
                u_ref[...] = acc
                return
            gain = gain_ref[...]
            for s in range(acc.shape[1] // LANES):
                sl = slice(s * LANES, (s + 1) * LANES)
                slab = acc[:, sl]
                if mode:
                    slab = _group_rmsnorm_slab(slab, gain[:, sl], mode)
                u_ref[:, sl] = slab
                if ri is not None:
                    rows_refs[ri][pl.ds(s, tm, stride=PROJ_TN // LANES), :] = slab


def _norm_proj(x, g, ws, gains, modes, w_dt=None, rows_blocks=(), tm=1024):
    m, k = x.shape
    tm = min(tm, m)
    nblk = [nb for _, nb, _ in ws]
    w_trans = tuple(t for _, _, t in ws)
    ws = [w for w, _, _ in ws]
    assert m % tm == 0 and sum(nblk) == len(modes)
    assert all(w.shape[0 if t else 1] >= nb * PROJ_TN for w, nb, t in zip(ws, nblk, w_trans))
    seg_of, offs = [], []
    for a, nb in enumerate(nblk):
        offs.append(len(seg_of))
        seg_of += [a] * nb
    n = PROJ_TN * len(modes)
    has_dt = w_dt is not None

    def w_spec(a):
        def blk(j):
            return jnp.minimum(jnp.maximum(j - offs[a], 0), nblk[a] - 1)
        if w_trans[a]:
            return pl.BlockSpec((PROJ_TN, k), lambda i, j: (blk(j), 0))
        return pl.BlockSpec((k, PROJ_TN), lambda i, j: (0, blk(j)))

    in_specs = [pl.BlockSpec((tm, k), lambda i, j: (i, 0)), pl.BlockSpec((1, k), lambda i, j: (0, 0))]
    in_specs += [w_spec(a) for a in range(len(ws))]
    in_specs.append(pl.BlockSpec((1, PROJ_TN), lambda i, j: (0, j)))
    args = [x, g, *ws, gains]
    out_shape = [jax.ShapeDtypeStruct((m, n), F32)]
    out_specs = [pl.BlockSpec((tm, PROJ_TN), lambda i, j: (i, j))]
    if has_dt:
        in_specs.append(pl.BlockSpec((LANES, k), lambda i, j: (0, 0)))
        args.append(w_dt)
        out_shape.append(jax.ShapeDtypeStruct((m, LANES), F32))
        out_specs.append(pl.BlockSpec((tm, LANES), lambda i, j: (i, 0)))
    heads = PROJ_TN // LANES
    for _ in rows_blocks:
        out_shape.append(jax.ShapeDtypeStruct((m * heads, LANES), F32))
        out_specs.append(pl.BlockSpec((tm * heads, LANES), lambda i, j: (i, 0)))
    return pl.pallas_call(
        functools.partial(_proj_kernel, modes=tuple(modes), seg_of=tuple(seg_of), w_trans=w_trans, has_dt=has_dt,
                          rows_blocks=tuple(rows_blocks)),
        grid=(m // tm, len(modes)),
        in_specs=in_specs,
        out_specs=out_specs,
        out_shape=out_shape,
        scratch_shapes=[pltpu.VMEM((tm, k), BF16)],
        compiler_params=_cparams(("parallel", "arbitrary")),
        name="norm_proj",
    )(*args)


def _split3(x):
    hi = x.astype(BF16)
    r1 = x - hi.astype(F32)
    mid = r1.astype(BF16)
    lo = (r1 - mid.astype(F32)).astype(BF16)
    return hi, mid, lo


def _ssd_kernel(*refs, t_in, valid_last, nc, has_init):
    if has_init:
        (xs_ref, z_ref, bc_ref, dt_ref, cprev_ref, h0_ref, convw_ref, convb_ref, dtb_ref, alog_ref,
         dskip_ref, gssm_ref, y_ref, cout_ref, hout_ref, xpad, hst) = refs
    else:
        (xs_ref, z_ref, bc_ref, dt_ref, convw_ref, convb_ref, dtb_ref, alog_ref,
         dskip_ref, gssm_ref, y_ref, cout_ref, hout_ref, xpad, hst) = refs
    T = SSD_T
    c = pl.program_id(1)

    @pl.when(c == 0)
    def _():
        if has_init:
            xpad[0:8, :] = cprev_ref[0]
            hst[...] = h0_ref[0]
        else:
            xpad[0:8, :] = jnp.zeros((8, CONV_DIM), F32)
            hst[...] = jnp.zeros(hst.shape, F32)

    def rows(ref):
        v = ref[...]
        if t_in < T:
            v = jnp.concatenate([v, jnp.zeros((T - t_in, v.shape[1]), F32)], axis=0)
        return v

    xpad[8:8 + T, 0:D_SSM] = rows(xs_ref)
    xpad[8:8 + T, D_SSM:CONV_DIM] = rows(bc_ref)
    z = rows(z_ref)
    dt_raw = rows(dt_ref)

    convw = convw_ref[...]
    conv = convb_ref[...] + xpad[5:5 + T, :] * convw[0:1, :]
    for jtap in range(1, CONV_W):
        conv = conv + xpad[5 + jtap:5 + jtap + T, :] * convw[jtap:jtap + 1, :]
    xact = _silu(conv)

    row_i = lax.broadcasted_iota(jnp.int32, (T, LANES), 0)
    col_i = lax.broadcasted_iota(jnp.int32, (T, LANES), 1)
    xv = dt_raw + dtb_ref[...]
    dt = jnp.maximum(xv, 0.0) + jnp.log1p(jnp.exp(-jnp.abs(xv)))
    if valid_last < T:
        dt = jnp.where(row_i < valid_last, dt, 0.0)
    a_neg = -jnp.exp(alog_ref[...])
    adt = dt * a_neg

    tril = (row_i >= col_i)
    tril_bf = jnp.where(tril, 1.0, 0.0).astype(BF16)
    a_hi, a_mid, a_lo = _split3(adt)
    acum = _dot(tril_bf, a_hi) + _dot(tril_bf, a_mid) + _dot(tril_bf, a_lo)
    acum_t = acum.T
    dt_t = dt.T
    e_acum = jnp.exp(acum)
    a_last = acum[T - 1:T, :]
    w_state = jnp.exp(a_last - acum) * dt
    da_last = jnp.exp(a_last)

    lo_half = col_i < SSM_HEAD_DIM
    neg_big = jnp.float32(-1e30)

    def colb(tile, r):
        return jnp.broadcast_to(tile[:, r:r + 1], (T, LANES))

    y_slabs = []
    for g in range(SSM_GROUPS):
        b_g = xact[:, D_SSM + g * D_STATE:D_SSM + (g + 1) * D_STATE]
        c_g = xact[:, D_SSM + SSM_GROUPS * D_STATE + g * D_STATE:
                   D_SSM + SSM_GROUPS * D_STATE + (g + 1) * D_STATE]
        b_bf = b_g.astype(BF16)
        c_bf = c_g.astype(BF16)
        cb = _dot_nt(c_bf, b_bf)
        for pp in range(SSM_HEADS // SSM_GROUPS // 2):
            p = g * (SSM_HEADS // SSM_GROUPS // 2) + pp
            sl = slice(p * LANES, (p + 1) * LANES)
            xs_slab = xact[:, sl]
            y_acc = dskip_ref[:, sl] * xs_slab
            for hh in range(2):
                r = 2 * p + hh
                seg = colb(acum, r) - acum_t[r:r + 1, :]
                lmat = jnp.exp(jnp.where(tril, seg, neg_big))
                mr = (cb * lmat * dt_t[r:r + 1, :]).astype(BF16)
                xh = jnp.where(lo_half if hh == 0 else jnp.logical_not(lo_half), xs_slab, 0.0)
                y_acc = y_acc + _dot(mr, xh.astype(BF16))
            hpair = hst[sl, :]
            e_pair = jnp.where(lo_half, colb(e_acum, 2 * p), colb(e_acum, 2 * p + 1))
            y_acc = y_acc + e_pair * _dot_nt(c_bf, hpair.astype(BF16))
            w_pair = jnp.where(lo_half, colb(w_state, 2 * p), colb(w_state, 2 * p + 1))
            xw_t = (xs_slab * w_pair).T
            st = _dot(xw_t.astype(BF16), b_bf)
            da = jnp.concatenate(
                [jnp.broadcast_to(da_last[:, 2 * p:2 * p + 1], (SSM_HEAD_DIM, LANES)),
                 jnp.broadcast_to(da_last[:, 2 * p + 1:2 * p + 2], (SSM_HEAD_DIM, LANES))], axis=0)
            hst[sl, :] = da * hpair + st
            y_slabs.append(y_acc * _silu(z[:, sl]))

    per_group = D_SSM // SSM_GROUPS // LANES
    for g in range(SSM_GROUPS):
        slabs = y_slabs[g * per_group:(g + 1) * per_group]
        ssum = jnp.sum(slabs[0] * slabs[0], axis=-1, keepdims=True)
        for s in slabs[1:]:
            ssum = ssum + jnp.sum(s * s, axis=-1, keepdims=True)
        r = lax.rsqrt(ssum * (1.0 / (per_group * LANES)) + EPS)
        for k, s in enumerate(slabs):
            sl = slice((g * per_group + k) * LANES, (g * per_group + k + 1) * LANES)
            y_ref[:, sl] = (s * r * gssm_ref[:, sl])[0:t_in, :].astype(y_ref.dtype)

    if nc > 1:
        xpad[5:8, :] = xpad[5 + T:8 + T, :]

    @pl.when(c == nc - 1)
    def _():
        cout_ref[...] = jnp.zeros(cout_ref.shape, F32)
        if nc > 1:
            cout_ref[0, 0:CONV_W - 1, :] = xpad[5:8, :]
        else:
            cout_ref[0, 0:CONV_W - 1, :] = xpad[5 + valid_last:8 + valid_last, :]
        hout_ref[0] = hst[...]


def _ssd(u, dt, n_batch, seq_rows, valid_len, ssm_p, conv_prev=None, h0=None):
    t_in = min(seq_rows, SSD_T)
    nc = max(seq_rows // SSD_T, 1)
    valid_last = valid_len - (nc - 1) * SSD_T
    has_init = conv_prev is not None
    convw, convb, dtb, alog, dskip, gssm = ssm_p
    ntb = N_MAIN // PROJ_TN

    def row(b, c):
        return b * nc + c

    in_specs = [
        pl.BlockSpec((t_in, D_SSM), lambda b, c: (row(b, c), COL_XS // D_SSM)),
        pl.BlockSpec((t_in, D_SSM), lambda b, c: (row(b, c), COL_Z // D_SSM)),
        pl.BlockSpec((t_in, BC_DIM), lambda b, c: (row(b, c), COL_BC // BC_DIM)),
        pl.BlockSpec((t_in, LANES), lambda b, c: (row(b, c), 0)),
    ]
    args = [u, u, u, dt]
    if has_init:
        in_specs += [pl.BlockSpec((1, 8, CONV_DIM), lambda b, c: (b, 0, 0)),
                     pl.BlockSpec((1, D_SSM, D_STATE), lambda b, c: (b, 0, 0))]
        args += [conv_prev, h0]
    in_specs += [
        pl.BlockSpec((8, CONV_DIM), lambda b, c: (0, 0)),
        pl.BlockSpec((1, CONV_DIM), lambda b, c: (0, 0)),
        pl.BlockSpec((1, LANES), lambda b, c: (0, 0)),
        pl.BlockSpec((1, LANES), lambda b, c: (0, 0)),
        pl.BlockSpec((1, D_SSM), lambda b, c: (0, 0)),
        pl.BlockSpec((1, D_SSM), lambda b, c: (0, 0)),
    ]
    args += [convw, convb, dtb, alog, dskip, gssm]
    del ntb
    return pl.pallas_call(
        functools.partial(_ssd_kernel, t_in=t_in, valid_last=valid_last, nc=nc, has_init=has_init),
        grid=(n_batch, nc),
        in_specs=in_specs,
        out_specs=[
            pl.BlockSpec((t_in, D_SSM), lambda b, c: (row(b, c), 0)),
            pl.BlockSpec((1, 8, CONV_DIM), lambda b, c: (b, 0, 0)),
            pl.BlockSpec((1, D_SSM, D_STATE), lambda b, c: (b, 0, 0)),
        ],
        out_shape=[
            jax.ShapeDtypeStruct((n_batch * seq_rows, D_SSM), BF16),
            jax.ShapeDtypeStruct((n_batch, 8, CONV_DIM), F32),
            jax.ShapeDtypeStruct((n_batch, D_SSM, D_STATE), F32),
        ],
        scratch_shapes=[pltpu.VMEM((8 + SSD_T, CONV_DIM), F32), pltpu.VMEM((D_SSM, D_STATE), F32)],
        compiler_params=_cparams(("parallel", "arbitrary")),
        name="ssd_scan",
    )(*args)


def _lambda_from(lam_ref, lam_init):
    lp = lam_ref[...]
    s1 = jnp.sum(lp[0:1, :] * lp[1:2, :], axis=-1, keepdims=True)
    s2 = jnp.sum(lp[2:3, :] * lp[3:4, :], axis=-1, keepdims=True)
    return jnp.exp(s1) - jnp.exp(s2) + lam_init


def _attn_prompt_kernel(q_ref, k_ref, v_ref, mq_ref, mk_ref, mv_ref, lam_ref, gsub_ref,
                        od_ref, om_ref, kb, vb, s_s, m_s, l_s, acc_s, *, tq, lam_init):
    qi = pl.program_id(2)

    @pl.when(qi == 0)
    def _():
        kb[...] = k_ref[...].astype(BF16)
        vb[...] = v_ref[...].astype(BF16)

    lane = lax.broadcasted_iota(jnp.int32, (tq, LANES), 1)
    lo = lane < DIFF_DK
    q = q_ref[...] * (DIFF_DK ** -0.5 * LOG2E)
    qm = (jnp.where(lo, q, 0.0).astype(BF16), jnp.where(lo, 0.0, q).astype(BF16))
    m_s[...] = jnp.full(m_s.shape, -jnp.inf, F32)
    l_s[...] = jnp.zeros(l_s.shape, F32)
    acc_s[...] = jnp.zeros(acc_s.shape, F32)
    reps = tq // LANES

    def scores(j, masked):
        start = pl.multiple_of(j * tq, tq)
        k_blk = kb[pl.ds(start, tq), :]
        for mi in range(2):
            s = _dot_nt(qm[mi], k_blk)
            if masked:
                r_i = lax.broadcasted_iota(jnp.int32, (tq, tq), 0)
                c_i = lax.broadcasted_iota(jnp.int32, (tq, tq), 1)
                s = jnp.where(r_i >= c_i, s, -jnp.inf)
            s_s[mi, :, pl.ds(start, tq)] = s
            m_s[mi] = jnp.maximum(m_s[mi], jnp.max(s, axis=-1, keepdims=True))

    def weighted(j):
        start = pl.multiple_of(j * tq, tq)
        v_blk = vb[pl.ds(start, tq), :]
        for mi in range(2):
            m_rep = jnp.concatenate([m_s[mi]] * reps, axis=1)
            p = jnp.exp2(s_s[mi, :, pl.ds(start, tq)] - m_rep)
            l_s[mi] += jnp.sum(p, axis=-1, keepdims=True)
            acc_s[mi] += _dot(p.astype(BF16), v_blk)

    def body1(j, carry):
        scores(j, False)
        return carry

    def body2(j, carry):
        weighted(j)
        return carry

    lax.fori_loop(0, qi, body1, 0)
    scores(qi, True)
    lax.fori_loop(0, qi + 1, body2, 0)

    lam = _lambda_from(lam_ref, lam_init)
    o = acc_s[0] / l_s[0] - lam * (acc_s[1] / l_s[1])
    r = lax.rsqrt(jnp.mean(o * o, axis=-1, keepdims=True) + EPS)
    od_ref[...] = ((o * r * gsub_ref[...]) * (1.0 - lam_init)).astype(od_ref.dtype)

    s = _dot_nt(mq_ref[...].astype(BF16), mk_ref[...].astype(BF16)) * (MEM_HD ** -0.5 * LOG2E)
    e = jnp.exp2(s - jnp.max(s, axis=-1, keepdims=True))
    om = _dot(e.astype(BF16), mv_ref[...].astype(BF16)) / jnp.sum(e, axis=-1, keepdims=True)
    om_ref[...] = om.astype(om_ref.dtype)


def _attn_prompt(u, mkv, lam_pack, gsub, n_batch, seq, lam_init, tq=512):
    nq = seq // tq
    cq, ck, cv, cmq = (COL_Q // LANES, COL_K // LANES, COL_V // LANES, COL_MQ // LANES)
    return pl.pallas_call(
        functools.partial(_attn_prompt_kernel, tq=tq, lam_init=lam_init),
        grid=(n_batch, DIFF_HEADS, nq),
        in_specs=[
            pl.BlockSpec((tq, LANES), lambda b, h, i: (b * nq + i, cq + h)),
            pl.BlockSpec((seq, LANES), lambda b, h, i: (b, ck + h)),
            pl.BlockSpec((seq, LANES), lambda b, h, i: (b, cv + h)),
            pl.BlockSpec((tq, LANES), lambda b, h, i: (b * nq + i, cmq + h)),
            pl.BlockSpec((MEM_TOKENS, LANES), lambda b, h, i: (b, h)),
            pl.BlockSpec((MEM_TOKENS, LANES), lambda b, h, i: (b, MEM_HEADS + h)),
            pl.BlockSpec((8, LANES), lambda b, h, i: (0, 0)),
            pl.BlockSpec((1, LANES), lambda b, h, i: (0, 0)),
        ],
        out_specs=[
            pl.BlockSpec((tq, LANES), lambda b, h, i: (b * nq + i, h)),
            pl.BlockSpec((tq, LANES), lambda b, h, i: (b * nq + i, h)),
        ],
        out_shape=[
            jax.ShapeDtypeStruct((n_batch * seq, D_DIFF), BF16),
            jax.ShapeDtypeStruct((n_batch * seq, D_MEMX), BF16),
        ],
        scratch_shapes=[
            pltpu.VMEM((seq, LANES), BF16),
            pltpu.VMEM((seq, LANES), BF16),
            pltpu.VMEM((2, tq, seq), F32),
            pltpu.VMEM((2, tq, LANES), F32),
            pltpu.VMEM((2, tq, LANES), F32),
            pltpu.VMEM((2, tq, LANES), F32),
        ],
        compiler_params=_cparams(("parallel", "parallel", "arbitrary")),
        name="attn_prompt",
    )(u, u, u, u, mkv, mkv, lam_pack, gsub)


PAGE = 128
PAGE_ROWS = PAGE * DIFF_HEADS


def _per_head_rows(x_row, split_maps):
    lane = lax.broadcasted_iota(jnp.int32, (1, LANES), 1)
    rows = []
    for r in range(2 * DIFF_HEADS):
        xh = x_row[:, (r // 2) * LANES:(r // 2 + 1) * LANES]
        if split_maps:
            xh = jnp.where((lane < DIFF_DK) if r % 2 == 0 else (lane >= DIFF_DK), xh, 0.0)
        rows.append(xh)
    return jnp.concatenate(rows, axis=0)


def _head_match(n_cols):
    r_i = lax.broadcasted_iota(jnp.int32, (8, n_cols), 0)
    c_i = lax.broadcasted_iota(jnp.int32, (8, n_cols), 1)
    return jnp.bitwise_and(c_i, DIFF_HEADS - 1) == jnp.right_shift(r_i, 1)


def _rider_specs(step_fn, spb):
    W = DIFF_HEADS * LANES
    assert spb & (spb - 1) == 0, "steps per sequence must be a power of two (shift/mask indexing)"
    shift = spb.bit_length() - 1

    def seq(idx):
        return jnp.right_shift(step_fn(*idx[:-1]), shift)

    tok_q = pl.BlockSpec((ROWS_S, W), lambda *idx: (seq(idx), COL_Q // W))
    hbm = pl.BlockSpec(memory_space=pl.ANY)
    part = pl.BlockSpec((8, LANES), lambda *idx: (seq(idx), 0))
    return [tok_q, hbm, hbm], [part, part, part]


def _rider_scratch(npp):
    buf = pltpu.VMEM((2, npp, PAGE_ROWS, LANES), F32)
    return [buf, buf, pltpu.SemaphoreType.DMA((2, 2))]


def _rider_pages(pt_ref, ck_hbm, cv_hbm, kbuf, vbuf, sem, *, spb, npp, page_lo, page_base):
    shift = spb.bit_length() - 1

    def copies(slot, rows):
        out = []
        for i in range(npp):
            out.append(pltpu.make_async_copy(ck_hbm.at[pl.ds(rows[i], PAGE_ROWS), :], kbuf.at[slot, i],
                                             sem.at[0, slot]))
            out.append(pltpu.make_async_copy(cv_hbm.at[pl.ds(rows[i], PAGE_ROWS), :], vbuf.at[slot, i],
                                             sem.at[1, slot]))
        return out

    def start(step, slot):
        seq = jnp.right_shift(step, shift)
        col = page_lo + jnp.bitwise_and(step, spb - 1) * npp
        rows = [pl.multiple_of((page_base + pt_ref[seq, col + i]) * PAGE_ROWS, PAGE_ROWS) for i in range(npp)]
        for c in copies(slot, rows):
            c.start()

    def wait(slot):
        for c in copies(slot, [0] * npp):
            c.wait()

    return start, wait


def _rider_parts(step, spb, tok_q, k_refs, v_refs, pm_ref, pl_ref, pa_ref, m_s, l_s, acc_s):
    npp = len(k_refs)
    j = jnp.bitwise_and(step, spb - 1)

    def first():
        m_s[...] = jnp.full(m_s.shape, -jnp.inf, F32)
        l_s[...] = jnp.zeros(l_s.shape, F32)
        acc_s[...] = jnp.zeros(acc_s.shape, F32)

    st = {"s": [], "pv": None}

    def scores(lo, hi):
        if not st["s"]:
            st["q"] = _per_head_rows(tok_q[0:1, :] * (DIFF_DK ** -0.5), True).astype(BF16)
        st["s"] += [_dot_nt(st["q"], k_refs[i][...].astype(BF16)) for i in range(lo, hi)]

    def softmax():
        s_all = jnp.concatenate(st["s"], axis=1)
        s_all = jnp.where(_head_match(npp * PAGE_ROWS), s_all, -jnp.inf)
        m_prev = m_s[...]
        m_new = jnp.maximum(m_prev, jnp.max(s_all, axis=-1, keepdims=True))
        st["alpha"] = jnp.exp(m_prev - m_new)
        p = jnp.exp(s_all - m_new[:, 0:1])
        l_s[...] = st["alpha"] * l_s[...] + jnp.sum(p, axis=-1, keepdims=True)
        m_s[...] = m_new
        st["p"] = p.astype(BF16)

    def weighted(lo, hi):
        for i in range(lo, hi):
            pv = _dot(st["p"][:, i * PAGE_ROWS:(i + 1) * PAGE_ROWS], v_refs[i][...].astype(BF16))
            st["pv"] = pv if st["pv"] is None else st["pv"] + pv
        if hi == npp:
            acc_s[...] = st["alpha"] * acc_s[...] + st["pv"]

    def last():
        pm_ref[...] = m_s[...]
        pl_ref[...] = l_s[...]
        pa_ref[...] = acc_s[...]

    return j, first, (scores, softmax, weighted), last


def _decode_finish_kernel(tok_q, tok_k, tok_v, tok_mq, mk_ref, mv_ref, lam_ref, gsub_ref, *rest, n_parts, grp,
                          lam_init):
    parts = rest[:3 * n_parts]
    od_ref, om_ref = rest[3 * n_parts:]
    mem_rows = MEM_TOKENS * MEM_HEADS
    lam = _lambda_from(lam_ref, lam_init)
    for g in range(grp):
        t0 = g * ROWS_S
        r8 = slice(g * 8, (g + 1) * 8)
        qmat = _per_head_rows(tok_q[t0:t0 + 1, :] * (DIFF_DK ** -0.5), True)
        s_new = jnp.sum(qmat * _per_head_rows(tok_k[t0:t0 + 1, :], False), axis=-1, keepdims=True)
        m_tot = jnp.broadcast_to(s_new, (8, LANES))
        for h in range(n_parts):
            m_tot = jnp.maximum(m_tot, parts[3 * h][r8, :])
        w_new = jnp.exp(s_new - m_tot)
        l_tot = w_new
        acc = w_new * _per_head_rows(tok_v[t0:t0 + 1, :], False)
        for h in range(n_parts):
            w = jnp.exp(parts[3 * h][r8, :] - m_tot)
            l_tot = l_tot + w * parts[3 * h + 1][r8, :]
            acc = acc + w * parts[3 * h + 2][r8, :]
        o_all = acc / l_tot
        outs = []
        for h in range(DIFF_HEADS):
            o = o_all[2 * h:2 * h + 1, :] - lam * o_all[2 * h + 1:2 * h + 2, :]
            r = lax.rsqrt(jnp.mean(o * o, axis=-1, keepdims=True) + EPS)
            outs.append((o * r * gsub_ref[...]) * (1.0 - lam_init))
        od_ref[r8, :] = jnp.broadcast_to(jnp.concatenate(outs, axis=1), (8, DIFF_HEADS * LANES))

        mrows = slice(g * mem_rows, (g + 1) * mem_rows)
        mqm = _per_head_rows(tok_mq[t0:t0 + 1, :], False)
        s = _dot_nt(mqm.astype(BF16), mk_ref[mrows, :].astype(BF16)) * (MEM_HD ** -0.5)
        s = jnp.where(_head_match(s.shape[1]), s, -jnp.inf)
        e = jnp.exp(s - jnp.max(s, axis=-1, keepdims=True))
        om = _dot(e.astype(BF16), mv_ref[mrows, :].astype(BF16)) / jnp.sum(e, axis=-1, keepdims=True)
        om_row = jnp.concatenate([om[2 * h:2 * h + 1, :] for h in range(MEM_HEADS)], axis=1)
        om_ref[r8, :] = jnp.broadcast_to(om_row, (8, MEM_HEADS * LANES))


def _decode_finish(u_s, parts, mem_k, mem_v, lam_pack, gsub, layer, n_b, lam_init, grp=4):
    W = DIFF_HEADS * LANES
    mem_rows = MEM_TOKENS * MEM_HEADS
    flat = [a for tri in parts for a in tri]
    assert n_b % grp == 0
    nblk = n_b // grp

    def tok(col):
        return pl.BlockSpec((grp * ROWS_S, W), lambda b: (b, col // W))

    od, om = pl.pallas_call(
        functools.partial(_decode_finish_kernel, n_parts=len(parts), grp=grp, lam_init=lam_init),
        grid=(nblk,),
        in_specs=[tok(COL_Q), tok(COL_K), tok(COL_V), tok(COL_MQ),
                  pl.BlockSpec((grp * mem_rows, LANES), lambda b: (layer * nblk + b, 0)),
                  pl.BlockSpec((grp * mem_rows, LANES), lambda b: (layer * nblk + b, 0)),
                  pl.BlockSpec((8, LANES), lambda b: (0, 0)),
                  pl.BlockSpec((1, LANES), lambda b: (0, 0))]
                 + [pl.BlockSpec((grp * 8, LANES), lambda b: (b, 0))] * len(flat),
        out_specs=[pl.BlockSpec((grp * 8, W), lambda b: (b, 0)), pl.BlockSpec((grp * 8, W), lambda b: (b, 0))],
        out_shape=[jax.ShapeDtypeStruct((n_b * 8, W), F32), jax.ShapeDtypeStruct((n_b * 8, W), F32)],
        compiler_params=_cparams(("parallel",)),
        name="decode_finish",
    )(u_s, u_s, u_s, u_s, mem_k, mem_v, lam_pack, gsub, *flat)
    return od.reshape(n_b, 8, W)[:, 0], om.reshape(n_b, 8, W)[:, 0]


def _outproj_kernel(x_ref, y_ref, od_ref, om_ref, w1_ref, w2_ref, w3_ref, o_ref):
    o_ref[...] = (x_ref[...]
                  + _dot(y_ref[...].astype(BF16), w1_ref[...])
                  + _dot(od_ref[...].astype(BF16), w2_ref[...])
                  + _dot(om_ref[...].astype(BF16), w3_ref[...]))


def _outproj(x, y, od, om, w_out, tm=512, tn=D_MODEL):
    m = x.shape[0]
    tm = min(tm, m)
    assert m % tm == 0
    return pl.pallas_call(
        _outproj_kernel,
        grid=(m // tm, D_MODEL // tn),
        in_specs=[
            pl.BlockSpec((tm, tn), lambda i, j: (i, j)),
            pl.BlockSpec((tm, D_SSM), lambda i, j: (i, 0)),
            pl.BlockSpec((tm, D_DIFF), lambda i, j: (i, 0)),
            pl.BlockSpec((tm, D_MEMX), lambda i, j: (i, 0)),
            pl.BlockSpec((D_SSM, tn), lambda i, j: (0, j)),
            pl.BlockSpec((D_DIFF, tn), lambda i, j: (D_SSM // D_DIFF, j)),
            pl.BlockSpec((D_MEMX, tn), lambda i, j: ((D_SSM + D_DIFF) // D_MEMX, j)),
        ],
        out_specs=pl.BlockSpec((tm, tn), lambda i, j: (i, j)),
        out_shape=jax.ShapeDtypeStruct((m, D_MODEL), F32),
        compiler_params=_cparams(("parallel", "arbitrary")),
        name="out_proj",
    )(x, y, od, om, w_out, w_out, w_out)


def _mlp_kernel(*refs, rider):
    if rider:
        npp, spb = rider["npp"], rider["spb"]
        pt_ref, x_ref, g_ref, wu_ref, wd_ref, tok_q, ck_hbm, cv_hbm = refs[:8]
        o_ref, pm_ref, pl_ref, pa_ref, h_ref, m_s, l_s, acc_s, kbuf, vbuf, sem = refs[8:]
    else:
        x_ref, g_ref, wu_ref, wd_ref, o_ref, h_ref = refs
    f = pl.program_id(1)

    @pl.when(f == 0)
    def _():
        x = x_ref[...]
        r = lax.rsqrt(jnp.mean(x * x, axis=-1, keepdims=True) + EPS)
        h_ref[...] = (x * r * g_ref[...]).astype(BF16)
        o_ref[...] = x

    if rider:
        n_steps = pl.num_programs(0) * pl.num_programs(1)
        step = pl.program_id(0) * pl.num_programs(1) + f
        slot = jnp.bitwise_and(step, 1)
        start, wait = _rider_pages(pt_ref, ck_hbm, cv_hbm, kbuf, vbuf, sem, spb=spb, npp=npp,
                                   page_lo=rider["page_lo"], page_base=rider["page_base"])
        k_refs = [kbuf.at[slot, i] for i in range(npp)]
        v_refs = [vbuf.at[slot, i] for i in range(npp)]
        j, first, main, last = _rider_parts(step, spb, tok_q, k_refs, v_refs, pm_ref, pl_ref, pa_ref,
                                            m_s, l_s, acc_s)
        pl.when(step == 0)(lambda: start(step, slot))
        pl.when(j == 0)(first)
        start(jnp.where(step + 1 == n_steps, 0, step + 1), 1 - slot)

    a = jnp.maximum(_dot(h_ref[...], wu_ref[...]), 0.0)
    o_ref[...] += _dot((a * a).astype(BF16), wd_ref[...])
    if rider:
        wait(slot)
        scores, softmax, weighted = main
        scores(0, npp)
        softmax()
        weighted(0, npp)
        pl.when(j == spb - 1)(last)
        pl.when(step == n_steps - 1)(lambda: wait(1 - slot))


def _mlp(x, g, w_up, w_down, tm=512, tf=1024, rider=None):
    m = x.shape[0]
    d_ff = w_up.shape[1]
    tm = min(tm, m)
    assert m % tm == 0 and d_ff % tf == 0
    grid = (m // tm, d_ff // tf)
    in_specs = [
        pl.BlockSpec((tm, D_MODEL), lambda i, f: (i, 0)),
        pl.BlockSpec((1, D_MODEL), lambda i, f: (0, 0)),
        pl.BlockSpec((D_MODEL, tf), lambda i, f: (0, f)),
        pl.BlockSpec((tf, D_MODEL), lambda i, f: (f, 0)),
    ]
    out_specs = [pl.BlockSpec((tm, D_MODEL), lambda i, f: (i, 0))]
    out_shape = [jax.ShapeDtypeStruct((m, D_MODEL), F32)]
    scratch = [pltpu.VMEM((tm, D_MODEL), BF16)]
    args = [x, g, w_up, w_down]
    if rider is None:
        out = pl.pallas_call(
            functools.partial(_mlp_kernel, rider=None),
            grid=grid, in_specs=in_specs, out_specs=out_specs, out_shape=out_shape, scratch_shapes=scratch,
            compiler_params=_cparams(("parallel", "arbitrary")), name="mlp",
        )(*args)
        return out[0]
    u_s, cache_k, cache_v, page_table, layer, n_pool, npp, page_lo, n_pages = rider
    n_b = page_table.shape[0]
    spb = n_pages // npp
    assert n_pages % npp == 0 and grid[0] * grid[1] == n_b * spb
    nf = grid[1]
    r_in, r_out = _rider_specs(lambda i, f: i * nf + f, spb)
    host_in = [pl.BlockSpec(s.block_shape, lambda i, f, pt, im=s.index_map: im(i, f)) for s in in_specs]
    host_out = [pl.BlockSpec(s.block_shape, lambda i, f, pt, im=s.index_map: im(i, f)) for s in out_specs]
    grid_spec = pltpu.PrefetchScalarGridSpec(
        num_scalar_prefetch=1, grid=grid,
        in_specs=host_in + r_in, out_specs=host_out + r_out,
        scratch_shapes=scratch + [pltpu.VMEM((8, LANES), F32)] * 3 + _rider_scratch(npp),
    )
    part_shape = jax.ShapeDtypeStruct((n_b * 8, LANES), F32)
    rider_cfg = dict(npp=npp, spb=spb, page_lo=page_lo, page_base=layer * n_pool)
    out, pm, pl_, pa = pl.pallas_call(
        functools.partial(_mlp_kernel, rider=rider_cfg),
        grid_spec=grid_spec,
        out_shape=out_shape + [part_shape] * 3,
        compiler_params=_cparams(("arbitrary", "arbitrary")),
        name="mlp_rider",
    )(page_table, *args, u_s, cache_k, cache_v)
    return out, (pm, pl_, pa)


def _pad_lanes(v, width=LANES):
    v = v.reshape(1, -1).astype(F32)
    return jnp.pad(v, ((0, 0), (0, width - v.shape[1])))


def kernel(x_prompt, x_sample, mem_prompt, cache_diff_k, cache_diff_v, cache_mem_k, cache_mem_v, state_conv, state_ssm, page_table, norm_mix, w_in, conv_w, conv_b, dt_bias, a_log, d_skip, g_ssm, g_q, g_k, lambda_q1, lambda_k1, lambda_q2, lambda_k2, g_subln, norm_mem, w_mem_kv, g_mq, g_mk, w_out, norm_mlp, w_up, w_down):
    depth = w_in.shape[0]
    bp, seq, _ = x_prompt.shape
    bd, dec_seq, _ = x_sample.shape
    assert dec_seq == 1 and seq % SSD_T == 0
    n_pool = cache_diff_k.shape[1]

    xp = x_prompt.reshape(bp * seq, D_MODEL)
    xs = x_sample.reshape(bd, D_MODEL)
    cache_k = cache_diff_k.reshape(depth * n_pool * PAGE_ROWS, LANES)
    cache_v = cache_diff_v.reshape(depth * n_pool * PAGE_ROWS, LANES)
    mem_k = cache_mem_k.reshape(depth * bd * MEM_TOKENS * MEM_HEADS, MEM_HD)
    mem_v = cache_mem_v.reshape(depth * bd * MEM_TOKENS * MEM_HEADS, MEM_HD)

    outs = [[] for _ in range(10)]
    for l in range(depth):
        lam_init = 0.8 - 0.6 * math.exp(-0.3 * l)
        wl_t = jnp.swapaxes(w_in[l], 0, 1)
        o_dt = D_SSM + CONV_DIM
        o_q = o_dt + SSM_HEADS
        w_main = [(wl_t, o_dt // PROJ_TN, True), (wl_t[o_q:], (N_MAIN - o_dt) // PROJ_TN, True)]
        w_dt = jnp.pad(wl_t[o_dt:o_q], ((0, LANES - SSM_HEADS), (0, 0))).astype(BF16)
        ones = jnp.ones((PROJ_TN,), F32)
        gains = jnp.concatenate(
            [ones] * 5 + [jnp.tile(g_q[l], 2 * DIFF_HEADS), jnp.tile(g_k[l], 2 * DIFF_HEADS), ones,
                          jnp.tile(g_mq[l], MEM_HEADS)]).reshape(1, N_MAIN)
        modes = (0, 0, 0, 0, 0, DIFF_DK, DIFF_DK, 0, MEM_HD)
        g_mix = norm_mix[l].reshape(1, D_MODEL)

        ssm_p = (jnp.pad(conv_w[l], ((0, 8 - CONV_W), (0, 0))),
                 conv_b[l].reshape(1, CONV_DIM),
                 _pad_lanes(dt_bias[l]), _pad_lanes(a_log[l]),
                 jnp.repeat(d_skip[l], SSM_HEAD_DIM).reshape(1, D_SSM),
                 g_ssm[l].reshape(1, D_SSM))
        lam_pack = jnp.concatenate(
            [_pad_lanes(lambda_q1[l]), _pad_lanes(lambda_k1[l]), _pad_lanes(lambda_q2[l]),
             _pad_lanes(lambda_k2[l]), jnp.zeros((4, LANES), F32)], axis=0)
        gsub = g_subln[l].reshape(1, LANES)
        w_out_bf = w_out[l].astype(BF16)
        w_up_bf = w_up[l].astype(BF16)
        w_down_bf = w_down[l].astype(BF16)
        g_mlp = norm_mlp[l].reshape(1, D_MODEL)

        xs_pad = jnp.pad(xs.reshape(bd, 1, D_MODEL), ((0, 0), (0, ROWS_S - 1), (0, 0))).reshape(bd * ROWS_S, D_MODEL)
        u_s, dt_s = _norm_proj(xs_pad, g_mix, w_main, gains, modes, w_dt=w_dt)

        u_p, dt_p, k_rows, v_rows = _norm_proj(xp, g_mix, w_main, gains, modes, w_dt=w_dt,
                                               rows_blocks=(COL_K // PROJ_TN, COL_V // PROJ_TN), tm=512)
        y_p, conv_p, h_p = _ssd(u_p, dt_p, bp, seq, seq, ssm_p)
        mem_gains = jnp.concatenate([jnp.tile(g_mk[l], MEM_HEADS), ones]).reshape(1, 2 * D_MEMX)
        mkv, = _norm_proj(mem_prompt.reshape(bp * MEM_TOKENS, D_MODEL), norm_mem[l].reshape(1, D_MODEL),
                          [(w_mem_kv[l], 2 * D_MEMX // PROJ_TN, False)], mem_gains, (MEM_HD, 0))
        od_p, om_p = _attn_prompt(u_p, mkv, lam_pack, gsub, bp, seq, lam_init)
        xp_mid = _outproj(xp, y_p, od_p, om_p, w_out_bf)
        n_pages = page_table.shape[1]
        xp_new, part = _mlp(xp_mid, g_mlp, w_up_bf, w_down_bf, tf=512,
                            rider=(u_s, cache_k, cache_v, page_table, l, n_pool, 16, 0, n_pages))

        outs[0].append(k_rows.reshape(bp, seq, DIFF_HEADS, LANES))
        outs[1].append(v_rows.reshape(bp, seq, DIFF_HEADS, LANES))
        outs[4].append(mkv[:, :D_MEMX].reshape(bp, MEM_TOKENS, MEM_HEADS, MEM_HD))
        outs[5].append(mkv[:, D_MEMX:].reshape(bp, MEM_TOKENS, MEM_HEADS, MEM_HD))
        outs[6].append(conv_p[:, :CONV_W - 1])
        outs[7].append(h_p.reshape(bp, SSM_HEADS, SSM_HEAD_DIM, D_STATE))

        conv_prev = jnp.pad(state_conv[l], ((0, 0), (8 - (CONV_W - 1), 0), (0, 0)))
        y_s, conv_s, h_s = _ssd(u_s, dt_s, bd, ROWS_S, 1, ssm_p, conv_prev=conv_prev,
                                h0=state_ssm[l].reshape(bd, D_SSM, D_STATE))
        od_s, om_s = _decode_finish(u_s, [part], mem_k, mem_v, lam_pack, gsub, l, bd, lam_init)
        y_s0 = y_s.reshape(bd, ROWS_S, D_SSM)[:, 0]
        xs_mid = _outproj(xs, y_s0, od_s, om_s, w_out_bf)
        xs_new = _mlp(xs_mid, g_mlp, w_up_bf, w_down_bf)
        u_s0 = u_s.reshape(bd, ROWS_S, N_MAIN)[:, 0]
        outs[2].append(u_s0[:, COL_K:COL_K + D_DIFF].reshape(bd, 1, DIFF_HEADS, LANES))
        outs[3].append(u_s0[:, COL_V:COL_V + D_DIFF].reshape(bd, 1, DIFF_HEADS, LANES))
        outs[8].append(conv_s[:, :CONV_W - 1])
        outs[9].append(h_s.reshape(bd, SSM_HEADS, SSM_HEAD_DIM, D_STATE))

        xp, xs = xp_new, xs_new

    st = [jnp.stack(o) for o in outs]
    return (xp.reshape(bp, seq, D_MODEL), xs.reshape(bd, 1, D_MODEL),
            st[0], st[1], st[2], st[3], st[4], st[5], st[6], st[7], st[8], st[9])
```

```python
import functools
import math

import jax
import jax.numpy as jnp
from jax import lax
from jax.experimental import pallas as pl
from jax.experimental.pallas import tpu as pltpu

F32 = jnp.float32
BF16 = jnp.bfloat16

D_MODEL = 2048
D_SSM = 1024
D_DIFF = 512
D_MEMX = 512
SSM_HEADS = 16
SSM_HEAD_DIM = 64
SSM_GROUPS = 2
D_STATE = 128
CONV_W = 4
BC_DIM = 2 * SSM_GROUPS * D_STATE
CONV_DIM = D_SSM + BC_DIM
DIFF_HEADS = 4
DIFF_DK = 64
MEM_HEADS = 4
MEM_HD = 128
MEM_TOKENS = 256
D_FF = 4 * D_MODEL
EPS = 1e-6
LOG2E = math.log2(math.e)

LANES = 128
SUBLANES_BF16 = 16
VMEM_LIMIT_BYTES = 58 * 1024 * 1024

N_MAIN = D_SSM + D_SSM + BC_DIM + 3 * D_DIFF + D_MEMX
COL_Z, COL_XS, COL_BC, COL_Q, COL_K, COL_V, COL_MQ = 0, 1024, 2048, 2560, 3072, 3584, 4096
PROJ_TN = 512
SSD_T = 128
ROWS_S = SUBLANES_BF16


def _cparams(sem):
    return pltpu.CompilerParams(dimension_semantics=sem, vmem_limit_bytes=VMEM_LIMIT_BYTES)


def _dot(a, b):
    return jnp.dot(a, b, preferred_element_type=F32)


def _dot_nt(a, b):
    return lax.dot_general(a, b, (((1,), (1,)), ((), ())), preferred_element_type=F32)


def _silu(x):
    return x * (1.0 / (1.0 + jnp.exp(-x)))


def _group_rmsnorm_slab(a, gain, group):
    sq = a * a
    if group == LANES:
        r = lax.rsqrt(jnp.sum(sq, axis=-1, keepdims=True) * (1.0 / LANES) + EPS)
    else:
        lane = lax.broadcasted_iota(jnp.int32, a.shape, 1)
        lo = lane < group
        s_lo = jnp.sum(jnp.where(lo, sq, 0.0), axis=-1, keepdims=True)
        s_hi = jnp.sum(jnp.where(lo, 0.0, sq), axis=-1, keepdims=True)
        r = jnp.where(lo, lax.rsqrt(s_lo * (1.0 / group) + EPS),
                      lax.rsqrt(s_hi * (1.0 / group) + EPS))
    return a * r * gain


def _proj_kernel(*refs, modes, seg_of, w_trans, has_dt, rows_blocks):
    n_w = len(w_trans)
    x_ref, g_ref = refs[:2]
    w_refs = refs[2:2 + n_w]
    gain_ref = refs[2 + n_w]
    pos = 3 + n_w
    if has_dt:
        wdt_ref = refs[pos]
        pos += 1
    u_ref = refs[pos]
    pos += 1
    if has_dt:
        dt_ref = refs[pos]
        pos += 1
    rows_refs = refs[pos:pos + len(rows_blocks)]
    h_ref = refs[pos + len(rows_blocks)]
    j = pl.program_id(1)
    tm = x_ref.shape[0]

    @pl.when(j == 0)
    def _():
        x = x_ref[...]
        r = lax.rsqrt(jnp.mean(x * x, axis=-1, keepdims=True) + EPS)
        h_ref[...] = (x * r * g_ref[...]).astype(BF16)
        if has_dt:
            dt_ref[...] = _dot_nt(h_ref[...], wdt_ref[...])

    keys = {}
    for jj, m in enumerate(modes):
        ri = rows_blocks.index(jj) if jj in rows_blocks else None
        keys.setdefault((seg_of[jj], m, ri), []).append(jj)
    for (seg, mode, ri), jjs in keys.items():
        cond = j == jjs[0]
        for jj in jjs[1:]:
            cond = jnp.logical_or(cond, j == jj)

        @pl.when(cond)
        def _(seg=seg, mode=mode, ri=ri):
            w = w_refs[seg][...].astype(BF16)
            acc = _dot_nt(h_ref[...], w) if w_trans[seg] else _dot(h_ref[...], w)
            if mode == 0 and ri is None:
                u_ref[...] = acc
                return
            gain = gain_ref[...]
            for s in range(acc.shape[1] // LANES):
                sl = slice(s * LANES, (s + 1) * LANES)
                slab = acc[:, sl]
                if mode:
                    slab = _group_rmsnorm_slab(slab, gain[:, sl], mode)
                u_ref[:, sl] = slab
                if ri is not None:
                    rows_refs[ri][pl.ds(s, tm, stride=PROJ_TN // LANES), :] = slab


def _norm_proj(x, g, ws, gains, modes, w_dt=None, rows_blocks=(), tm=1024):
    m, k = x.shape
    tm = min(tm, m)
    nblk = [nb for _, nb, _ in ws]
    w_trans = tuple(t for _, _, t in ws)
    ws = [w for w, _, _ in ws]
    assert m % tm == 0 and sum(nblk) == len(modes)
    assert all(w.shape[0 if t else 1] >= nb * PROJ_TN for w, nb, t in zip(ws, nblk, w_trans))
    seg_of, offs = [], []
    for a, nb in enumerate(nblk):
        offs.append(len(seg_of))
        seg_of += [a] * nb
    n = PROJ_TN * len(modes)
    has_dt = w_dt is not None

    def w_spec(a):
        def blk(j):
            return jnp.minimum(jnp.maximum(j - offs[a], 0), nblk[a] - 1)
        if w_trans[a]:
            return pl.BlockSpec((PROJ_TN, k), lambda i, j: (blk(j), 0))
        return pl.BlockSpec((k, PROJ_TN), lambda i, j: (0, blk(j)))

    in_specs = [pl.BlockSpec((tm, k), lambda i, j: (i, 0)), pl.BlockSpec((1, k), lambda i, j: (0, 0))]
    in_specs += [w_spec(a) for a in range(len(ws))]
    in_specs.append(pl.BlockSpec((1, PROJ_TN), lambda i, j: (0, j)))
    args = [x, g, *ws, gains]
    out_shape = [jax.ShapeDtypeStruct((m, n), F32)]
    out_specs = [pl.BlockSpec((tm, PROJ_TN), lambda i, j: (i, j))]
    if has_dt:
        in_specs.append(pl.BlockSpec((LANES, k), lambda i, j: (0, 0)))
        args.append(w_dt)
        out_shape.append(jax.ShapeDtypeStruct((m, LANES), F32))
        out_specs.append(pl.BlockSpec((tm, LANES), lambda i, j: (i, 0)))
    heads = PROJ_TN // LANES
    for _ in rows_blocks:
        out_shape.append(jax.ShapeDtypeStruct((m * heads, LANES), F32))
        out_specs.append(pl.BlockSpec((tm * heads, LANES), lambda i, j: (i, 0)))
    return pl.pallas_call(
        functools.partial(_proj_kernel, modes=tuple(modes), seg_of=tuple(seg_of), w_trans=w_trans, has_dt=has_dt,
                          rows_blocks=tuple(rows_blocks)),
        grid=(m // tm, len(modes)),
        in_specs=in_specs,
        out_specs=out_specs,
        out_shape=out_shape,
        scratch_shapes=[pltpu.VMEM((tm, k), BF16)],
        compiler_params=_cparams(("parallel", "arbitrary")),
        name="norm_proj",
    )(*args)


def _split3(x):
    hi = x.astype(BF16)
    r1 = x - hi.astype(F32)
    mid = r1.astype(BF16)
    lo = (r1 - mid.astype(F32)).astype(BF16)
    return hi, mid, lo


def _ssd_kernel(*refs, t_in, valid_last, nc, has_init):
    if has_init:
        (xs_ref, z_ref, bc_ref, dt_ref, cprev_ref, h0_ref, convw_ref, convb_ref, dtb_ref, alog_ref,
         dskip_ref, gssm_ref, y_ref, cout_ref, hout_ref, xpad, hst) = refs
    else:
        (xs_ref, z_ref, bc_ref, dt_ref, convw_ref, convb_ref, dtb_ref, alog_ref,
         dskip_ref, gssm_ref, y_ref, cout_ref, hout_ref, xpad, hst) = refs
    T = SSD_T
    c = pl.program_id(1)

    @pl.when(c == 0)
    def _():
        if has_init:
            xpad[0:8, :] = cprev_ref[0]
            hst[...] = h0_ref[0]
        else:
            xpad[0:8, :] = jnp.zeros((8, CONV_DIM), F32)
            hst[...] = jnp.zeros(hst.shape, F32)

    def rows(ref):
        v = ref[...]
        if t_in < T:
            v = jnp.concatenate([v, jnp.zeros((T - t_in, v.shape[1]), F32)], axis=0)
        return v

    xpad[8:8 + T, 0:D_SSM] = rows(xs_ref)
    xpad[8:8 + T, D_SSM:CONV_DIM] = rows(bc_ref)
    z = rows(z_ref)
    dt_raw = rows(dt_ref)

    convw = convw_ref[...]
    conv = convb_ref[...] + xpad[5:5 + T, :] * convw[0:1, :]
    for jtap in range(1, CONV_W):
        conv = conv + xpad[5 + jtap:5 + jtap + T, :] * convw[jtap:jtap + 1, :]
    xact = _silu(conv)

    row_i = lax.broadcasted_iota(jnp.int32, (T, LANES), 0)
    col_i = lax.broadcasted_iota(jnp.int32, (T, LANES), 1)
    xv = dt_raw + dtb_ref[...]
    dt = jnp.maximum(xv, 0.0) + jnp.log1p(jnp.exp(-jnp.abs(xv)))
    if valid_last < T:
        dt = jnp.where(row_i < valid_last, dt, 0.0)
    a_neg = -jnp.exp(alog_ref[...])
    adt = dt * a_neg

    tril = (row_i >= col_i)
    tril_bf = jnp.where(tril, 1.0, 0.0).astype(BF16)
    a_hi, a_mid, a_lo = _split3(adt)
    acum = _dot(tril_bf, a_hi) + _dot(tril_bf, a_mid) + _dot(tril_bf, a_lo)
    acum_t = acum.T
    dt_t = dt.T
    e_acum = jnp.exp(acum)
    a_last = acum[T - 1:T, :]
    w_state = jnp.exp(a_last - acum) * dt
    da_last = jnp.exp(a_last)

    lo_half = col_i < SSM_HEAD_DIM
    neg_big = jnp.float32(-1e30)

    def colb(tile, r):
        return jnp.broadcast_to(tile[:, r:r + 1], (T, LANES))

    y_slabs = []
    for g in range(SSM_GROUPS):
        b_g = xact[:, D_SSM + g * D_STATE:D_SSM + (g + 1) * D_STATE]
        c_g = xact[:, D_SSM + SSM_GROUPS * D_STATE + g * D_STATE:
                   D_SSM + SSM_GROUPS * D_STATE + (g + 1) * D_STATE]
        b_bf = b_g.astype(BF16)
        c_bf = c_g.astype(BF16)
        cb = _dot_nt(c_bf, b_bf)
        for pp in range(SSM_HEADS // SSM_GROUPS // 2):
            p = g * (SSM_HEADS // SSM_GROUPS // 2) + pp
            sl = slice(p * LANES, (p + 1) * LANES)
            xs_slab = xact[:, sl]
            y_acc = dskip_ref[:, sl] * xs_slab
            for hh in range(2):
                r = 2 * p + hh
                seg = colb(acum, r) - acum_t[r:r + 1, :]
                lmat = jnp.exp(jnp.where(tril, seg, neg_big))
                mr = (cb * lmat * dt_t[r:r + 1, :]).astype(BF16)
                xh = jnp.where(lo_half if hh == 0 else jnp.logical_not(lo_half), xs_slab, 0.0)
                y_acc = y_acc + _dot(mr, xh.astype(BF16))
            hpair = hst[sl, :]
            e_pair = jnp.where(lo_half, colb(e_acum, 2 * p), colb(e_acum, 2 * p + 1))
            y_acc = y_acc + e_pair * _dot_nt(c_bf, hpair.astype(BF16))
            w_pair = jnp.where(lo_half, colb(w_state, 2 * p), colb(w_state, 2 * p + 1))
            xw_t = (xs_slab * w_pair).T
            st = _dot(xw_t.astype(BF16), b_bf)
            da = jnp.concatenate(
                [jnp.broadcast_to(da_last[:, 2 * p:2 * p + 1], (SSM_HEAD_DIM, LANES)),
                 jnp.broadcast_to(da_last[:, 2 * p + 1:2 * p + 2], (SSM_HEAD_DIM, LANES))], axis=0)
            hst[sl, :] = da * hpair + st
            y_slabs.append(y_acc * _silu(z[:, sl]))

    per_group = D_SSM // SSM_GROUPS // LANES
    for g in range(SSM_GROUPS):
        slabs = y_slabs[g * per_group:(g + 1) * per_group]
        ssum = jnp.sum(slabs[0] * slabs[0], axis=-1, keepdims=True)
        for s in slabs[1:]:
            ssum = ssum + jnp.sum(s * s, axis=-1, keepdims=True)
        r = lax.rsqrt(ssum * (1.0 / (per_group * LANES)) + EPS)
        for k, s in enumerate(slabs):
            sl = slice((g * per_group + k) * LANES, (g * per_group + k + 1) * LANES)
            y_ref[:, sl] = (s * r * gssm_ref[:, sl])[0:t_in, :].astype(y_ref.dtype)

    if nc > 1:
        xpad[5:8, :] = xpad[5 + T:8 + T, :]

    @pl.when(c == nc - 1)
    def _():
        cout_ref[...] = jnp.zeros(cout_ref.shape, F32)
        if nc > 1:
            cout_ref[0, 0:CONV_W - 1, :] = xpad[5:8, :]
        else:
            cout_ref[0, 0:CONV_W - 1, :] = xpad[5 + valid_last:8 + valid_last, :]
        hout_ref[0] = hst[...]


def _ssd(u, dt, n_batch, seq_rows, valid_len, ssm_p, conv_prev=None, h0=None):
    t_in = min(seq_rows, SSD_T)
    nc = max(seq_rows // SSD_T, 1)
    valid_last = valid_len - (nc - 1) * SSD_T
    has_init = conv_prev is not None
    convw, convb, dtb, alog, dskip, gssm = ssm_p
    ntb = N_MAIN // PROJ_TN

    def row(b, c):
        return b * nc + c

    in_specs = [
        pl.BlockSpec((t_in, D_SSM), lambda b, c: (row(b, c), COL_XS // D_SSM)),
        pl.BlockSpec((t_in, D_SSM), lambda b, c: (row(b, c), COL_Z // D_SSM)),
        pl.BlockSpec((t_in, BC_DIM), lambda b, c: (row(b, c), COL_BC // BC_DIM)),
        pl.BlockSpec((t_in, LANES), lambda b, c: (row(b, c), 0)),
    ]
    args = [u, u, u, dt]
    if has_init:
        in_specs += [pl.BlockSpec((1, 8, CONV_DIM), lambda b, c: (b, 0, 0)),
                     pl.BlockSpec((1, D_SSM, D_STATE), lambda b, c: (b, 0, 0))]
        args += [conv_prev, h0]
    in_specs += [
        pl.BlockSpec((8, CONV_DIM), lambda b, c: (0, 0)),
        pl.BlockSpec((1, CONV_DIM), lambda b, c: (0, 0)),
        pl.BlockSpec((1, LANES), lambda b, c: (0, 0)),
        pl.BlockSpec((1, LANES), lambda b, c: (0, 0)),
        pl.BlockSpec((1, D_SSM), lambda b, c: (0, 0)),
        pl.BlockSpec((1, D_SSM), lambda b, c: (0, 0)),
    ]
    args += [convw, convb, dtb, alog, dskip, gssm]
    del ntb
    return pl.pallas_call(
        functools.partial(_ssd_kernel, t_in=t_in, valid_last=valid_last, nc=nc, has_init=has_init),
        grid=(n_batch, nc),
        in_specs=in_specs,
        out_specs=[
            pl.BlockSpec((t_in, D_SSM), lambda b, c: (row(b, c), 0)),
            pl.BlockSpec((1, 8, CONV_DIM), lambda b, c: (b, 0, 0)),
            pl.BlockSpec((1, D_SSM, D_STATE), lambda b, c: (b, 0, 0)),
        ],
        out_shape=[
            jax.ShapeDtypeStruct((n_batch * seq_rows, D_SSM), BF16),
            jax.ShapeDtypeStruct((n_batch, 8, CONV_DIM), F32),
            jax.ShapeDtypeStruct((n_batch, D_SSM, D_STATE), F32),
        ],
        scratch_shapes=[pltpu.VMEM((8 + SSD_T, CONV_DIM), F32), pltpu.VMEM((D_SSM, D_STATE), F32)],
        compiler_params=_cparams(("parallel", "arbitrary")),
        name="ssd_scan",
    )(*args)


def _lambda_from(lam_ref, lam_init):
    lp = lam_ref[...]
    s1 = jnp.sum(lp[0:1, :] * lp[1:2, :], axis=-1, keepdims=True)
    s2 = jnp.sum(lp[2:3, :] * lp[3:4, :], axis=-1, keepdims=True)
    return jnp.exp(s1) - jnp.exp(s2) + lam_init


def _attn_prompt_kernel(q_ref, k_ref, v_ref, mq_ref, mk_ref, mv_ref, lam_ref, gsub_ref,
                        od_ref, om_ref, kb, vb, s_s, m_s, l_s, acc_s, *, tq, lam_init):
    qi = pl.program_id(2)

    @pl.when(qi == 0)
    def _():
        kb[...] = k_ref[...].astype(BF16)
        vb[...] = v_ref[...].astype(BF16)

    lane = lax.broadcasted_iota(jnp.int32, (tq, LANES), 1)
    lo = lane < DIFF_DK
    q = q_ref[...] * (DIFF_DK ** -0.5 * LOG2E)
    q2 = jnp.concatenate([jnp.where(lo, q, 0.0), jnp.where(lo, 0.0, q)], axis=0).astype(BF16)
    m_s[...] = jnp.full(m_s.shape, -jnp.inf, F32)
    l_s[...] = jnp.zeros(l_s.shape, F32)
    acc_s[...] = jnp.zeros(acc_s.shape, F32)
    reps = tq // LANES

    def scores(j, masked):
        start = pl.multiple_of(j * tq, tq)
        s = _dot_nt(q2, kb[pl.ds(start, tq), :])
        if masked:
            r_i = jnp.bitwise_and(lax.broadcasted_iota(jnp.int32, (2 * tq, tq), 0), tq - 1)
            c_i = lax.broadcasted_iota(jnp.int32, (2 * tq, tq), 1)
            s = jnp.where(r_i >= c_i, s, -jnp.inf)
        s_s[:, pl.ds(start, tq)] = s
        m_s[...] = jnp.maximum(m_s[...], jnp.max(s, axis=-1, keepdims=True))

    def weighted(j):
        start = pl.multiple_of(j * tq, tq)
        m_rep = jnp.concatenate([m_s[...]] * reps, axis=1)
        p = jnp.exp2(s_s[:, pl.ds(start, tq)] - m_rep)
        l_s[...] += jnp.sum(p, axis=-1, keepdims=True)
        acc_s[...] += _dot(p.astype(BF16), vb[pl.ds(start, tq), :])

    def body1(j, carry):
        scores(j, False)
        return carry

    def body2(j, carry):
        weighted(j)
        return carry

    lax.fori_loop(0, qi, body1, 0)
    scores(qi, True)
    lax.fori_loop(0, qi + 1, body2, 0)

    lam = _lambda_from(lam_ref, lam_init)
    o = acc_s[0:tq, :] / l_s[0:tq, :] - lam * (acc_s[tq:2 * tq, :] / l_s[tq:2 * tq, :])
    r = lax.rsqrt(jnp.mean(o * o, axis=-1, keepdims=True) + EPS)
    od_ref[...] = ((o * r * gsub_ref[...]) * (1.0 - lam_init)).astype(od_ref.dtype)

    s = _dot_nt(mq_ref[...].astype(BF16), mk_ref[...].astype(BF16)) * (MEM_HD ** -0.5 * LOG2E)
    e = jnp.exp2(s - jnp.max(s, axis=-1, keepdims=True))
    om = _dot(e.astype(BF16), mv_ref[...].astype(BF16)) / jnp.sum(e, axis=-1, keepdims=True)
    om_ref[...] = om.astype(om_ref.dtype)


def _attn_prompt(u, mkv, lam_pack, gsub, n_batch, seq, lam_init, tq=512):
    nq = seq // tq
    cq, ck, cv, cmq = (COL_Q // LANES, COL_K // LANES, COL_V // LANES, COL_MQ // LANES)
    return pl.pallas_call(
        functools.partial(_attn_prompt_kernel, tq=tq, lam_init=lam_init),
        grid=(n_batch, DIFF_HEADS, nq),
        in_specs=[
            pl.BlockSpec((tq, LANES), lambda b, h, i: (b * nq + i, cq + h)),
            pl.BlockSpec((seq, LANES), lambda b, h, i: (b, ck + h)),
            pl.BlockSpec((seq, LANES), lambda b, h, i: (b, cv + h)),
            pl.BlockSpec((tq, LANES), lambda b, h, i: (b * nq + i, cmq + h)),
            pl.BlockSpec((MEM_TOKENS, LANES), lambda b, h, i: (b, h)),
            pl.BlockSpec((MEM_TOKENS, LANES), lambda b, h, i: (b, MEM_HEADS + h)),
            pl.BlockSpec((8, LANES), lambda b, h, i: (0, 0)),
            pl.BlockSpec((1, LANES), lambda b, h, i: (0, 0)),
        ],
        out_specs=[
            pl.BlockSpec((tq, LANES), lambda b, h, i: (b * nq + i, h)),
            pl.BlockSpec((tq, LANES), lambda b, h, i: (b * nq + i, h)),
        ],
        out_shape=[
            jax.ShapeDtypeStruct((n_batch * seq, D_DIFF), BF16),
            jax.ShapeDtypeStruct((n_batch * seq, D_MEMX), BF16),
        ],
        scratch_shapes=[
            pltpu.VMEM((seq, LANES), BF16),
            pltpu.VMEM((seq, LANES), BF16),
            pltpu.VMEM((2 * tq, seq), F32),
            pltpu.VMEM((2 * tq, LANES), F32),
            pltpu.VMEM((2 * tq, LANES), F32),
            pltpu.VMEM((2 * tq, LANES), F32),
        ],
        compiler_params=_cparams(("parallel", "parallel", "arbitrary")),
        name="attn_prompt",
    )(u, u, u, u, mkv, mkv, lam_pack, gsub)


PAGE = 128
PAGE_ROWS = PAGE * DIFF_HEADS


def _per_head_rows(x_row, split_maps):
    lane = lax.broadcasted_iota(jnp.int32, (1, LANES), 1)
    rows = []
    for r in range(2 * DIFF_HEADS):
        xh = x_row[:, (r // 2) * LANES:(r // 2 + 1) * LANES]
        if split_maps:
            xh = jnp.where((lane < DIFF_DK) if r % 2 == 0 else (lane >= DIFF_DK), xh, 0.0)
        rows.append(xh)
    return jnp.concatenate(rows, axis=0)


def _head_match(n_cols):
    r_i = lax.broadcasted_iota(jnp.int32, (8, n_cols), 0)
    c_i = lax.broadcasted_iota(jnp.int32, (8, n_cols), 1)
    return jnp.bitwise_and(c_i, DIFF_HEADS - 1) == jnp.right_shift(r_i, 1)


def _rider_specs(step_fn, spb):
    W = DIFF_HEADS * LANES
    assert spb & (spb - 1) == 0, "steps per sequence must be a power of two (shift/mask indexing)"
    shift = spb.bit_length() - 1

    def seq(idx):
        return jnp.right_shift(step_fn(*idx[:-1]), shift)

    tok_q = pl.BlockSpec((ROWS_S, W), lambda *idx: (seq(idx), COL_Q // W))
    hbm = pl.BlockSpec(memory_space=pl.ANY)
    part = pl.BlockSpec((8, LANES), lambda *idx: (seq(idx), 0))
    return [tok_q, hbm, hbm], [part, part, part]


def _rider_scratch(npp):
    buf = pltpu.VMEM((2, npp, PAGE_ROWS, LANES), F32)
    return [buf, buf, pltpu.SemaphoreType.DMA((2, 2))]


def _rider_pages(pt_ref, ck_hbm, cv_hbm, kbuf, vbuf, sem, *, spb, npp, page_lo, page_base):
    shift = spb.bit_length() - 1

    def copies(slot, rows):
        out = []
        for i in range(npp):
            out.append(pltpu.make_async_copy(ck_hbm.at[pl.ds(rows[i], PAGE_ROWS), :], kbuf.at[slot, i],
                                             sem.at[0, slot]))
            out.append(pltpu.make_async_copy(cv_hbm.at[pl.ds(rows[i], PAGE_ROWS), :], vbuf.at[slot, i],
                                             sem.at[1, slot]))
        return out

    def start(step, slot):
        seq = jnp.right_shift(step, shift)
        col = page_lo + jnp.bitwise_and(step, spb - 1) * npp
        rows = [pl.multiple_of((page_base + pt_ref[seq, col + i]) * PAGE_ROWS, PAGE_ROWS) for i in range(npp)]
        for c in copies(slot, rows):
            c.start()

    def wait(slot):
        for c in copies(slot, [0] * npp):
            c.wait()

    return start, wait


def _rider_parts(step, spb, tok_q, k_refs, v_refs, pm_ref, pl_ref, pa_ref, m_s, l_s, acc_s):
    npp = len(k_refs)
    j = jnp.bitwise_and(step, spb - 1)

    def first():
        m_s[...] = jnp.full(m_s.shape, -jnp.inf, F32)
        l_s[...] = jnp.zeros(l_s.shape, F32)
        acc_s[...] = jnp.zeros(acc_s.shape, F32)

    st = {"s": [], "pv": None}

    def scores(lo, hi):
        if not st["s"]:
            st["q"] = _per_head_rows(tok_q[0:1, :] * (DIFF_DK ** -0.5), True).astype(BF16)
        st["s"] += [_dot_nt(st["q"], k_refs[i][...].astype(BF16)) for i in range(lo, hi)]

    def softmax():
        s_all = jnp.concatenate(st["s"], axis=1)
        s_all = jnp.where(_head_match(npp * PAGE_ROWS), s_all, -jnp.inf)
        m_prev = m_s[...]
        m_new = jnp.maximum(m_prev, jnp.max(s_all, axis=-1, keepdims=True))
        st["alpha"] = jnp.exp(m_prev - m_new)
        p = jnp.exp(s_all - m_new[:, 0:1])
        l_s[...] = st["alpha"] * l_s[...] + jnp.sum(p, axis=-1, keepdims=True)
        m_s[...] = m_new
        st["p"] = p.astype(BF16)

    def weighted(lo, hi):
        for i in range(lo, hi):
            pv = _dot(st["p"][:, i * PAGE_ROWS:(i + 1) * PAGE_ROWS], v_refs[i][...].astype(BF16))
            st["pv"] = pv if st["pv"] is None else st["pv"] + pv
        if hi == npp:
            acc_s[...] = st["alpha"] * acc_s[...] + st["pv"]

    def last():
        pm_ref[...] = m_s[...]
        pl_ref[...] = l_s[...]
        pa_ref[...] = acc_s[...]

    return j, first, (scores, softmax, weighted), last


def _decode_finish_kernel(tok_q, tok_k, tok_v, tok_mq, mk_ref, mv_ref, lam_ref, gsub_ref, *rest, n_parts, grp,
                          lam_init):
    parts = rest[:3 * n_parts]
    od_ref, om_ref = rest[3 * n_parts:]
    mem_rows = MEM_TOKENS * MEM_HEADS
    lam = _lambda_from(lam_ref, lam_init)
    for g in range(grp):
        t0 = g * ROWS_S
        r8 = slice(g * 8, (g + 1) * 8)
        qmat = _per_head_rows(tok_q[t0:t0 + 1, :] * (DIFF_DK ** -0.5), True)
        s_new = jnp.sum(qmat * _per_head_rows(tok_k[t0:t0 + 1, :], False), axis=-1, keepdims=True)
        m_tot = jnp.broadcast_to(s_new, (8, LANES))
        for h in range(n_parts):
            m_tot = jnp.maximum(m_tot, parts[3 * h][r8, :])
        w_new = jnp.exp(s_new - m_tot)
        l_tot = w_new
        acc = w_new * _per_head_rows(tok_v[t0:t0 + 1, :], False)
        for h in range(n_parts):
            w = jnp.exp(parts[3 * h][r8, :] - m_tot)
            l_tot = l_tot + w * parts[3 * h + 1][r8, :]
            acc = acc + w * parts[3 * h + 2][r8, :]
        o_all = acc / l_tot
        outs = []
        for h in range(DIFF_HEADS):
            o = o_all[2 * h:2 * h + 1, :] - lam * o_all[2 * h + 1:2 * h + 2, :]
            r = lax.rsqrt(jnp.mean(o * o, axis=-1, keepdims=True) + EPS)
            outs.append((o * r * gsub_ref[...]) * (1.0 - lam_init))
        od_ref[r8, :] = jnp.broadcast_to(jnp.concatenate(outs, axis=1), (8, DIFF_HEADS * LANES))

        mrows = slice(g * mem_rows, (g + 1) * mem_rows)
        mqm = _per_head_rows(tok_mq[t0:t0 + 1, :], False)
        s = _dot_nt(mqm.astype(BF16), mk_ref[mrows, :].astype(BF16)) * (MEM_HD ** -0.5)
        s = jnp.where(_head_match(s.shape[1]), s, -jnp.inf)
        e = jnp.exp(s - jnp.max(s, axis=-1, keepdims=True))
        om = _dot(e.astype(BF16), mv_ref[mrows, :].astype(BF16)) / jnp.sum(e, axis=-1, keepdims=True)
        om_row = jnp.concatenate([om[2 * h:2 * h + 1, :] for h in range(MEM_HEADS)], axis=1)
        om_ref[r8, :] = jnp.broadcast_to(om_row, (8, MEM_HEADS * LANES))


def _decode_finish(u_s, parts, mem_k, mem_v, lam_pack, gsub, layer, n_b, lam_init, grp=4):
    W = DIFF_HEADS * LANES
    mem_rows = MEM_TOKENS * MEM_HEADS
    flat = [a for tri in parts for a in tri]
    assert n_b % grp == 0
    nblk = n_b // grp

    def tok(col):
        return pl.BlockSpec((grp * ROWS_S, W), lambda b: (b, col // W))

    od, om = pl.pallas_call(
        functools.partial(_decode_finish_kernel, n_parts=len(parts), grp=grp, lam_init=lam_init),
        grid=(nblk,),
        in_specs=[tok(COL_Q), tok(COL_K), tok(COL_V), tok(COL_MQ),
                  pl.BlockSpec((grp * mem_rows, LANES), lambda b: (layer * nblk + b, 0)),
                  pl.BlockSpec((grp * mem_rows, LANES), lambda b: (layer * nblk + b, 0)),
                  pl.BlockSpec((8, LANES), lambda b: (0, 0)),
                  pl.BlockSpec((1, LANES), lambda b: (0, 0))]
                 + [pl.BlockSpec((grp * 8, LANES), lambda b: (b, 0))] * len(flat),
        out_specs=[pl.BlockSpec((grp * 8, W), lambda b: (b, 0)), pl.BlockSpec((grp * 8, W), lambda b: (b, 0))],
        out_shape=[jax.ShapeDtypeStruct((n_b * 8, W), F32), jax.ShapeDtypeStruct((n_b * 8, W), F32)],
        compiler_params=_cparams(("parallel",)),
        name="decode_finish",
    )(u_s, u_s, u_s, u_s, mem_k, mem_v, lam_pack, gsub, *flat)
    return od.reshape(n_b, 8, W)[:, 0], om.reshape(n_b, 8, W)[:, 0]


def _outproj_kernel(x_ref, y_ref, od_ref, om_ref, w1_ref, w2_ref, w3_ref, o_ref):
    o_ref[...] = (x_ref[...]
                  + _dot(y_ref[...].astype(BF16), w1_ref[...])
                  + _dot(od_ref[...].astype(BF16), w2_ref[...])
                  + _dot(om_ref[...].astype(BF16), w3_ref[...]))


def _outproj(x, y, od, om, w_out, tm=512, tn=D_MODEL):
    m = x.shape[0]
    tm = min(tm, m)
    assert m % tm == 0
    return pl.pallas_call(
        _outproj_kernel,
        grid=(m // tm, D_MODEL // tn),
        in_specs=[
            pl.BlockSpec((tm, tn), lambda i, j: (i, j)),
            pl.BlockSpec((tm, D_SSM), lambda i, j: (i, 0)),
            pl.BlockSpec((tm, D_DIFF), lambda i, j: (i, 0)),
            pl.BlockSpec((tm, D_MEMX), lambda i, j: (i, 0)),
            pl.BlockSpec((D_SSM, tn), lambda i, j: (0, j)),
            pl.BlockSpec((D_DIFF, tn), lambda i, j: (D_SSM // D_DIFF, j)),
            pl.BlockSpec((D_MEMX, tn), lambda i, j: ((D_SSM + D_DIFF) // D_MEMX, j)),
        ],
        out_specs=pl.BlockSpec((tm, tn), lambda i, j: (i, j)),
        out_shape=jax.ShapeDtypeStruct((m, D_MODEL), F32),
        compiler_params=_cparams(("parallel", "arbitrary")),
        name="out_proj",
    )(x, y, od, om, w_out, w_out, w_out)


def _mlp_kernel(*refs, rider):
    if rider:
        npp, spb = rider["npp"], rider["spb"]
        pt_ref, x_ref, g_ref, wu_ref, wd_ref, tok_q, ck_hbm, cv_hbm = refs[:8]
        o_ref, pm_ref, pl_ref, pa_ref, h_ref, m_s, l_s, acc_s, kbuf, vbuf, sem = refs[8:]
    else:
        x_ref, g_ref, wu_ref, wd_ref, o_ref, h_ref = refs
    f = pl.program_id(1)

    @pl.when(f == 0)
    def _():
        x = x_ref[...]
        r = lax.rsqrt(jnp.mean(x * x, axis=-1, keepdims=True) + EPS)
        h_ref[...] = (x * r * g_ref[...]).astype(BF16)
        o_ref[...] = x

    if not rider:
        a = jnp.maximum(_dot(h_ref[...], wu_ref[...]), 0.0)
        o_ref[...] += _dot((a * a).astype(BF16), wd_ref[...])
        return

    rounds = rider["rounds"]
    n_steps = pl.num_programs(0) * pl.num_programs(1)
    n_rounds = n_steps * rounds
    step = pl.program_id(0) * pl.num_programs(1) + f
    start, wait = _rider_pages(pt_ref, ck_hbm, cv_hbm, kbuf, vbuf, sem, spb=spb, npp=npp,
                               page_lo=rider["page_lo"], page_base=rider["page_base"])
    ts = [step * rounds + r for r in range(rounds)]
    slots = [jnp.bitwise_and(t, 1) for t in ts]
    parts = []
    for t, slot in zip(ts, slots):
        k_refs = [kbuf.at[slot, i] for i in range(npp)]
        v_refs = [vbuf.at[slot, i] for i in range(npp)]
        parts.append(_rider_parts(t, spb, tok_q, k_refs, v_refs, pm_ref, pl_ref, pa_ref, m_s, l_s, acc_s))

    def run_round(r):
        wait(slots[r])
        scores, softmax, weighted = parts[r][2]
        scores(0, npp)
        softmax()
        weighted(0, npp)

    def next_round(t):
        return jnp.where(t + 1 == n_rounds, 0, t + 1)

    pl.when(step == 0)(lambda: start(ts[0], slots[0]))
    pl.when(parts[0][0] == 0)(parts[0][1])
    start(next_round(ts[0]), 1 - slots[0])
    a = jnp.maximum(_dot(h_ref[...], wu_ref[...]), 0.0)
    if rounds == 2:
        run_round(0)
        start(next_round(ts[1]), slots[0])
    o_ref[...] += _dot((a * a).astype(BF16), wd_ref[...])
    run_round(rounds - 1)
    pl.when(parts[-1][0] == spb - 1)(parts[-1][3])
    pl.when(step == n_steps - 1)(lambda: wait(1 - slots[-1]))


def _mlp(x, g, w_up, w_down, tm=512, tf=1024, rider=None):
    m = x.shape[0]
    d_ff = w_up.shape[1]
    tm = min(tm, m)
    assert m % tm == 0 and d_ff % tf == 0
    grid = (m // tm, d_ff // tf)
    in_specs = [
        pl.BlockSpec((tm, D_MODEL), lambda i, f: (i, 0)),
        pl.BlockSpec((1, D_MODEL), lambda i, f: (0, 0)),
        pl.BlockSpec((D_MODEL, tf), lambda i, f: (0, f)),
        pl.BlockSpec((tf, D_MODEL), lambda i, f: (f, 0)),
    ]
    out_specs = [pl.BlockSpec((tm, D_MODEL), lambda i, f: (i, 0))]
    out_shape = [jax.ShapeDtypeStruct((m, D_MODEL), F32)]
    scratch = [pltpu.VMEM((tm, D_MODEL), BF16)]
    args = [x, g, w_up, w_down]
    if rider is None:
        out = pl.pallas_call(
            functools.partial(_mlp_kernel, rider=None),
            grid=grid, in_specs=in_specs, out_specs=out_specs, out_shape=out_shape, scratch_shapes=scratch,
            compiler_params=_cparams(("parallel", "arbitrary")), name="mlp",
        )(*args)
        return out[0]
    u_s, cache_k, cache_v, page_table, layer, n_pool, npp, page_lo, n_pages = rider
    n_b = page_table.shape[0]
    spb = n_pages // npp
    n_steps = grid[0] * grid[1]
    rounds = n_b * spb // n_steps
    assert n_pages % npp == 0 and rounds in (1, 2) and n_steps * rounds == n_b * spb and spb % rounds == 0
    nf = grid[1]
    r_in, r_out = _rider_specs(lambda i, f: i * nf + f, spb // rounds)
    host_in = [pl.BlockSpec(s.block_shape, lambda i, f, pt, im=s.index_map: im(i, f)) for s in in_specs]
    host_in[0] = pl.BlockSpec(host_in[0].block_shape, host_in[0].index_map, pipeline_mode=pl.Buffered(1))
    host_out = [pl.BlockSpec(s.block_shape, lambda i, f, pt, im=s.index_map: im(i, f)) for s in out_specs]
    grid_spec = pltpu.PrefetchScalarGridSpec(
        num_scalar_prefetch=1, grid=grid,
        in_specs=host_in + r_in, out_specs=host_out + r_out,
        scratch_shapes=scratch + [pltpu.VMEM((8, LANES), F32)] * 3 + _rider_scratch(npp),
    )
    part_shape = jax.ShapeDtypeStruct((n_b * 8, LANES), F32)
    rider_cfg = dict(npp=npp, spb=spb, rounds=rounds, page_lo=page_lo, page_base=layer * n_pool)
    out, pm, pl_, pa = pl.pallas_call(
        functools.partial(_mlp_kernel, rider=rider_cfg),
        grid_spec=grid_spec,
        out_shape=out_shape + [part_shape] * 3,
        compiler_params=_cparams(("arbitrary", "arbitrary")),
        name="mlp_rider",
    )(page_table, *args, u_s, cache_k, cache_v)
    return out, (pm, pl_, pa)


def _pad_lanes(v, width=LANES):
    v = v.reshape(1, -1).astype(F32)
    return jnp.pad(v, ((0, 0), (0, width - v.shape[1])))


def kernel(x_prompt, x_sample, mem_prompt, cache_diff_k, cache_diff_v, cache_mem_k, cache_mem_v, state_conv, state_ssm, page_table, norm_mix, w_in, conv_w, conv_b, dt_bias, a_log, d_skip, g_ssm, g_q, g_k, lambda_q1, lambda_k1, lambda_q2, lambda_k2, g_subln, norm_mem, w_mem_kv, g_mq, g_mk, w_out, norm_mlp, w_up, w_down):
    depth = w_in.shape[0]
    bp, seq, _ = x_prompt.shape
    bd, dec_seq, _ = x_sample.shape
    assert dec_seq == 1 and seq % SSD_T == 0
    n_pool = cache_diff_k.shape[1]

    xp = x_prompt.reshape(bp * seq, D_MODEL)
    xs = x_sample.reshape(bd, D_MODEL)
    cache_k = cache_diff_k.reshape(depth * n_pool * PAGE_ROWS, LANES)
    cache_v = cache_diff_v.reshape(depth * n_pool * PAGE_ROWS, LANES)
    mem_k = cache_mem_k.reshape(depth * bd * MEM_TOKENS * MEM_HEADS, MEM_HD)
    mem_v = cache_mem_v.reshape(depth * bd * MEM_TOKENS * MEM_HEADS, MEM_HD)

    outs = [[] for _ in range(10)]
    for l in range(depth):
        lam_init = 0.8 - 0.6 * math.exp(-0.3 * l)
        wl_t = jnp.swapaxes(w_in[l], 0, 1).astype(BF16)
        o_dt = D_SSM + CONV_DIM
        o_q = o_dt + SSM_HEADS
        w_main = [(wl_t, o_dt // PROJ_TN, True), (wl_t[o_q:], (N_MAIN - o_dt) // PROJ_TN, True)]
        w_dt = jnp.pad(wl_t[o_dt:o_q], ((0, LANES - SSM_HEADS), (0, 0)))
        ones = jnp.ones((PROJ_TN,), F32)
        gains = jnp.concatenate(
            [ones] * 5 + [jnp.tile(g_q[l], 2 * DIFF_HEADS), jnp.tile(g_k[l], 2 * DIFF_HEADS), ones,
                          jnp.tile(g_mq[l], MEM_HEADS)]).reshape(1, N_MAIN)
        modes = (0, 0, 0, 0, 0, DIFF_DK, DIFF_DK, 0, MEM_HD)
        g_mix = norm_mix[l].reshape(1, D_MODEL)

        ssm_p = (jnp.pad(conv_w[l], ((0, 8 - CONV_W), (0, 0))),
                 conv_b[l].reshape(1, CONV_DIM),
                 _pad_lanes(dt_bias[l]), _pad_lanes(a_log[l]),
                 jnp.repeat(d_skip[l], SSM_HEAD_DIM).reshape(1, D_SSM),
                 g_ssm[l].reshape(1, D_SSM))
        lam_pack = jnp.concatenate(
            [_pad_lanes(lambda_q1[l]), _pad_lanes(lambda_k1[l]), _pad_lanes(lambda_q2[l]),
             _pad_lanes(lambda_k2[l]), jnp.zeros((4, LANES), F32)], axis=0)
        gsub = g_subln[l].reshape(1, LANES)
        w_out_bf = w_out[l].astype(BF16)
        w_up_bf = w_up[l].astype(BF16)
        w_down_bf = w_down[l].astype(BF16)
        g_mlp = norm_mlp[l].reshape(1, D_MODEL)

        xs_pad = jnp.pad(xs.reshape(bd, 1, D_MODEL), ((0, 0), (0, ROWS_S - 1), (0, 0))).reshape(bd * ROWS_S, D_MODEL)
        u_s, dt_s = _norm_proj(xs_pad, g_mix, w_main, gains, modes, w_dt=w_dt)

        u_p, dt_p, k_rows, v_rows = _norm_proj(xp, g_mix, w_main, gains, modes, w_dt=w_dt,
                                               rows_blocks=(COL_K // PROJ_TN, COL_V // PROJ_TN))
        y_p, conv_p, h_p = _ssd(u_p, dt_p, bp, seq, seq, ssm_p)
        mem_gains = jnp.concatenate([jnp.tile(g_mk[l], MEM_HEADS), ones]).reshape(1, 2 * D_MEMX)
        mkv, = _norm_proj(mem_prompt.reshape(bp * MEM_TOKENS, D_MODEL), norm_mem[l].reshape(1, D_MODEL),
                          [(w_mem_kv[l], 2 * D_MEMX // PROJ_TN, False)], mem_gains, (MEM_HD, 0))
        od_p, om_p = _attn_prompt(u_p, mkv, lam_pack, gsub, bp, seq, lam_init)
        xp_mid = _outproj(xp, y_p, od_p, om_p, w_out_bf)
        n_pages = page_table.shape[1]
        xp_new, part = _mlp(xp_mid, g_mlp, w_up_bf, w_down_bf, tm=1024, tf=512,
                            rider=(u_s, cache_k, cache_v, page_table, l, n_pool, 16, 0, n_pages))

        outs[0].append(k_rows.reshape(bp, seq, DIFF_HEADS, LANES))
        outs[1].append(v_rows.reshape(bp, seq, DIFF_HEADS, LANES))
        outs[4].append(mkv[:, :D_MEMX].reshape(bp, MEM_TOKENS, MEM_HEADS, MEM_HD))
        outs[5].append(mkv[:, D_MEMX:].reshape(bp, MEM_TOKENS, MEM_HEADS, MEM_HD))
        outs[6].append(conv_p[:, :CONV_W - 1])
        outs[7].append(h_p.reshape(bp, SSM_HEADS, SSM_HEAD_DIM, D_STATE))

        conv_prev = jnp.pad(state_conv[l], ((0, 0), (8 - (CONV_W - 1), 0), (0, 0)))
        y_s, conv_s, h_s = _ssd(u_s, dt_s, bd, ROWS_S, 1, ssm_p, conv_prev=conv_prev,
                                h0=state_ssm[l].reshape(bd, D_SSM, D_STATE))
        od_s, om_s = _decode_finish(u_s, [part], mem_k, mem_v, lam_pack, gsub, l, bd, lam_init)
        y_s0 = y_s.reshape(bd, ROWS_S, D_SSM)[:, 0]
        xs_mid = _outproj(xs, y_s0, od_s, om_s, w_out_bf)
        xs_new = _mlp(xs_mid, g_mlp, w_up_bf, w_down_bf)
        u_s0 = u_s.reshape(bd, ROWS_S, N_MAIN)[:, 0]
        outs[2].append(u_s0[:, COL_K:COL_K + D_DIFF].reshape(bd, 1, DIFF_HEADS, LANES))
        outs[3].append(u_s0[:, COL_V:COL_V + D_DIFF].reshape(bd, 1, DIFF_HEADS, LANES))
        outs[8].append(conv_s[:, :CONV_W - 1])
        outs[9].append(h_s.reshape(bd, SSM_HEADS, SSM_HEAD_DIM, D_STATE))

        xp, xs = xp_new, xs_new

    st = [jnp.stack(o) for o in outs]
    return (xp.reshape(bp, seq, D_MODEL), xs.reshape(bd, 1, D_MODEL),
            st[0], st[1], st[2], st[3], st[4], st[5], st[6], st[7], st[8], st[9])
```

```python
import functools
import math

import jax
import jax.numpy as jnp
from jax import lax
from jax.experimental import pallas as pl
from jax.experimental.pallas import tpu as pltpu

F32 = jnp.float32
BF16 = jnp.bfloat16

D_MODEL = 2048
D_SSM = 1024
D_DIFF = 512
D_MEMX = 512
SSM_HEADS = 16
SSM_HEAD_DIM = 64
SSM_GROUPS = 2
D_STATE = 128
CONV_W = 4
BC_DIM = 2 * SSM_GROUPS * D_STATE
CONV_DIM = D_SSM + BC_DIM
DIFF_HEADS = 4
DIFF_DK = 64
MEM_HEADS = 4
MEM_HD = 128
MEM_TOKENS = 256
D_FF = 4 * D_MODEL
EPS = 1e-6
LOG2E = math.log2(math.e)

LANES = 128
SUBLANES_BF16 = 16
VMEM_LIMIT_BYTES = 58 * 1024 * 1024

N_MAIN = D_SSM + D_SSM + BC_DIM + 3 * D_DIFF + D_MEMX
COL_Z, COL_XS, COL_BC, COL_Q, COL_K, COL_V, COL_MQ = 0, 1024, 2048, 2560, 3072, 3584, 4096
PROJ_TN = 512
SSD_T = 128
ROWS_S = SUBLANES_BF16


def _cparams(sem):
    return pltpu.CompilerParams(dimension_semantics=sem, vmem_limit_bytes=VMEM_LIMIT_BYTES)


def _dot(a, b):
    return jnp.dot(a, b, preferred_element_type=F32)


def _dot_nt(a, b):
    return lax.dot_general(a, b, (((1,), (1,)), ((), ())), preferred_element_type=F32)


def _silu(x):
    return x * (1.0 / (1.0 + jnp.exp(-x)))


def _group_rmsnorm_slab(a, gain, group):
    sq = a * a
    if group == LANES:
        r = lax.rsqrt(jnp.sum(sq, axis=-1, keepdims=True) * (1.0 / LANES) + EPS)
    else:
        lane = lax.broadcasted_iota(jnp.int32, a.shape, 1)
        lo = lane < group
        s_lo = jnp.sum(jnp.where(lo, sq, 0.0), axis=-1, keepdims=True)
        s_hi = jnp.sum(jnp.where(lo, 0.0, sq), axis=-1, keepdims=True)
        r = jnp.where(lo, lax.rsqrt(s_lo * (1.0 / group) + EPS),
                      lax.rsqrt(s_hi * (1.0 / group) + EPS))
    return a * r * gain


def _proj_kernel(*refs, modes, seg_of, w_trans, has_dt, rows_blocks):
    n_w = len(w_trans)
    x_ref, g_ref = refs[:2]
    w_refs = refs[2:2 + n_w]
    gain_ref = refs[2 + n_w]
    pos = 3 + n_w
    if has_dt:
        wdt_ref = refs[pos]
        pos += 1
    u_ref = refs[pos]
    pos += 1
    if has_dt:
        dt_ref = refs[pos]
        pos += 1
    rows_refs = refs[pos:pos + len(rows_blocks)]
    h_ref = refs[pos + len(rows_blocks)]
    j = pl.program_id(1)
    tm = x_ref.shape[0]

    @pl.when(j == 0)
    def _():
        x = x_ref[...]
        r = lax.rsqrt(jnp.mean(x * x, axis=-1, keepdims=True) + EPS)
        h_ref[...] = (x * r * g_ref[...]).astype(BF16)
        if has_dt:
            dt_ref[...] = _dot_nt(h_ref[...], wdt_ref[...])

    keys = {}
    for jj, m in enumerate(modes):
        ri = rows_blocks.index(jj) if jj in rows_blocks else None
        keys.setdefault((seg_of[jj], m, ri), []).append(jj)
    for (seg, mode, ri), jjs in keys.items():
        cond = j == jjs[0]
        for jj in jjs[1:]:
            cond = jnp.logical_or(cond, j == jj)

        @pl.when(cond)
        def _(seg=seg, mode=mode, ri=ri):
            w = w_refs[seg][...].astype(BF16)
            acc = _dot_nt(h_ref[...], w) if w_trans[seg] else _dot(h_ref[...], w)
            if mode == 0 and ri is None:
                u_ref[...] = acc
                return
            gain = gain_ref[...]
            for s in range(acc.shape[1] // LANES):
                sl = slice(s * LANES, (s + 1) * LANES)
                slab = acc[:, sl]
                if mode:
                    slab = _group_rmsnorm_slab(slab, gain[:, sl], mode)
                u_ref[:, sl] = slab
                if ri is not None:
                    rows_refs[ri][pl.ds(s, tm, stride=PROJ_TN // LANES), :] = slab


def _norm_proj(x, g, ws, gains, modes, w_dt=None, rows_blocks=(), tm=1024):
    m, k = x.shape
    tm = min(tm, m)
    nblk = [nb for _, nb, _ in ws]
    w_trans = tuple(t for _, _, t in ws)
    ws = [w for w, _, _ in ws]
    assert m % tm == 0 and sum(nblk) == len(modes)
    assert all(w.shape[0 if t else 1] >= nb * PROJ_TN for w, nb, t in zip(ws, nblk, w_trans))
    seg_of, offs = [], []
    for a, nb in enumerate(nblk):
        offs.append(len(seg_of))
        seg_of += [a] * nb
    n = PROJ_TN * len(modes)
    has_dt = w_dt is not None

    def w_spec(a):
        def blk(j):
            return jnp.minimum(jnp.maximum(j - offs[a], 0), nblk[a] - 1)
        if w_trans[a]:
            return pl.BlockSpec((PROJ_TN, k), lambda i, j: (blk(j), 0))
        return pl.BlockSpec((k, PROJ_TN), lambda i, j: (0, blk(j)))

    in_specs = [pl.BlockSpec((tm, k), lambda i, j: (i, 0)), pl.BlockSpec((1, k), lambda i, j: (0, 0))]
    in_specs += [w_spec(a) for a in range(len(ws))]
    in_specs.append(pl.BlockSpec((1, PROJ_TN), lambda i, j: (0, j)))
    args = [x, g, *ws, gains]
    out_shape = [jax.ShapeDtypeStruct((m, n), F32)]
    out_specs = [pl.BlockSpec((tm, PROJ_TN), lambda i, j: (i, j))]
    if has_dt:
        in_specs.append(pl.BlockSpec((LANES, k), lambda i, j: (0, 0)))
        args.append(w_dt)
        out_shape.append(jax.ShapeDtypeStruct((m, LANES), F32))
        out_specs.append(pl.BlockSpec((tm, LANES), lambda i, j: (i, 0)))
    heads = PROJ_TN // LANES
    for _ in rows_blocks:
        out_shape.append(jax.ShapeDtypeStruct((m * heads, LANES), F32))
        out_specs.append(pl.BlockSpec((tm * heads, LANES), lambda i, j: (i, 0)))
    return pl.pallas_call(
        functools.partial(_proj_kernel, modes=tuple(modes), seg_of=tuple(seg_of), w_trans=w_trans, has_dt=has_dt,
                          rows_blocks=tuple(rows_blocks)),
        grid=(m // tm, len(modes)),
        in_specs=in_specs,
        out_specs=out_specs,
        out_shape=out_shape,
        scratch_shapes=[pltpu.VMEM((tm, k), BF16)],
        compiler_params=_cparams(("parallel", "arbitrary")),
        name="norm_proj",
    )(*args)


def _split3(x):
    hi = x.astype(BF16)
    r1 = x - hi.astype(F32)
    mid = r1.astype(BF16)
    lo = (r1 - mid.astype(F32)).astype(BF16)
    return hi, mid, lo


def _ssd_kernel(*refs, t_in, valid_last, nc, has_init):
    if has_init:
        (xs_ref, z_ref, bc_ref, dt_ref, cprev_ref, h0_ref, convw_ref, convb_ref, dtb_ref, alog_ref,
         dskip_ref, gssm_ref, y_ref, cout_ref, hout_ref, xpad, hst) = refs
    else:
        (xs_ref, z_ref, bc_ref, dt_ref, convw_ref, convb_ref, dtb_ref, alog_ref,
         dskip_ref, gssm_ref, y_ref, cout_ref, hout_ref, xpad, hst) = refs
    T = SSD_T
    c = pl.program_id(1)
    single_step = nc == 1 and valid_last == 1

    @pl.when(c == 0)
    def _():
        if has_init:
            xpad[0:8, :] = cprev_ref[0]
            hst[...] = h0_ref[0]
        else:
            xpad[0:8, :] = jnp.zeros((8, CONV_DIM), F32)
            hst[...] = jnp.zeros(hst.shape, F32)

    def rows(ref):
        v = ref[...]
        if t_in < T:
            v = jnp.concatenate([v, jnp.zeros((T - t_in, v.shape[1]), F32)], axis=0)
        return v

    xpad[8:8 + T, 0:D_SSM] = rows(xs_ref)
    xpad[8:8 + T, D_SSM:CONV_DIM] = rows(bc_ref)
    z = rows(z_ref)
    dt_raw = rows(dt_ref)

    live = 8 if single_step else T
    convw = convw_ref[...]
    conv = convb_ref[...] + xpad[5:5 + live, :] * convw[0:1, :]
    for jtap in range(1, CONV_W):
        conv = conv + xpad[5 + jtap:5 + jtap + live, :] * convw[jtap:jtap + 1, :]
    xact = _silu(conv)
    z_gate = _silu(z[0:live, :])
    if live < T:
        xact = jnp.concatenate([xact, jnp.zeros((T - live, CONV_DIM), F32)], axis=0)
        z_gate = jnp.concatenate([z_gate, jnp.zeros((T - live, D_SSM), F32)], axis=0)

    row_i = lax.broadcasted_iota(jnp.int32, (T, LANES), 0)
    col_i = lax.broadcasted_iota(jnp.int32, (T, LANES), 1)
    xv = dt_raw + dtb_ref[...]
    dt = jnp.maximum(xv, 0.0) + jnp.log1p(jnp.exp(-jnp.abs(xv)))
    if valid_last < T:
        dt = jnp.where(row_i < valid_last, dt, 0.0)
    a_neg = -jnp.exp(alog_ref[...])
    adt = dt * a_neg

    tril = (row_i >= col_i)
    tril_bf = jnp.where(tril, 1.0, 0.0).astype(BF16)
    a_hi, a_mid, a_lo = _split3(adt)
    acum = _dot(tril_bf, a_hi) + _dot(tril_bf, a_mid) + _dot(tril_bf, a_lo)
    acum_t = acum.T
    dt_t = dt.T
    e_acum = jnp.exp(acum)
    a_last = acum[T - 1:T, :]
    w_state = jnp.exp(a_last - acum) * dt
    da_last = jnp.exp(a_last)

    lo_half = col_i < SSM_HEAD_DIM
    neg_big = jnp.float32(-1e30)

    def colb(tile, r):
        return jnp.broadcast_to(tile[:, r:r + 1], (T, LANES))

    y_slabs = []
    for g in range(SSM_GROUPS):
        b_g = xact[:, D_SSM + g * D_STATE:D_SSM + (g + 1) * D_STATE]
        c_g = xact[:, D_SSM + SSM_GROUPS * D_STATE + g * D_STATE:
                   D_SSM + SSM_GROUPS * D_STATE + (g + 1) * D_STATE]
        b_bf = b_g.astype(BF16)
        c_bf = c_g.astype(BF16)
        if single_step:
            cb00 = jnp.sum(c_g[0:1, :] * b_g[0:1, :], axis=-1, keepdims=True)
        else:
            cb = _dot_nt(c_bf, b_bf)
        for pp in range(SSM_HEADS // SSM_GROUPS // 2):
            p = g * (SSM_HEADS // SSM_GROUPS // 2) + pp
            sl = slice(p * LANES, (p + 1) * LANES)
            xs_slab = xact[:, sl]
            y_acc = dskip_ref[:, sl] * xs_slab
            if single_step:
                dt_row = jnp.where(lo_half[0:1, :], dt[0:1, 2 * p:2 * p + 1], dt[0:1, 2 * p + 1:2 * p + 2])
                y_acc = y_acc + (cb00 * dt_row) * xs_slab
            for hh in range(0 if single_step else 2):
                r = 2 * p + hh
                seg = colb(acum, r) - acum_t[r:r + 1, :]
                lmat = jnp.exp(jnp.where(tril, seg, neg_big))
                mr = (cb * lmat * dt_t[r:r + 1, :]).astype(BF16)
                xh = jnp.where(lo_half if hh == 0 else jnp.logical_not(lo_half), xs_slab, 0.0)
                y_acc = y_acc + _dot(mr, xh.astype(BF16))
            hpair = hst[sl, :]
            e_pair = jnp.where(lo_half, colb(e_acum, 2 * p), colb(e_acum, 2 * p + 1))
            y_acc = y_acc + e_pair * _dot_nt(c_bf, hpair.astype(BF16))
            w_pair = jnp.where(lo_half, colb(w_state, 2 * p), colb(w_state, 2 * p + 1))
            xw_t = (xs_slab * w_pair).T
            st = _dot(xw_t.astype(BF16), b_bf)
            da = jnp.concatenate(
                [jnp.broadcast_to(da_last[:, 2 * p:2 * p + 1], (SSM_HEAD_DIM, LANES)),
                 jnp.broadcast_to(da_last[:, 2 * p + 1:2 * p + 2], (SSM_HEAD_DIM, LANES))], axis=0)
            hst[sl, :] = da * hpair + st
            y_slabs.append((y_acc * z_gate[:, sl])[0:t_in, :])

    per_group = D_SSM // SSM_GROUPS // LANES
    for g in range(SSM_GROUPS):
        slabs = y_slabs[g * per_group:(g + 1) * per_group]
        ssum = jnp.sum(slabs[0] * slabs[0], axis=-1, keepdims=True)
        for s in slabs[1:]:
            ssum = ssum + jnp.sum(s * s, axis=-1, keepdims=True)
        r = lax.rsqrt(ssum * (1.0 / (per_group * LANES)) + EPS)
        for k, s in enumerate(slabs):
            sl = slice((g * per_group + k) * LANES, (g * per_group + k + 1) * LANES)
            y_ref[:, sl] = (s * r * gssm_ref[:, sl]).astype(y_ref.dtype)

    if nc > 1:
        xpad[5:8, :] = xpad[5 + T:8 + T, :]

    @pl.when(c == nc - 1)
    def _():
        cout_ref[...] = jnp.zeros(cout_ref.shape, F32)
        if nc > 1:
            cout_ref[0, 0:CONV_W - 1, :] = xpad[5:8, :]
        else:
            cout_ref[0, 0:CONV_W - 1, :] = xpad[5 + valid_last:8 + valid_last, :]
        hout_ref[0] = hst[...]


def _ssd(u, dt, n_batch, seq_rows, valid_len, ssm_p, conv_prev=None, h0=None):
    t_in = min(seq_rows, SSD_T)
    nc = max(seq_rows // SSD_T, 1)
    valid_last = valid_len - (nc - 1) * SSD_T
    has_init = conv_prev is not None
    convw, convb, dtb, alog, dskip, gssm = ssm_p

    def row(b, c):
        return b * nc + c

    in_specs = [
        pl.BlockSpec((t_in, D_SSM), lambda b, c: (row(b, c), COL_XS // D_SSM)),
        pl.BlockSpec((t_in, D_SSM), lambda b, c: (row(b, c), COL_Z // D_SSM)),
        pl.BlockSpec((t_in, BC_DIM), lambda b, c: (row(b, c), COL_BC // BC_DIM)),
        pl.BlockSpec((t_in, LANES), lambda b, c: (row(b, c), 0)),
    ]
    args = [u, u, u, dt]
    if has_init:
        in_specs += [pl.BlockSpec((1, 8, CONV_DIM), lambda b, c: (b, 0, 0)),
                     pl.BlockSpec((1, D_SSM, D_STATE), lambda b, c: (b, 0, 0))]
        args += [conv_prev, h0]
    in_specs += [
        pl.BlockSpec((8, CONV_DIM), lambda b, c: (0, 0)),
        pl.BlockSpec((1, CONV_DIM), lambda b, c: (0, 0)),
        pl.BlockSpec((1, LANES), lambda b, c: (0, 0)),
        pl.BlockSpec((1, LANES), lambda b, c: (0, 0)),
        pl.BlockSpec((1, D_SSM), lambda b, c: (0, 0)),
        pl.BlockSpec((1, D_SSM), lambda b, c: (0, 0)),
    ]
    args += [convw, convb, dtb, alog, dskip, gssm]
    return pl.pallas_call(
        functools.partial(_ssd_kernel, t_in=t_in, valid_last=valid_last, nc=nc, has_init=has_init),
        grid=(n_batch, nc),
        in_specs=in_specs,
        out_specs=[
            pl.BlockSpec((t_in, D_SSM), lambda b, c: (row(b, c), 0)),
            pl.BlockSpec((1, 8, CONV_DIM), lambda b, c: (b, 0, 0)),
            pl.BlockSpec((1, D_SSM, D_STATE), lambda b, c: (b, 0, 0)),
        ],
        out_shape=[
            jax.ShapeDtypeStruct((n_batch * seq_rows, D_SSM), BF16),
            jax.ShapeDtypeStruct((n_batch, 8, CONV_DIM), F32),
            jax.ShapeDtypeStruct((n_batch, D_SSM, D_STATE), F32),
        ],
        scratch_shapes=[pltpu.VMEM((8 + SSD_T, CONV_DIM), F32), pltpu.VMEM((D_SSM, D_STATE), F32)],
        compiler_params=_cparams(("parallel", "arbitrary")),
        name="ssd_scan",
    )(*args)


def _lambda_from(lam_ref, lam_init):
    lp = lam_ref[...]
    s1 = jnp.sum(lp[0:1, :] * lp[1:2, :], axis=-1, keepdims=True)
    s2 = jnp.sum(lp[2:3, :] * lp[3:4, :], axis=-1, keepdims=True)
    return jnp.exp(s1) - jnp.exp(s2) + lam_init


def _attn_prompt_kernel(q_ref, k_ref, v_ref, mq_ref, mk_ref, mv_ref, lam_ref, gsub_ref,
                        od_ref, om_ref, kb, vb, s_s, m_s, l_s, acc_s, *, tq, lam_init):
    qi = pl.program_id(2)

    @pl.when(qi == 0)
    def _():
        kb[...] = k_ref[...].astype(BF16)
        vb[...] = v_ref[...].astype(BF16)

    lane = lax.broadcasted_iota(jnp.int32, (tq, LANES), 1)
    lo = lane < DIFF_DK
    q = q_ref[...] * (DIFF_DK ** -0.5 * LOG2E)
    q2 = jnp.concatenate([jnp.where(lo, q, 0.0), jnp.where(lo, 0.0, q)], axis=0).astype(BF16)
    m_s[...] = jnp.full(m_s.shape, -jnp.inf, F32)
    l_s[...] = jnp.zeros(l_s.shape, F32)
    acc_s[...] = jnp.zeros(acc_s.shape, F32)
    reps = tq // LANES

    def scores(j, masked):
        start = pl.multiple_of(j * tq, tq)
        s = _dot_nt(q2, kb[pl.ds(start, tq), :])
        if masked:
            r_i = jnp.bitwise_and(lax.broadcasted_iota(jnp.int32, (2 * tq, tq), 0), tq - 1)
            c_i = lax.broadcasted_iota(jnp.int32, (2 * tq, tq), 1)
            s = jnp.where(r_i >= c_i, s, -jnp.inf)
        s_s[:, pl.ds(start, tq)] = s
        m_s[...] = jnp.maximum(m_s[...], jnp.max(s, axis=-1, keepdims=True))

    def weighted(j):
        start = pl.multiple_of(j * tq, tq)
        m_rep = jnp.concatenate([m_s[...]] * reps, axis=1)
        p = jnp.exp2(s_s[:, pl.ds(start, tq)] - m_rep)
        l_s[...] += jnp.sum(p, axis=-1, keepdims=True)
        acc_s[...] += _dot(p.astype(BF16), vb[pl.ds(start, tq), :])

    def body1(j, carry):
        scores(j, False)
        return carry

    def body2(j, carry):
        weighted(j)
        return carry

    lax.fori_loop(0, qi, body1, 0)
    scores(qi, True)
    lax.fori_loop(0, qi + 1, body2, 0)

    lam = _lambda_from(lam_ref, lam_init)
    o = acc_s[0:tq, :] / l_s[0:tq, :] - lam * (acc_s[tq:2 * tq, :] / l_s[tq:2 * tq, :])
    r = lax.rsqrt(jnp.mean(o * o, axis=-1, keepdims=True) + EPS)
    od_ref[...] = ((o * r * gsub_ref[...]) * (1.0 - lam_init)).astype(od_ref.dtype)

    s = _dot_nt(mq_ref[...].astype(BF16), mk_ref[...].astype(BF16)) * (MEM_HD ** -0.5 * LOG2E)
    e = jnp.exp2(s - jnp.max(s, axis=-1, keepdims=True))
    om = _dot(e.astype(BF16), mv_ref[...].astype(BF16)) / jnp.sum(e, axis=-1, keepdims=True)
    om_ref[...] = om.astype(om_ref.dtype)


def _attn_prompt(u, mkv, lam_pack, gsub, n_batch, seq, lam_init, tq=1024):
    nq = seq // tq
    cq, ck, cv, cmq = (COL_Q // LANES, COL_K // LANES, COL_V // LANES, COL_MQ // LANES)
    return pl.pallas_call(
        functools.partial(_attn_prompt_kernel, tq=tq, lam_init=lam_init),
        grid=(n_batch, DIFF_HEADS, nq),
        in_specs=[
            pl.BlockSpec((tq, LANES), lambda b, h, i: (b * nq + i, cq + h)),
            pl.BlockSpec((seq, LANES), lambda b, h, i: (b, ck + h)),
            pl.BlockSpec((seq, LANES), lambda b, h, i: (b, cv + h)),
            pl.BlockSpec((tq, LANES), lambda b, h, i: (b * nq + i, cmq + h)),
            pl.BlockSpec((MEM_TOKENS, LANES), lambda b, h, i: (b, h)),
            pl.BlockSpec((MEM_TOKENS, LANES), lambda b, h, i: (b, MEM_HEADS + h)),
            pl.BlockSpec((8, LANES), lambda b, h, i: (0, 0)),
            pl.BlockSpec((1, LANES), lambda b, h, i: (0, 0)),
        ],
        out_specs=[
            pl.BlockSpec((tq, LANES), lambda b, h, i: (b * nq + i, h)),
            pl.BlockSpec((tq, LANES), lambda b, h, i: (b * nq + i, h)),
        ],
        out_shape=[
            jax.ShapeDtypeStruct((n_batch * seq, D_DIFF), BF16),
            jax.ShapeDtypeStruct((n_batch * seq, D_MEMX), BF16),
        ],
        scratch_shapes=[
            pltpu.VMEM((seq, LANES), BF16),
            pltpu.VMEM((seq, LANES), BF16),
            pltpu.VMEM((2 * tq, seq), F32),
            pltpu.VMEM((2 * tq, LANES), F32),
            pltpu.VMEM((2 * tq, LANES), F32),
            pltpu.VMEM((2 * tq, LANES), F32),
        ],
        compiler_params=_cparams(("parallel", "parallel", "arbitrary")),
        name="attn_prompt",
    )(u, u, u, u, mkv, mkv, lam_pack, gsub)


PAGE = 128
PAGE_ROWS = PAGE * DIFF_HEADS


def _per_head_rows(x_row, split_maps):
    lane = lax.broadcasted_iota(jnp.int32, (1, LANES), 1)
    rows = []
    for r in range(2 * DIFF_HEADS):
        xh = x_row[:, (r // 2) * LANES:(r // 2 + 1) * LANES]
        if split_maps:
            xh = jnp.where((lane < DIFF_DK) if r % 2 == 0 else (lane >= DIFF_DK), xh, 0.0)
        rows.append(xh)
    return jnp.concatenate(rows, axis=0)


def _head_match(n_cols):
    r_i = lax.broadcasted_iota(jnp.int32, (8, n_cols), 0)
    c_i = lax.broadcasted_iota(jnp.int32, (8, n_cols), 1)
    return jnp.bitwise_and(c_i, DIFF_HEADS - 1) == jnp.right_shift(r_i, 1)


def _rider_specs(step_fn, spb):
    W = DIFF_HEADS * LANES
    assert spb & (spb - 1) == 0, "steps per sequence must be a power of two (shift/mask indexing)"
    shift = spb.bit_length() - 1

    def seq(idx):
        return jnp.right_shift(step_fn(*idx[:-1]), shift)

    tok_q = pl.BlockSpec((ROWS_S, W), lambda *idx: (seq(idx), COL_Q // W))
    hbm = pl.BlockSpec(memory_space=pl.ANY)
    part = pl.BlockSpec((8, LANES), lambda *idx: (seq(idx), 0))
    return [tok_q, hbm, hbm], [part, part, part]


def _rider_scratch(npp):
    buf = pltpu.VMEM((2, npp, PAGE_ROWS, LANES), F32)
    return [buf, buf, pltpu.SemaphoreType.DMA((2, 2))]


def _rider_pages(pt_ref, ck_hbm, cv_hbm, kbuf, vbuf, sem, *, spb, npp, page_lo, page_base):
    shift = spb.bit_length() - 1

    def copies(slot, rows):
        out = []
        for i in range(npp):
            out.append(pltpu.make_async_copy(ck_hbm.at[pl.ds(rows[i], PAGE_ROWS), :], kbuf.at[slot, i],
                                             sem.at[0, slot]))
            out.append(pltpu.make_async_copy(cv_hbm.at[pl.ds(rows[i], PAGE_ROWS), :], vbuf.at[slot, i],
                                             sem.at[1, slot]))
        return out

    def start(step, slot):
        seq = jnp.right_shift(step, shift)
        col = page_lo + jnp.bitwise_and(step, spb - 1) * npp
        rows = [pl.multiple_of((page_base + pt_ref[seq, col + i]) * PAGE_ROWS, PAGE_ROWS) for i in range(npp)]
        for c in copies(slot, rows):
            c.start()

    def wait(slot):
        for c in copies(slot, [0] * npp):
            c.wait()

    return start, wait


def _rider_parts(step, spb, tok_q, k_refs, v_refs, pm_ref, pl_ref, pa_ref, m_s, l_s, acc_s):
    npp = len(k_refs)
    j = jnp.bitwise_and(step, spb - 1)

    def first():
        m_s[...] = jnp.full(m_s.shape, -jnp.inf, F32)
        l_s[...] = jnp.zeros(l_s.shape, F32)
        acc_s[...] = jnp.zeros(acc_s.shape, F32)

    def main():
        q = _per_head_rows(tok_q[0:1, :] * (DIFF_DK ** -0.5), True).astype(BF16)
        s_all = jnp.concatenate([_dot_nt(q, k_refs[i][...].astype(BF16)) for i in range(npp)], axis=1)
        s_all = jnp.where(_head_match(npp * PAGE_ROWS), s_all, -jnp.inf)
        m_prev = m_s[...]
        m_new = jnp.maximum(m_prev, jnp.max(s_all, axis=-1, keepdims=True))
        alpha = jnp.exp(m_prev - m_new)
        p = jnp.exp(s_all - m_new[:, 0:1])
        l_s[...] = alpha * l_s[...] + jnp.sum(p, axis=-1, keepdims=True)
        m_s[...] = m_new
        p_bf = p.astype(BF16)
        pv = _dot(p_bf[:, 0:PAGE_ROWS], v_refs[0][...].astype(BF16))
        for i in range(1, npp):
            pv = pv + _dot(p_bf[:, i * PAGE_ROWS:(i + 1) * PAGE_ROWS], v_refs[i][...].astype(BF16))
        acc_s[...] = alpha * acc_s[...] + pv

    def last():
        pm_ref[...] = m_s[...]
        pl_ref[...] = l_s[...]
        pa_ref[...] = acc_s[...]

    return j, first, main, last


def _decode_finish_kernel(tok_q, tok_k, tok_v, tok_mq, mk_ref, mv_ref, lam_ref, gsub_ref, *rest, n_parts, grp,
                          lam_init):
    parts = rest[:3 * n_parts]
    od_ref, om_ref = rest[3 * n_parts:]
    mem_rows = MEM_TOKENS * MEM_HEADS
    lam = _lambda_from(lam_ref, lam_init)
    for g in range(grp):
        t0 = g * ROWS_S
        r8 = slice(g * 8, (g + 1) * 8)
        qmat = _per_head_rows(tok_q[t0:t0 + 1, :] * (DIFF_DK ** -0.5), True)
        s_new = jnp.sum(qmat * _per_head_rows(tok_k[t0:t0 + 1, :], False), axis=-1, keepdims=True)
        m_tot = jnp.broadcast_to(s_new, (8, LANES))
        for h in range(n_parts):
            m_tot = jnp.maximum(m_tot, parts[3 * h][r8, :])
        w_new = jnp.exp(s_new - m_tot)
        l_tot = w_new
        acc = w_new * _per_head_rows(tok_v[t0:t0 + 1, :], False)
        for h in range(n_parts):
            w = jnp.exp(parts[3 * h][r8, :] - m_tot)
            l_tot = l_tot + w * parts[3 * h + 1][r8, :]
            acc = acc + w * parts[3 * h + 2][r8, :]
        o_all = acc / l_tot
        outs = []
        for h in range(DIFF_HEADS):
            o = o_all[2 * h:2 * h + 1, :] - lam * o_all[2 * h + 1:2 * h + 2, :]
            r = lax.rsqrt(jnp.mean(o * o, axis=-1, keepdims=True) + EPS)
            outs.append((o * r * gsub_ref[...]) * (1.0 - lam_init))
        od_ref[r8, :] = jnp.broadcast_to(jnp.concatenate(outs, axis=1), (8, DIFF_HEADS * LANES))

        mrows = slice(g * mem_rows, (g + 1) * mem_rows)
        mqm = _per_head_rows(tok_mq[t0:t0 + 1, :], False)
        s = _dot_nt(mqm.astype(BF16), mk_ref[mrows, :].astype(BF16)) * (MEM_HD ** -0.5)
        s = jnp.where(_head_match(s.shape[1]), s, -jnp.inf)
        e = jnp.exp(s - jnp.max(s, axis=-1, keepdims=True))
        om = _dot(e.astype(BF16), mv_ref[mrows, :].astype(BF16)) / jnp.sum(e, axis=-1, keepdims=True)
        om_row = jnp.concatenate([om[2 * h:2 * h + 1, :] for h in range(MEM_HEADS)], axis=1)
        om_ref[r8, :] = jnp.broadcast_to(om_row, (8, MEM_HEADS * LANES))


def _decode_finish(u_s, parts, mem_k, mem_v, lam_pack, gsub, layer, n_b, lam_init, grp=4):
    W = DIFF_HEADS * LANES
    mem_rows = MEM_TOKENS * MEM_HEADS
    flat = [a for tri in parts for a in tri]
    assert n_b % grp == 0
    nblk = n_b // grp

    def tok(col):
        return pl.BlockSpec((grp * ROWS_S, W), lambda b: (b, col // W))

    od, om = pl.pallas_call(
        functools.partial(_decode_finish_kernel, n_parts=len(parts), grp=grp, lam_init=lam_init),
        grid=(nblk,),
        in_specs=[tok(COL_Q), tok(COL_K), tok(COL_V), tok(COL_MQ),
                  pl.BlockSpec((grp * mem_rows, LANES), lambda b: (layer * nblk + b, 0)),
                  pl.BlockSpec((grp * mem_rows, LANES), lambda b: (layer * nblk + b, 0)),
                  pl.BlockSpec((8, LANES), lambda b: (0, 0)),
                  pl.BlockSpec((1, LANES), lambda b: (0, 0))]
                 + [pl.BlockSpec((grp * 8, LANES), lambda b: (b, 0))] * len(flat),
        out_specs=[pl.BlockSpec((grp * 8, W), lambda b: (b, 0)), pl.BlockSpec((grp * 8, W), lambda b: (b, 0))],
        out_shape=[jax.ShapeDtypeStruct((n_b * 8, W), F32), jax.ShapeDtypeStruct((n_b * 8, W), F32)],
        compiler_params=_cparams(("parallel",)),
        name="decode_finish",
    )(u_s, u_s, u_s, u_s, mem_k, mem_v, lam_pack, gsub, *flat)
    return od.reshape(n_b, 8, W)[:, 0], om.reshape(n_b, 8, W)[:, 0]


def _outproj_kernel(x_ref, y_ref, od_ref, om_ref, w1_ref, w2_ref, w3_ref, o_ref):
    o_ref[...] = (x_ref[...]
                  + _dot(y_ref[...].astype(BF16), w1_ref[...])
                  + _dot(od_ref[...].astype(BF16), w2_ref[...])
                  + _dot(om_ref[...].astype(BF16), w3_ref[...]))


def _outproj(x, y, od, om, w_out, tm=512, tn=D_MODEL):
    m = x.shape[0]
    tm = min(tm, m)
    assert m % tm == 0
    return pl.pallas_call(
        _outproj_kernel,
        grid=(m // tm, D_MODEL // tn),
        in_specs=[
            pl.BlockSpec((tm, tn), lambda i, j: (i, j)),
            pl.BlockSpec((tm, D_SSM), lambda i, j: (i, 0)),
            pl.BlockSpec((tm, D_DIFF), lambda i, j: (i, 0)),
            pl.BlockSpec((tm, D_MEMX), lambda i, j: (i, 0)),
            pl.BlockSpec((D_SSM, tn), lambda i, j: (0, j)),
            pl.BlockSpec((D_DIFF, tn), lambda i, j: (D_SSM // D_DIFF, j)),
            pl.BlockSpec((D_MEMX, tn), lambda i, j: ((D_SSM + D_DIFF) // D_MEMX, j)),
        ],
        out_specs=pl.BlockSpec((tm, tn), lambda i, j: (i, j)),
        out_shape=jax.ShapeDtypeStruct((m, D_MODEL), F32),
        compiler_params=_cparams(("parallel", "arbitrary")),
        name="out_proj",
    )(x, y, od, om, w_out, w_out, w_out)


def _mlp_kernel(*refs, rider):
    if rider:
        npp, spb = rider["npp"], rider["spb"]
        pt_ref, x_ref, g_ref, wu_ref, wd_ref, tok_q, ck_hbm, cv_hbm = refs[:8]
        o_ref, pm_ref, pl_ref, pa_ref, h_ref, m_s, l_s, acc_s, kbuf, vbuf, sem = refs[8:]
    else:
        x_ref, g_ref, wu_ref, wd_ref, o_ref, h_ref = refs
    f = pl.program_id(1)

    @pl.when(f == 0)
    def _():
        x = x_ref[...]
        r = lax.rsqrt(jnp.mean(x * x, axis=-1, keepdims=True) + EPS)
        h_ref[...] = (x * r * g_ref[...]).astype(BF16)
        o_ref[...] = x

    if not rider:
        a = jnp.maximum(_dot(h_ref[...], wu_ref[...]), 0.0)
        o_ref[...] += _dot((a * a).astype(BF16), wd_ref[...])
        return

    rounds = rider["rounds"]
    n_steps = pl.num_programs(0) * pl.num_programs(1)
    n_rounds = n_steps * rounds
    step = pl.program_id(0) * pl.num_programs(1) + f
    start, wait = _rider_pages(pt_ref, ck_hbm, cv_hbm, kbuf, vbuf, sem, spb=spb, npp=npp,
                               page_lo=rider["page_lo"], page_base=rider["page_base"])
    ts = [step * rounds + r for r in range(rounds)]
    slots = [jnp.bitwise_and(t, 1) for t in ts]
    parts = []
    for t, slot in zip(ts, slots):
        k_refs = [kbuf.at[slot, i] for i in range(npp)]
        v_refs = [vbuf.at[slot, i] for i in range(npp)]
        parts.append(_rider_parts(t, spb, tok_q, k_refs, v_refs, pm_ref, pl_ref, pa_ref, m_s, l_s, acc_s))

    def run_round(r):
        wait(slots[r])
        parts[r][2]()

    def next_round(t):
        return jnp.where(t + 1 == n_rounds, 0, t + 1)

    pl.when(step == 0)(lambda: start(ts[0], slots[0]))
    pl.when(parts[0][0] == 0)(parts[0][1])
    start(next_round(ts[0]), 1 - slots[0])
    a = jnp.maximum(_dot(h_ref[...], wu_ref[...]), 0.0)
    if rounds == 2:
        run_round(0)
        start(next_round(ts[1]), slots[0])
    o_ref[...] += _dot((a * a).astype(BF16), wd_ref[...])
    run_round(rounds - 1)
    pl.when(parts[-1][0] == spb - 1)(parts[-1][3])
    pl.when(step == n_steps - 1)(lambda: wait(1 - slots[-1]))


def _mlp(x, g, w_up, w_down, tm=512, tf=1024, rider=None):
    m = x.shape[0]
    d_ff = w_up.shape[1]
    tm = min(tm, m)
    assert m % tm == 0 and d_ff % tf == 0
    grid = (m // tm, d_ff // tf)
    in_specs = [
        pl.BlockSpec((tm, D_MODEL), lambda i, f: (i, 0)),
        pl.BlockSpec((1, D_MODEL), lambda i, f: (0, 0)),
        pl.BlockSpec((D_MODEL, tf), lambda i, f: (0, f)),
        pl.BlockSpec((tf, D_MODEL), lambda i, f: (f, 0)),
    ]
    out_specs = [pl.BlockSpec((tm, D_MODEL), lambda i, f: (i, 0))]
    out_shape = [jax.ShapeDtypeStruct((m, D_MODEL), F32)]
    scratch = [pltpu.VMEM((tm, D_MODEL), BF16)]
    args = [x, g, w_up, w_down]
    if rider is None:
        out = pl.pallas_call(
            functools.partial(_mlp_kernel, rider=None),
            grid=grid, in_specs=in_specs, out_specs=out_specs, out_shape=out_shape, scratch_shapes=scratch,
            compiler_params=_cparams(("parallel", "arbitrary")), name="mlp",
        )(*args)
        return out[0]
    u_s, cache_k, cache_v, page_table, layer, n_pool, npp, page_lo, n_pages = rider
    n_b = page_table.shape[0]
    spb = n_pages // npp
    n_steps = grid[0] * grid[1]
    rounds = n_b * spb // n_steps
    assert n_pages % npp == 0 and rounds in (1, 2) and n_steps * rounds == n_b * spb and spb % rounds == 0
    nf = grid[1]
    r_in, r_out = _rider_specs(lambda i, f: i * nf + f, spb // rounds)
    host_in = [pl.BlockSpec(s.block_shape, lambda i, f, pt, im=s.index_map: im(i, f)) for s in in_specs]
    host_in[0] = pl.BlockSpec(host_in[0].block_shape, host_in[0].index_map, pipeline_mode=pl.Buffered(1))
    host_out = [pl.BlockSpec(s.block_shape, lambda i, f, pt, im=s.index_map: im(i, f)) for s in out_specs]
    grid_spec = pltpu.PrefetchScalarGridSpec(
        num_scalar_prefetch=1, grid=grid,
        in_specs=host_in + r_in, out_specs=host_out + r_out,
        scratch_shapes=scratch + [pltpu.VMEM((8, LANES), F32)] * 3 + _rider_scratch(npp),
    )
    part_shape = jax.ShapeDtypeStruct((n_b * 8, LANES), F32)
    rider_cfg = dict(npp=npp, spb=spb, rounds=rounds, page_lo=page_lo, page_base=layer * n_pool)
    out, pm, pl_, pa = pl.pallas_call(
        functools.partial(_mlp_kernel, rider=rider_cfg),
        grid_spec=grid_spec,
        out_shape=out_shape + [part_shape] * 3,
        compiler_params=_cparams(("arbitrary", "arbitrary")),
        name="mlp_rider",
    )(page_table, *args, u_s, cache_k, cache_v)
    return out, (pm, pl_, pa)


def _pad_lanes(v, width=LANES):
    v = v.reshape(1, -1).astype(F32)
    return jnp.pad(v, ((0, 0), (0, width - v.shape[1])))


def kernel(x_prompt, x_sample, mem_prompt, cache_diff_k, cache_diff_v, cache_mem_k, cache_mem_v, state_conv, state_ssm, page_table, norm_mix, w_in, conv_w, conv_b, dt_bias, a_log, d_skip, g_ssm, g_q, g_k, lambda_q1, lambda_k1, lambda_q2, lambda_k2, g_subln, norm_mem, w_mem_kv, g_mq, g_mk, w_out, norm_mlp, w_up, w_down):
    depth = w_in.shape[0]
    bp, seq, _ = x_prompt.shape
    bd, dec_seq, _ = x_sample.shape
    assert dec_seq == 1 and seq % SSD_T == 0
    n_pool = cache_diff_k.shape[1]

    xp = x_prompt.reshape(bp * seq, D_MODEL)
    xs = x_sample.reshape(bd, D_MODEL)
    cache_k = cache_diff_k.reshape(depth * n_pool * PAGE_ROWS, LANES)
    cache_v = cache_diff_v.reshape(depth * n_pool * PAGE_ROWS, LANES)
    mem_k = cache_mem_k.reshape(depth * bd * MEM_TOKENS * MEM_HEADS, MEM_HD)
    mem_v = cache_mem_v.reshape(depth * bd * MEM_TOKENS * MEM_HEADS, MEM_HD)

    outs = [[] for _ in range(10)]
    for l in range(depth):
        lam_init = 0.8 - 0.6 * math.exp(-0.3 * l)
        wl_t = jnp.swapaxes(w_in[l], 0, 1).astype(BF16)
        o_dt = D_SSM + CONV_DIM
        o_q = o_dt + SSM_HEADS
        w_main = [(wl_t, o_dt // PROJ_TN, True), (wl_t[o_q:], (N_MAIN - o_dt) // PROJ_TN, True)]
        w_dt = jnp.pad(wl_t[o_dt:o_q], ((0, LANES - SSM_HEADS), (0, 0)))
        ones = jnp.ones((PROJ_TN,), F32)
        gains = jnp.concatenate(
            [ones] * 5 + [jnp.tile(g_q[l], 2 * DIFF_HEADS), jnp.tile(g_k[l], 2 * DIFF_HEADS), ones,
                          jnp.tile(g_mq[l], MEM_HEADS)]).reshape(1, N_MAIN)
        modes = (0, 0, 0, 0, 0, DIFF_DK, DIFF_DK, 0, MEM_HD)
        g_mix = norm_mix[l].reshape(1, D_MODEL)

        ssm_p = (jnp.pad(conv_w[l], ((0, 8 - CONV_W), (0, 0))),
                 conv_b[l].reshape(1, CONV_DIM),
                 _pad_lanes(dt_bias[l]), _pad_lanes(a_log[l]),
                 jnp.repeat(d_skip[l], SSM_HEAD_DIM).reshape(1, D_SSM),
                 g_ssm[l].reshape(1, D_SSM))
        lam_pack = jnp.concatenate(
            [_pad_lanes(lambda_q1[l]), _pad_lanes(lambda_k1[l]), _pad_lanes(lambda_q2[l]),
             _pad_lanes(lambda_k2[l]), jnp.zeros((4, LANES), F32)], axis=0)
        gsub = g_subln[l].reshape(1, LANES)
        w_out_bf = w_out[l].astype(BF16)
        w_up_bf = w_up[l].astype(BF16)
        w_down_bf = w_down[l].astype(BF16)
        g_mlp = norm_mlp[l].reshape(1, D_MODEL)

        xs_pad = jnp.pad(xs.reshape(bd, 1, D_MODEL), ((0, 0), (0, ROWS_S - 1), (0, 0))).reshape(bd * ROWS_S, D_MODEL)
        u_s, dt_s = _norm_proj(xs_pad, g_mix, w_main, gains, modes, w_dt=w_dt)

        u_p, dt_p, k_rows, v_rows = _norm_proj(xp, g_mix, w_main, gains, modes, w_dt=w_dt,
                                               rows_blocks=(COL_K // PROJ_TN, COL_V // PROJ_TN))
        y_p, conv_p, h_p = _ssd(u_p, dt_p, bp, seq, seq, ssm_p)
        mem_gains = jnp.concatenate([jnp.tile(g_mk[l], MEM_HEADS), ones]).reshape(1, 2 * D_MEMX)
        mkv, = _norm_proj(mem_prompt.reshape(bp * MEM_TOKENS, D_MODEL), norm_mem[l].reshape(1, D_MODEL),
                          [(w_mem_kv[l], 2 * D_MEMX // PROJ_TN, False)], mem_gains, (MEM_HD, 0))
        od_p, om_p = _attn_prompt(u_p, mkv, lam_pack, gsub, bp, seq, lam_init)
        xp_mid = _outproj(xp, y_p, od_p, om_p, w_out_bf)
        n_pages = page_table.shape[1]
        xp_new, part = _mlp(xp_mid, g_mlp, w_up_bf, w_down_bf, tm=1024, tf=512,
                            rider=(u_s, cache_k, cache_v, page_table, l, n_pool, 16, 0, n_pages))

        outs[0].append(k_rows.reshape(bp, seq, DIFF_HEADS, LANES))
        outs[1].append(v_rows.reshape(bp, seq, DIFF_HEADS, LANES))
        outs[4].append(mkv[:, :D_MEMX].reshape(bp, MEM_TOKENS, MEM_HEADS, MEM_HD))
        outs[5].append(mkv[:, D_MEMX:].reshape(bp, MEM_TOKENS, MEM_HEADS, MEM_HD))
        outs[6].append(conv_p[:, :CONV_W - 1])
        outs[7].append(h_p.reshape(bp, SSM_HEADS, SSM_HEAD_DIM, D_STATE))

        conv_prev = jnp.pad(state_conv[l], ((0, 0), (8 - (CONV_W - 1), 0), (0, 0)))
        y_s, conv_s, h_s = _ssd(u_s, dt_s, bd, ROWS_S, 1, ssm_p, conv_prev=conv_prev,
                                h0=state_ssm[l].reshape(bd, D_SSM, D_STATE))
        od_s, om_s = _decode_finish(u_s, [part], mem_k, mem_v, lam_pack, gsub, l, bd, lam_init)
        y_s0 = y_s.reshape(bd, ROWS_S, D_SSM)[:, 0]
        xs_mid = _outproj(xs, y_s0, od_s, om_s, w_out_bf)
        xs_new = _mlp(xs_mid, g_mlp, w_up_bf, w_down_bf)
        u_s0 = u_s.reshape(bd, ROWS_S, N_MAIN)[:, 0]
        outs[2].append(u_s0[:, COL_K:COL_K + D_DIFF].reshape(bd, 1, DIFF_HEADS, LANES))
        outs[3].append(u_s0[:, COL_V:COL_V + D_DIFF].reshape(bd, 1, DIFF_HEADS, LANES))
        outs[8].append(conv_s[:, :CONV_W - 1])
        outs[9].append(h_s.reshape(bd, SSM_HEADS, SSM_HEAD_DIM, D_STATE))

        xp, xs = xp_new, xs_new

    st = [jnp.stack(o) for o in outs]
    return (xp.reshape(bp, seq, D_MODEL), xs.reshape(bd, 1, D_MODEL),
            st[0], st[1], st[2], st[3], st[4], st[5], st[6], st[7], st[8], st[9])
```

```python
import functools
import math

import jax
import jax.numpy as jnp
from jax import lax
from jax.experimental import pallas as pl
from jax.experimental.pallas import tpu as pltpu

F32 = jnp.float32
BF16 = jnp.bfloat16

D_MODEL = 2048
D_SSM = 1024
D_DIFF = 512
D_MEMX = 512
SSM_HEADS = 16
SSM_HEAD_DIM = 64
SSM_GROUPS = 2
D_STATE = 128
CONV_W = 4
BC_DIM = 2 * SSM_GROUPS * D_STATE
CONV_DIM = D_SSM + BC_DIM
DIFF_HEADS = 4
DIFF_DK = 64
MEM_HEADS = 4
MEM_HD = 128
MEM_TOKENS = 256
D_FF = 4 * D_MODEL
EPS = 1e-6
LOG2E = math.log2(math.e)

LANES = 128
SUBLANES_BF16 = 16
VMEM_LIMIT_BYTES = 58 * 1024 * 1024

N_MAIN = D_SSM + D_SSM + BC_DIM + 3 * D_DIFF + D_MEMX
COL_Z, COL_XS, COL_BC, COL_Q, COL_K, COL_V, COL_MQ = 0, 1024, 2048, 2560, 3072, 3584, 4096
PROJ_TN = 512
SSD_T = 128
ROWS_S = SUBLANES_BF16


def _cparams(sem):
    return pltpu.CompilerParams(dimension_semantics=sem, vmem_limit_bytes=VMEM_LIMIT_BYTES)


def _dot(a, b):
    return jnp.dot(a, b, preferred_element_type=F32)


def _dot_nt(a, b):
    return lax.dot_general(a, b, (((1,), (1,)), ((), ())), preferred_element_type=F32)


def _silu(x):
    return x * (1.0 / (1.0 + jnp.exp(-x)))


def _group_rmsnorm_slab(a, gain, group):
    sq = a * a
    if group == LANES:
        r = lax.rsqrt(jnp.sum(sq, axis=-1, keepdims=True) * (1.0 / LANES) + EPS)
    else:
        lane = lax.broadcasted_iota(jnp.int32, a.shape, 1)
        lo = lane < group
        s_lo = jnp.sum(jnp.where(lo, sq, 0.0), axis=-1, keepdims=True)
        s_hi = jnp.sum(jnp.where(lo, 0.0, sq), axis=-1, keepdims=True)
        r = jnp.where(lo, lax.rsqrt(s_lo * (1.0 / group) + EPS),
                      lax.rsqrt(s_hi * (1.0 / group) + EPS))
    return a * r * gain


def _proj_kernel(*refs, modes, seg_of, w_trans, has_dt, rows_blocks):
    n_w = len(w_trans)
    x_ref, g_ref = refs[:2]
    w_refs = refs[2:2 + n_w]
    gain_ref = refs[2 + n_w]
    pos = 3 + n_w
    if has_dt:
        wdt_ref = refs[pos]
        pos += 1
    u_ref = refs[pos]
    pos += 1
    if has_dt:
        dt_ref = refs[pos]
        pos += 1
    rows_refs = refs[pos:pos + len(rows_blocks)]
    h_ref = refs[pos + len(rows_blocks)]
    j = pl.program_id(1)
    tm = x_ref.shape[0]

    @pl.when(j == 0)
    def _():
        x = x_ref[...]
        r = lax.rsqrt(jnp.mean(x * x, axis=-1, keepdims=True) + EPS)
        h_ref[...] = (x * r * g_ref[...]).astype(BF16)
        if has_dt:
            dt_ref[...] = _dot_nt(h_ref[...], wdt_ref[...])

    keys = {}
    for jj, m in enumerate(modes):
        ri = rows_blocks.index(jj) if jj in rows_blocks else None
        keys.setdefault((seg_of[jj], m, ri), []).append(jj)
    for (seg, mode, ri), jjs in keys.items():
        cond = j == jjs[0]
        for jj in jjs[1:]:
            cond = jnp.logical_or(cond, j == jj)

        @pl.when(cond)
        def _(seg=seg, mode=mode, ri=ri):
            w = w_refs[seg][...].astype(BF16)
            acc = _dot_nt(h_ref[...], w) if w_trans[seg] else _dot(h_ref[...], w)
            if mode == 0 and ri is None:
                u_ref[...] = acc
                return
            gain = gain_ref[...]
            for s in range(acc.shape[1] // LANES):
                sl = slice(s * LANES, (s + 1) * LANES)
                slab = acc[:, sl]
                if mode:
                    slab = _group_rmsnorm_slab(slab, gain[:, sl], mode)
                u_ref[:, sl] = slab
                if ri is not None:
                    rows_refs[ri][pl.ds(s, tm, stride=PROJ_TN // LANES), :] = slab


def _norm_proj(x, g, ws, gains, modes, w_dt=None, rows_blocks=(), tm=1024):
    m, k = x.shape
    tm = min(tm, m)
    nblk = [nb for _, nb, _ in ws]
    w_trans = tuple(t for _, _, t in ws)
    ws = [w for w, _, _ in ws]
    assert m % tm == 0 and sum(nblk) == len(modes)
    assert all(w.shape[0 if t else 1] >= nb * PROJ_TN for w, nb, t in zip(ws, nblk, w_trans))
    seg_of, offs = [], []
    for a, nb in enumerate(nblk):
        offs.append(len(seg_of))
        seg_of += [a] * nb
    n = PROJ_TN * len(modes)
    has_dt = w_dt is not None

    def w_spec(a):
        def blk(j):
            return jnp.minimum(jnp.maximum(j - offs[a], 0), nblk[a] - 1)
        if w_trans[a]:
            return pl.BlockSpec((PROJ_TN, k), lambda i, j: (blk(j), 0))
        return pl.BlockSpec((k, PROJ_TN), lambda i, j: (0, blk(j)))

    in_specs = [pl.BlockSpec((tm, k), lambda i, j: (i, 0)), pl.BlockSpec((1, k), lambda i, j: (0, 0))]
    in_specs += [w_spec(a) for a in range(len(ws))]
    in_specs.append(pl.BlockSpec((1, PROJ_TN), lambda i, j: (0, j)))
    args = [x, g, *ws, gains]
    out_shape = [jax.ShapeDtypeStruct((m, n), F32)]
    out_specs = [pl.BlockSpec((tm, PROJ_TN), lambda i, j: (i, j))]
    if has_dt:
        in_specs.append(pl.BlockSpec((LANES, k), lambda i, j: (0, 0)))
        args.append(w_dt)
        out_shape.append(jax.ShapeDtypeStruct((m, LANES), F32))
        out_specs.append(pl.BlockSpec((tm, LANES), lambda i, j: (i, 0)))
    heads = PROJ_TN // LANES
    for _ in rows_blocks:
        out_shape.append(jax.ShapeDtypeStruct((m * heads, LANES), F32))
        out_specs.append(pl.BlockSpec((tm * heads, LANES), lambda i, j: (i, 0)))
    return pl.pallas_call(
        functools.partial(_proj_kernel, modes=tuple(modes), seg_of=tuple(seg_of), w_trans=w_trans, has_dt=has_dt,
                          rows_blocks=tuple(rows_blocks)),
        grid=(m // tm, len(modes)),
        in_specs=in_specs,
        out_specs=out_specs,
        out_shape=out_shape,
        scratch_shapes=[pltpu.VMEM((tm, k), BF16)],
        compiler_params=_cparams(("parallel", "arbitrary")),
        name="norm_proj",
    )(*args)


def _split3(x):
    hi = x.astype(BF16)
    r1 = x - hi.astype(F32)
    mid = r1.astype(BF16)
    lo = (r1 - mid.astype(F32)).astype(BF16)
    return hi, mid, lo


def _ssd_kernel(*refs, t_in, valid_last, nc, has_init):
    if has_init:
        (xs_ref, z_ref, bc_ref, dt_ref, cprev_ref, h0_ref, convw_ref, convb_ref, dtb_ref, alog_ref,
         dskip_ref, gssm_ref, y_ref, cout_ref, hout_ref, xpad, hst) = refs
    else:
        (xs_ref, z_ref, bc_ref, dt_ref, convw_ref, convb_ref, dtb_ref, alog_ref,
         dskip_ref, gssm_ref, y_ref, cout_ref, hout_ref, xpad, hst) = refs
    T = SSD_T
    c = pl.program_id(1)
    single_step = nc == 1 and valid_last == 1

    @pl.when(c == 0)
    def _():
        if has_init:
            xpad[0:8, :] = cprev_ref[0]
            hst[...] = h0_ref[0]
        else:
            xpad[0:8, :] = jnp.zeros((8, CONV_DIM), F32)
            hst[...] = jnp.zeros(hst.shape, F32)

    def rows(ref):
        v = ref[...]
        if t_in < T:
            v = jnp.concatenate([v, jnp.zeros((T - t_in, v.shape[1]), F32)], axis=0)
        return v

    xpad[8:8 + T, 0:D_SSM] = rows(xs_ref)
    xpad[8:8 + T, D_SSM:CONV_DIM] = rows(bc_ref)
    z = rows(z_ref)
    dt_raw = rows(dt_ref)

    live = 8 if single_step else T
    convw = convw_ref[...]
    conv = convb_ref[...] + xpad[5:5 + live, :] * convw[0:1, :]
    for jtap in range(1, CONV_W):
        conv = conv + xpad[5 + jtap:5 + jtap + live, :] * convw[jtap:jtap + 1, :]
    xact = _silu(conv)
    z_gate = _silu(z[0:live, :])
    if live < T:
        xact = jnp.concatenate([xact, jnp.zeros((T - live, CONV_DIM), F32)], axis=0)
        z_gate = jnp.concatenate([z_gate, jnp.zeros((T - live, D_SSM), F32)], axis=0)

    row_i = lax.broadcasted_iota(jnp.int32, (T, LANES), 0)
    col_i = lax.broadcasted_iota(jnp.int32, (T, LANES), 1)
    xv = dt_raw + dtb_ref[...]
    dt = jnp.maximum(xv, 0.0) + jnp.log1p(jnp.exp(-jnp.abs(xv)))
    if valid_last < T:
        dt = jnp.where(row_i < valid_last, dt, 0.0)
    a_neg = -jnp.exp(alog_ref[...])
    adt = dt * a_neg

    tril = (row_i >= col_i)
    tril_bf = jnp.where(tril, 1.0, 0.0).astype(BF16)
    a_hi, a_mid, a_lo = _split3(adt)
    acum = _dot(tril_bf, a_hi) + _dot(tril_bf, a_mid) + _dot(tril_bf, a_lo)
    acum_t = acum.T
    dt_t = dt.T
    e_acum = jnp.exp(acum)
    a_last = acum[T - 1:T, :]
    w_state = jnp.exp(a_last - acum) * dt
    da_last = jnp.exp(a_last)

    lo_half = col_i < SSM_HEAD_DIM
    neg_big = jnp.float32(-1e30)

    def colb(tile, r):
        return jnp.broadcast_to(tile[:, r:r + 1], (T, LANES))

    y_slabs = []
    for g in range(SSM_GROUPS):
        b_g = xact[:, D_SSM + g * D_STATE:D_SSM + (g + 1) * D_STATE]
        c_g = xact[:, D_SSM + SSM_GROUPS * D_STATE + g * D_STATE:
                   D_SSM + SSM_GROUPS * D_STATE + (g + 1) * D_STATE]
        b_bf = b_g.astype(BF16)
        c_bf = c_g.astype(BF16)
        if single_step:
            cb00 = jnp.sum(c_g[0:1, :] * b_g[0:1, :], axis=-1, keepdims=True)
        else:
            cb = _dot_nt(c_bf, b_bf)
        for pp in range(SSM_HEADS // SSM_GROUPS // 2):
            p = g * (SSM_HEADS // SSM_GROUPS // 2) + pp
            sl = slice(p * LANES, (p + 1) * LANES)
            xs_slab = xact[:, sl]
            y_acc = dskip_ref[:, sl] * xs_slab
            if single_step:
                dt_row = jnp.where(lo_half[0:1, :], dt[0:1, 2 * p:2 * p + 1], dt[0:1, 2 * p + 1:2 * p + 2])
                y_acc = y_acc + (cb00 * dt_row) * xs_slab
            for hh in range(0 if single_step else 2):
                r = 2 * p + hh
                seg = colb(acum, r) - acum_t[r:r + 1, :]
                lmat = jnp.exp(jnp.where(tril, seg, neg_big))
                mr = (cb * lmat * dt_t[r:r + 1, :]).astype(BF16)
                xh = jnp.where(lo_half if hh == 0 else jnp.logical_not(lo_half), xs_slab, 0.0)
                y_acc = y_acc + _dot(mr, xh.astype(BF16))
            hpair = hst[sl, :]
            e_pair = jnp.where(lo_half, colb(e_acum, 2 * p), colb(e_acum, 2 * p + 1))
            y_acc = y_acc + e_pair * _dot_nt(c_bf, hpair.astype(BF16))
            w_pair = jnp.where(lo_half, colb(w_state, 2 * p), colb(w_state, 2 * p + 1))
            xw_t = (xs_slab * w_pair).T
            st = _dot(xw_t.astype(BF16), b_bf)
            da = jnp.concatenate(
                [jnp.broadcast_to(da_last[:, 2 * p:2 * p + 1], (SSM_HEAD_DIM, LANES)),
                 jnp.broadcast_to(da_last[:, 2 * p + 1:2 * p + 2], (SSM_HEAD_DIM, LANES))], axis=0)
            hst[sl, :] = da * hpair + st
            y_slabs.append((y_acc * z_gate[:, sl])[0:t_in, :])

    per_group = D_SSM // SSM_GROUPS // LANES
    for g in range(SSM_GROUPS):
        slabs = y_slabs[g * per_group:(g + 1) * per_group]
        ssum = jnp.sum(slabs[0] * slabs[0], axis=-1, keepdims=True)
        for s in slabs[1:]:
            ssum = ssum + jnp.sum(s * s, axis=-1, keepdims=True)
        r = lax.rsqrt(ssum * (1.0 / (per_group * LANES)) + EPS)
        for k, s in enumerate(slabs):
            sl = slice((g * per_group + k) * LANES, (g * per_group + k + 1) * LANES)
            y_ref[:, sl] = (s * r * gssm_ref[:, sl]).astype(y_ref.dtype)

    if nc > 1:
        xpad[5:8, :] = xpad[5 + T:8 + T, :]

    @pl.when(c == nc - 1)
    def _():
        cout_ref[...] = jnp.zeros(cout_ref.shape, F32)
        if nc > 1:
            cout_ref[0, 0:CONV_W - 1, :] = xpad[5:8, :]
        else:
            cout_ref[0, 0:CONV_W - 1, :] = xpad[5 + valid_last:8 + valid_last, :]
        hout_ref[0] = hst[...]


def _ssd(u, dt, n_batch, seq_rows, valid_len, ssm_p, conv_prev=None, h0=None):
    t_in = min(seq_rows, SSD_T)
    nc = max(seq_rows // SSD_T, 1)
    valid_last = valid_len - (nc - 1) * SSD_T
    has_init = conv_prev is not None
    convw, convb, dtb, alog, dskip, gssm = ssm_p

    def row(b, c):
        return b * nc + c

    in_specs = [
        pl.BlockSpec((t_in, D_SSM), lambda b, c: (row(b, c), COL_XS // D_SSM)),
        pl.BlockSpec((t_in, D_SSM), lambda b, c: (row(b, c), COL_Z // D_SSM)),
        pl.BlockSpec((t_in, BC_DIM), lambda b, c: (row(b, c), COL_BC // BC_DIM)),
        pl.BlockSpec((t_in, LANES), lambda b, c: (row(b, c), 0)),
    ]
    args = [u, u, u, dt]
    if has_init:
        in_specs += [pl.BlockSpec((1, 8, CONV_DIM), lambda b, c: (b, 0, 0)),
                     pl.BlockSpec((1, D_SSM, D_STATE), lambda b, c: (b, 0, 0))]
        args += [conv_prev, h0]
    in_specs += [
        pl.BlockSpec((8, CONV_DIM), lambda b, c: (0, 0)),
        pl.BlockSpec((1, CONV_DIM), lambda b, c: (0, 0)),
        pl.BlockSpec((1, LANES), lambda b, c: (0, 0)),
        pl.BlockSpec((1, LANES), lambda b, c: (0, 0)),
        pl.BlockSpec((1, D_SSM), lambda b, c: (0, 0)),
        pl.BlockSpec((1, D_SSM), lambda b, c: (0, 0)),
    ]
    args += [convw, convb, dtb, alog, dskip, gssm]
    return pl.pallas_call(
        functools.partial(_ssd_kernel, t_in=t_in, valid_last=valid_last, nc=nc, has_init=has_init),
        grid=(n_batch, nc),
        in_specs=in_specs,
        out_specs=[
            pl.BlockSpec((t_in, D_SSM), lambda b, c: (row(b, c), 0)),
            pl.BlockSpec((1, 8, CONV_DIM), lambda b, c: (b, 0, 0)),
            pl.BlockSpec((1, D_SSM, D_STATE), lambda b, c: (b, 0, 0)),
        ],
        out_shape=[
            jax.ShapeDtypeStruct((n_batch * seq_rows, D_SSM), BF16),
            jax.ShapeDtypeStruct((n_batch, 8, CONV_DIM), F32),
            jax.ShapeDtypeStruct((n_batch, D_SSM, D_STATE), F32),
        ],
        scratch_shapes=[pltpu.VMEM((8 + SSD_T, CONV_DIM), F32), pltpu.VMEM((D_SSM, D_STATE), F32)],
        compiler_params=_cparams(("parallel", "arbitrary")),
        name="ssd_scan",
    )(*args)


def _lambda_from(lam_ref, lam_init):
    lp = lam_ref[...]
    s1 = jnp.sum(lp[0:1, :] * lp[1:2, :], axis=-1, keepdims=True)
    s2 = jnp.sum(lp[2:3, :] * lp[3:4, :], axis=-1, keepdims=True)
    return jnp.exp(s1) - jnp.exp(s2) + lam_init


def _attn_prompt_kernel(q_ref, k_ref, v_ref, mq_ref, mk_ref, mv_ref, lam_ref, gsub_ref,
                        od_ref, om_ref, kb, vb, s_s, m_s, l_s, acc_s, *, tq, lam_init):
    qi = pl.program_id(2)

    @pl.when(qi == 0)
    def _():
        kb[...] = k_ref[...].astype(BF16)
        vb[...] = v_ref[...].astype(BF16)

    lane = lax.broadcasted_iota(jnp.int32, (tq, LANES), 1)
    lo = lane < DIFF_DK
    q = q_ref[...] * (DIFF_DK ** -0.5 * LOG2E)
    q2 = jnp.concatenate([jnp.where(lo, q, 0.0), jnp.where(lo, 0.0, q)], axis=0).astype(BF16)
    m_s[...] = jnp.full(m_s.shape, -jnp.inf, F32)
    l_s[...] = jnp.zeros(l_s.shape, F32)
    acc_s[...] = jnp.zeros(acc_s.shape, F32)
    tk = tq // 2
    reps = tk // LANES
    late = (slice(tk, tq), slice(tq + tk, 2 * tq))

    def causal(s):
        r_i = jnp.bitwise_and(lax.broadcasted_iota(jnp.int32, s.shape, 0), s.shape[0] // 2 - 1)
        c_i = lax.broadcasted_iota(jnp.int32, s.shape, 1)
        return jnp.where(r_i >= c_i, s, -jnp.inf)

    def scores(j, masked):
        start = pl.multiple_of(j * tk, tk)
        s = _dot_nt(q2, kb[pl.ds(start, tk), :])
        if masked:
            s = causal(s)
        s_s[:, pl.ds(start, tk)] = s
        m_s[...] = jnp.maximum(m_s[...], jnp.max(s, axis=-1, keepdims=True))

    def weighted(j):
        start = pl.multiple_of(j * tk, tk)
        m_rep = jnp.concatenate([m_s[...]] * reps, axis=1)
        p = jnp.exp2(s_s[:, pl.ds(start, tk)] - m_rep)
        l_s[...] += jnp.sum(p, axis=-1, keepdims=True)
        acc_s[...] += _dot(p.astype(BF16), vb[pl.ds(start, tk), :])

    def body1(j, carry):
        scores(j, False)
        return carry

    def body2(j, carry):
        weighted(j)
        return carry

    lax.fori_loop(0, 2 * qi, body1, 0)
    scores(2 * qi, True)
    start_b = pl.multiple_of((2 * qi + 1) * tk, tk)
    q_late = jnp.concatenate([q2[late[0], :], q2[late[1], :]], axis=0)
    s_late = causal(_dot_nt(q_late, kb[pl.ds(start_b, tk), :]))
    m_late = jnp.max(s_late, axis=-1, keepdims=True)
    for h, rows in enumerate(late):
        s_s[rows, pl.ds(start_b, tk)] = s_late[h * tk:(h + 1) * tk, :]
        m_s[rows, :] = jnp.maximum(m_s[rows, :], m_late[h * tk:(h + 1) * tk, :])
    lax.fori_loop(0, 2 * qi + 1, body2, 0)
    v_late = vb[pl.ds(start_b, tk), :]
    for rows in late:
        m_rep = jnp.concatenate([m_s[rows, :]] * reps, axis=1)
        p = jnp.exp2(s_s[rows, pl.ds(start_b, tk)] - m_rep)
        l_s[rows, :] += jnp.sum(p, axis=-1, keepdims=True)
        acc_s[rows, :] += _dot(p.astype(BF16), v_late)

    lam = _lambda_from(lam_ref, lam_init)
    o = acc_s[0:tq, :] / l_s[0:tq, :] - lam * (acc_s[tq:2 * tq, :] / l_s[tq:2 * tq, :])
    r = lax.rsqrt(jnp.mean(o * o, axis=-1, keepdims=True) + EPS)
    od_ref[...] = ((o * r * gsub_ref[...]) * (1.0 - lam_init)).astype(od_ref.dtype)

    s = _dot_nt(mq_ref[...].astype(BF16), mk_ref[...].astype(BF16)) * (MEM_HD ** -0.5 * LOG2E)
    e = jnp.exp2(s - jnp.max(s, axis=-1, keepdims=True))
    om = _dot(e.astype(BF16), mv_ref[...].astype(BF16)) / jnp.sum(e, axis=-1, keepdims=True)
    om_ref[...] = om.astype(om_ref.dtype)


def _attn_prompt(u, mkv, lam_pack, gsub, n_batch, seq, lam_init, tq=1024):
    nq = seq // tq
    cq, ck, cv, cmq = (COL_Q // LANES, COL_K // LANES, COL_V // LANES, COL_MQ // LANES)
    return pl.pallas_call(
        functools.partial(_attn_prompt_kernel, tq=tq, lam_init=lam_init),
        grid=(n_batch, DIFF_HEADS, nq),
        in_specs=[
            pl.BlockSpec((tq, LANES), lambda b, h, i: (b * nq + i, cq + h)),
            pl.BlockSpec((seq, LANES), lambda b, h, i: (b, ck + h)),
            pl.BlockSpec((seq, LANES), lambda b, h, i: (b, cv + h)),
            pl.BlockSpec((tq, LANES), lambda b, h, i: (b * nq + i, cmq + h)),
            pl.BlockSpec((MEM_TOKENS, LANES), lambda b, h, i: (b, h)),
            pl.BlockSpec((MEM_TOKENS, LANES), lambda b, h, i: (b, MEM_HEADS + h)),
            pl.BlockSpec((8, LANES), lambda b, h, i: (0, 0)),
            pl.BlockSpec((1, LANES), lambda b, h, i: (0, 0)),
        ],
        out_specs=[
            pl.BlockSpec((tq, LANES), lambda b, h, i: (b * nq + i, h)),
            pl.BlockSpec((tq, LANES), lambda b, h, i: (b * nq + i, h)),
        ],
        out_shape=[
            jax.ShapeDtypeStruct((n_batch * seq, D_DIFF), BF16),
            jax.ShapeDtypeStruct((n_batch * seq, D_MEMX), BF16),
        ],
        scratch_shapes=[
            pltpu.VMEM((seq, LANES), BF16),
            pltpu.VMEM((seq, LANES), BF16),
            pltpu.VMEM((2 * tq, seq), F32),
            pltpu.VMEM((2 * tq, LANES), F32),
            pltpu.VMEM((2 * tq, LANES), F32),
            pltpu.VMEM((2 * tq, LANES), F32),
        ],
        compiler_params=_cparams(("parallel", "parallel", "arbitrary")),
        name="attn_prompt",
    )(u, u, u, u, mkv, mkv, lam_pack, gsub)


PAGE = 128
PAGE_ROWS = PAGE * DIFF_HEADS


def _per_head_rows(x_row, split_maps):
    lane = lax.broadcasted_iota(jnp.int32, (1, LANES), 1)
    rows = []
    for r in range(2 * DIFF_HEADS):
        xh = x_row[:, (r // 2) * LANES:(r // 2 + 1) * LANES]
        if split_maps:
            xh = jnp.where((lane < DIFF_DK) if r % 2 == 0 else (lane >= DIFF_DK), xh, 0.0)
        rows.append(xh)
    return jnp.concatenate(rows, axis=0)


def _head_match(n_cols):
    r_i = lax.broadcasted_iota(jnp.int32, (8, n_cols), 0)
    c_i = lax.broadcasted_iota(jnp.int32, (8, n_cols), 1)
    return jnp.bitwise_and(c_i, DIFF_HEADS - 1) == jnp.right_shift(r_i, 1)


def _rider_specs(step_fn, spb):
    W = DIFF_HEADS * LANES
    assert spb & (spb - 1) == 0, "steps per sequence must be a power of two (shift/mask indexing)"
    shift = spb.bit_length() - 1

    def seq(idx):
        return jnp.right_shift(step_fn(*idx[:-1]), shift)

    tok_q = pl.BlockSpec((ROWS_S, W), lambda *idx: (seq(idx), COL_Q // W))
    hbm = pl.BlockSpec(memory_space=pl.ANY)
    part = pl.BlockSpec((8, LANES), lambda *idx: (seq(idx), 0))
    return [tok_q, hbm, hbm], [part, part, part]


def _rider_scratch(npp):
    buf = pltpu.VMEM((2, npp, PAGE_ROWS, LANES), F32)
    return [buf, buf, pltpu.SemaphoreType.DMA((2, 2))]


def _rider_pages(pt_ref, ck_hbm, cv_hbm, kbuf, vbuf, sem, *, spb, npp, page_lo, page_base):
    shift = spb.bit_length() - 1

    def copies(slot, rows):
        out = []
        for i in range(npp):
            out.append(pltpu.make_async_copy(ck_hbm.at[pl.ds(rows[i], PAGE_ROWS), :], kbuf.at[slot, i],
                                             sem.at[0, slot]))
            out.append(pltpu.make_async_copy(cv_hbm.at[pl.ds(rows[i], PAGE_ROWS), :], vbuf.at[slot, i],
                                             sem.at[1, slot]))
        return out

    def start(step, slot):
        seq = jnp.right_shift(step, shift)
        col = page_lo + jnp.bitwise_and(step, spb - 1) * npp
        rows = [pl.multiple_of((page_base + pt_ref[seq, col + i]) * PAGE_ROWS, PAGE_ROWS) for i in range(npp)]
        for c in copies(slot, rows):
            c.start()

    def wait(slot):
        for c in copies(slot, [0] * npp):
            c.wait()

    return start, wait


def _rider_parts(step, spb, tok_q, k_refs, v_refs, pm_ref, pl_ref, pa_ref, m_s, l_s, acc_s):
    npp = len(k_refs)
    j = jnp.bitwise_and(step, spb - 1)

    def first():
        m_s[...] = jnp.full(m_s.shape, -jnp.inf, F32)
        l_s[...] = jnp.zeros(l_s.shape, F32)
        acc_s[...] = jnp.zeros(acc_s.shape, F32)

    def main():
        q = _per_head_rows(tok_q[0:1, :] * (DIFF_DK ** -0.5), True).astype(BF16)
        s_all = jnp.concatenate([_dot_nt(q, k_refs[i][...].astype(BF16)) for i in range(npp)], axis=1)
        s_all = jnp.where(_head_match(npp * PAGE_ROWS), s_all, -jnp.inf)
        m_prev = m_s[...]
        m_new = jnp.maximum(m_prev, jnp.max(s_all, axis=-1, keepdims=True))
        alpha = jnp.exp(m_prev - m_new)
        p = jnp.exp(s_all - m_new[:, 0:1])
        l_s[...] = alpha * l_s[...] + jnp.sum(p, axis=-1, keepdims=True)
        m_s[...] = m_new
        p_bf = p.astype(BF16)
        pv = _dot(p_bf[:, 0:PAGE_ROWS], v_refs[0][...].astype(BF16))
        for i in range(1, npp):
            pv = pv + _dot(p_bf[:, i * PAGE_ROWS:(i + 1) * PAGE_ROWS], v_refs[i][...].astype(BF16))
        acc_s[...] = alpha * acc_s[...] + pv

    def last():
        pm_ref[...] = m_s[...]
        pl_ref[...] = l_s[...]
        pa_ref[...] = acc_s[...]

    return j, first, main, last


def _decode_finish_kernel(tok_q, tok_k, tok_v, tok_mq, mk_ref, mv_ref, lam_ref, gsub_ref, *rest, n_parts, grp,
                          lam_init):
    parts = rest[:3 * n_parts]
    od_ref, om_ref = rest[3 * n_parts:]
    mem_rows = MEM_TOKENS * MEM_HEADS
    lam = _lambda_from(lam_ref, lam_init)
    for g in range(grp):
        t0 = g * ROWS_S
        r8 = slice(g * 8, (g + 1) * 8)
        qmat = _per_head_rows(tok_q[t0:t0 + 1, :] * (DIFF_DK ** -0.5), True)
        s_new = jnp.sum(qmat * _per_head_rows(tok_k[t0:t0 + 1, :], False), axis=-1, keepdims=True)
        m_tot = jnp.broadcast_to(s_new, (8, LANES))
        for h in range(n_parts):
            m_tot = jnp.maximum(m_tot, parts[3 * h][r8, :])
        w_new = jnp.exp(s_new - m_tot)
        l_tot = w_new
        acc = w_new * _per_head_rows(tok_v[t0:t0 + 1, :], False)
        for h in range(n_parts):
            w = jnp.exp(parts[3 * h][r8, :] - m_tot)
            l_tot = l_tot + w * parts[3 * h + 1][r8, :]
            acc = acc + w * parts[3 * h + 2][r8, :]
        o_all = acc / l_tot
        outs = []
        for h in range(DIFF_HEADS):
            o = o_all[2 * h:2 * h + 1, :] - lam * o_all[2 * h + 1:2 * h + 2, :]
            r = lax.rsqrt(jnp.mean(o * o, axis=-1, keepdims=True) + EPS)
            outs.append((o * r * gsub_ref[...]) * (1.0 - lam_init))
        od_ref[r8, :] = jnp.broadcast_to(jnp.concatenate(outs, axis=1), (8, DIFF_HEADS * LANES))

        mrows = slice(g * mem_rows, (g + 1) * mem_rows)
        mqm = _per_head_rows(tok_mq[t0:t0 + 1, :], False)
        s = _dot_nt(mqm.astype(BF16), mk_ref[mrows, :].astype(BF16)) * (MEM_HD ** -0.5)
        s = jnp.where(_head_match(s.shape[1]), s, -jnp.inf)
        e = jnp.exp(s - jnp.max(s, axis=-1, keepdims=True))
        om = _dot(e.astype(BF16), mv_ref[mrows, :].astype(BF16)) / jnp.sum(e, axis=-1, keepdims=True)
        om_row = jnp.concatenate([om[2 * h:2 * h + 1, :] for h in range(MEM_HEADS)], axis=1)
        om_ref[r8, :] = jnp.broadcast_to(om_row, (8, MEM_HEADS * LANES))


def _decode_finish(u_s, parts, mem_k, mem_v, lam_pack, gsub, layer, n_b, lam_init, grp=4):
    W = DIFF_HEADS * LANES
    mem_rows = MEM_TOKENS * MEM_HEADS
    flat = [a for tri in parts for a in tri]
    assert n_b % grp == 0
    nblk = n_b // grp

    def tok(col):
        return pl.BlockSpec((grp * ROWS_S, W), lambda b: (b, col // W))

    od, om = pl.pallas_call(
        functools.partial(_decode_finish_kernel, n_parts=len(parts), grp=grp, lam_init=lam_init),
        grid=(nblk,),
        in_specs=[tok(COL_Q), tok(COL_K), tok(COL_V), tok(COL_MQ),
                  pl.BlockSpec((grp * mem_rows, LANES), lambda b: (layer * nblk + b, 0)),
                  pl.BlockSpec((grp * mem_rows, LANES), lambda b: (layer * nblk + b, 0)),
                  pl.BlockSpec((8, LANES), lambda b: (0, 0)),
                  pl.BlockSpec((1, LANES), lambda b: (0, 0))]
                 + [pl.BlockSpec((grp * 8, LANES), lambda b: (b, 0))] * len(flat),
        out_specs=[pl.BlockSpec((grp * 8, W), lambda b: (b, 0)), pl.BlockSpec((grp * 8, W), lambda b: (b, 0))],
        out_shape=[jax.ShapeDtypeStruct((n_b * 8, W), F32), jax.ShapeDtypeStruct((n_b * 8, W), F32)],
        compiler_params=_cparams(("parallel",)),
        name="decode_finish",
    )(u_s, u_s, u_s, u_s, mem_k, mem_v, lam_pack, gsub, *flat)
    return od.reshape(n_b, 8, W)[:, 0], om.reshape(n_b, 8, W)[:, 0]


def _outproj_kernel(x_ref, y_ref, od_ref, om_ref, w1_ref, w2_ref, w3_ref, o_ref):
    o_ref[...] = (x_ref[...]
                  + _dot(y_ref[...].astype(BF16), w1_ref[...])
                  + _dot(od_ref[...].astype(BF16), w2_ref[...])
                  + _dot(om_ref[...].astype(BF16), w3_ref[...]))


def _outproj(x, y, od, om, w_out, tm=512, tn=D_MODEL):
    m = x.shape[0]
    tm = min(tm, m)
    assert m % tm == 0
    return pl.pallas_call(
        _outproj_kernel,
        grid=(m // tm, D_MODEL // tn),
        in_specs=[
            pl.BlockSpec((tm, tn), lambda i, j: (i, j)),
            pl.BlockSpec((tm, D_SSM), lambda i, j: (i, 0)),
            pl.BlockSpec((tm, D_DIFF), lambda i, j: (i, 0)),
            pl.BlockSpec((tm, D_MEMX), lambda i, j: (i, 0)),
            pl.BlockSpec((D_SSM, tn), lambda i, j: (0, j)),
            pl.BlockSpec((D_DIFF, tn), lambda i, j: (D_SSM // D_DIFF, j)),
            pl.BlockSpec((D_MEMX, tn), lambda i, j: ((D_SSM + D_DIFF) // D_MEMX, j)),
        ],
        out_specs=pl.BlockSpec((tm, tn), lambda i, j: (i, j)),
        out_shape=jax.ShapeDtypeStruct((m, D_MODEL), F32),
        compiler_params=_cparams(("parallel", "arbitrary")),
        name="out_proj",
    )(x, y, od, om, w_out, w_out, w_out)


def _mlp_kernel(*refs, rider):
    if rider:
        npp, spb = rider["npp"], rider["spb"]
        pt_ref, x_ref, g_ref, wu_ref, wd_ref, tok_q, ck_hbm, cv_hbm = refs[:8]
        o_ref, pm_ref, pl_ref, pa_ref, h_ref, m_s, l_s, acc_s, kbuf, vbuf, sem = refs[8:]
    else:
        x_ref, g_ref, wu_ref, wd_ref, o_ref, h_ref = refs
    f = pl.program_id(1)

    @pl.when(f == 0)
    def _():
        x = x_ref[...]
        r = lax.rsqrt(jnp.mean(x * x, axis=-1, keepdims=True) + EPS)
        h_ref[...] = (x * r * g_ref[...]).astype(BF16)
        o_ref[...] = x

    if not rider:
        a = jnp.maximum(_dot(h_ref[...], wu_ref[...]), 0.0)
        o_ref[...] += _dot((a * a).astype(BF16), wd_ref[...])
        return

    rounds = rider["rounds"]
    n_steps = pl.num_programs(0) * pl.num_programs(1)
    n_rounds = n_steps * rounds
    step = pl.program_id(0) * pl.num_programs(1) + f
    start, wait = _rider_pages(pt_ref, ck_hbm, cv_hbm, kbuf, vbuf, sem, spb=spb, npp=npp,
                               page_lo=rider["page_lo"], page_base=rider["page_base"])
    ts = [step * rounds + r for r in range(rounds)]
    slots = [jnp.bitwise_and(t, 1) for t in ts]
    parts = []
    for t, slot in zip(ts, slots):
        k_refs = [kbuf.at[slot, i] for i in range(npp)]
        v_refs = [vbuf.at[slot, i] for i in range(npp)]
        parts.append(_rider_parts(t, spb, tok_q, k_refs, v_refs, pm_ref, pl_ref, pa_ref, m_s, l_s, acc_s))

    def run_round(r):
        wait(slots[r])
        parts[r][2]()

    def next_round(t):
        return jnp.where(t + 1 == n_rounds, 0, t + 1)

    pl.when(step == 0)(lambda: start(ts[0], slots[0]))
    pl.when(parts[0][0] == 0)(parts[0][1])
    start(next_round(ts[0]), 1 - slots[0])
    a = jnp.maximum(_dot(h_ref[...], wu_ref[...]), 0.0)
    if rounds == 2:
        run_round(0)
        start(next_round(ts[1]), slots[0])
    o_ref[...] += _dot((a * a).astype(BF16), wd_ref[...])
    run_round(rounds - 1)
    pl.when(parts[-1][0] == spb - 1)(parts[-1][3])
    pl.when(step == n_steps - 1)(lambda: wait(1 - slots[-1]))


def _mlp(x, g, w_up, w_down, tm=512, tf=1024, rider=None):
    m = x.shape[0]
    d_ff = w_up.shape[1]
    tm = min(tm, m)
    assert m % tm == 0 and d_ff % tf == 0
    grid = (m // tm, d_ff // tf)
    in_specs = [
        pl.BlockSpec((tm, D_MODEL), lambda i, f: (i, 0)),
        pl.BlockSpec((1, D_MODEL), lambda i, f: (0, 0)),
        pl.BlockSpec((D_MODEL, tf), lambda i, f: (0, f)),
        pl.BlockSpec((tf, D_MODEL), lambda i, f: (f, 0)),
    ]
    out_specs = [pl.BlockSpec((tm, D_MODEL), lambda i, f: (i, 0))]
    out_shape = [jax.ShapeDtypeStruct((m, D_MODEL), F32)]
    scratch = [pltpu.VMEM((tm, D_MODEL), BF16)]
    args = [x, g, w_up, w_down]
    if rider is None:
        out = pl.pallas_call(
            functools.partial(_mlp_kernel, rider=None),
            grid=grid, in_specs=in_specs, out_specs=out_specs, out_shape=out_shape, scratch_shapes=scratch,
            compiler_params=_cparams(("parallel", "arbitrary")), name="mlp",
        )(*args)
        return out[0]
    u_s, cache_k, cache_v, page_table, layer, n_pool, npp, page_lo, n_pages = rider
    n_b = page_table.shape[0]
    spb = n_pages // npp
    n_steps = grid[0] * grid[1]
    rounds = n_b * spb // n_steps
    assert n_pages % npp == 0 and rounds in (1, 2) and n_steps * rounds == n_b * spb and spb % rounds == 0
    nf = grid[1]
    r_in, r_out = _rider_specs(lambda i, f: i * nf + f, spb // rounds)
    host_in = [pl.BlockSpec(s.block_shape, lambda i, f, pt, im=s.index_map: im(i, f)) for s in in_specs]
    host_in[0] = pl.BlockSpec(host_in[0].block_shape, host_in[0].index_map, pipeline_mode=pl.Buffered(1))
    host_out = [pl.BlockSpec(s.block_shape, lambda i, f, pt, im=s.index_map: im(i, f)) for s in out_specs]
    grid_spec = pltpu.PrefetchScalarGridSpec(
        num_scalar_prefetch=1, grid=grid,
        in_specs=host_in + r_in, out_specs=host_out + r_out,
        scratch_shapes=scratch + [pltpu.VMEM((8, LANES), F32)] * 3 + _rider_scratch(npp),
    )
    part_shape = jax.ShapeDtypeStruct((n_b * 8, LANES), F32)
    rider_cfg = dict(npp=npp, spb=spb, rounds=rounds, page_lo=page_lo, page_base=layer * n_pool)
    out, pm, pl_, pa = pl.pallas_call(
        functools.partial(_mlp_kernel, rider=rider_cfg),
        grid_spec=grid_spec,
        out_shape=out_shape + [part_shape] * 3,
        compiler_params=_cparams(("arbitrary", "arbitrary")),
        name="mlp_rider",
    )(page_table, *args, u_s, cache_k, cache_v)
    return out, (pm, pl_, pa)


def _pad_lanes(v, width=LANES):
    v = v.reshape(1, -1).astype(F32)
    return jnp.pad(v, ((0, 0), (0, width - v.shape[1])))


def kernel(x_prompt, x_sample, mem_prompt, cache_diff_k, cache_diff_v, cache_mem_k, cache_mem_v, state_conv, state_ssm, page_table, norm_mix, w_in, conv_w, conv_b, dt_bias, a_log, d_skip, g_ssm, g_q, g_k, lambda_q1, lambda_k1, lambda_q2, lambda_k2, g_subln, norm_mem, w_mem_kv, g_mq, g_mk, w_out, norm_mlp, w_up, w_down):
    depth = w_in.shape[0]
    bp, seq, _ = x_prompt.shape
    bd, dec_seq, _ = x_sample.shape
    assert dec_seq == 1 and seq % SSD_T == 0
    n_pool = cache_diff_k.shape[1]

    xp = x_prompt.reshape(bp * seq, D_MODEL)
    xs = x_sample.reshape(bd, D_MODEL)
    cache_k = cache_diff_k.reshape(depth * n_pool * PAGE_ROWS, LANES)
    cache_v = cache_diff_v.reshape(depth * n_pool * PAGE_ROWS, LANES)
    mem_k = cache_mem_k.reshape(depth * bd * MEM_TOKENS * MEM_HEADS, MEM_HD)
    mem_v = cache_mem_v.reshape(depth * bd * MEM_TOKENS * MEM_HEADS, MEM_HD)

    outs = [[] for _ in range(10)]
    for l in range(depth):
        lam_init = 0.8 - 0.6 * math.exp(-0.3 * l)
        wl_t = jnp.swapaxes(w_in[l], 0, 1).astype(BF16)
        o_dt = D_SSM + CONV_DIM
        o_q = o_dt + SSM_HEADS
        w_main = [(wl_t, o_dt // PROJ_TN, True), (wl_t[o_q:], (N_MAIN - o_dt) // PROJ_TN, True)]
        w_dt = jnp.pad(wl_t[o_dt:o_q], ((0, LANES - SSM_HEADS), (0, 0)))
        ones = jnp.ones((PROJ_TN,), F32)
        gains = jnp.concatenate(
            [ones] * 5 + [jnp.tile(g_q[l], 2 * DIFF_HEADS), jnp.tile(g_k[l], 2 * DIFF_HEADS), ones,
                          jnp.tile(g_mq[l], MEM_HEADS)]).reshape(1, N_MAIN)
        modes = (0, 0, 0, 0, 0, DIFF_DK, DIFF_DK, 0, MEM_HD)
        g_mix = norm_mix[l].reshape(1, D_MODEL)

        ssm_p = (jnp.pad(conv_w[l], ((0, 8 - CONV_W), (0, 0))),
                 conv_b[l].reshape(1, CONV_DIM),
                 _pad_lanes(dt_bias[l]), _pad_lanes(a_log[l]),
                 jnp.repeat(d_skip[l], SSM_HEAD_DIM).reshape(1, D_SSM),
                 g_ssm[l].reshape(1, D_SSM))
        lam_pack = jnp.concatenate(
            [_pad_lanes(lambda_q1[l]), _pad_lanes(lambda_k1[l]), _pad_lanes(lambda_q2[l]),
             _pad_lanes(lambda_k2[l]), jnp.zeros((4, LANES), F32)], axis=0)
        gsub = g_subln[l].reshape(1, LANES)
        w_out_bf = w_out[l].astype(BF16)
        w_up_bf = w_up[l].astype(BF16)
        w_down_bf = w_down[l].astype(BF16)
        g_mlp = norm_mlp[l].reshape(1, D_MODEL)

        xs_pad = jnp.pad(xs.reshape(bd, 1, D_MODEL), ((0, 0), (0, ROWS_S - 1), (0, 0))).reshape(bd * ROWS_S, D_MODEL)
        u_s, dt_s = _norm_proj(xs_pad, g_mix, w_main, gains, modes, w_dt=w_dt)

        u_p, dt_p, k_rows, v_rows = _norm_proj(xp, g_mix, w_main, gains, modes, w_dt=w_dt,
                                               rows_blocks=(COL_K // PROJ_TN, COL_V // PROJ_TN))
        y_p, conv_p, h_p = _ssd(u_p, dt_p, bp, seq, seq, ssm_p)
        mem_gains = jnp.concatenate([jnp.tile(g_mk[l], MEM_HEADS), ones]).reshape(1, 2 * D_MEMX)
        mkv, = _norm_proj(mem_prompt.reshape(bp * MEM_TOKENS, D_MODEL), norm_mem[l].reshape(1, D_MODEL),
                          [(w_mem_kv[l], 2 * D_MEMX // PROJ_TN, False)], mem_gains, (MEM_HD, 0))
        od_p, om_p = _attn_prompt(u_p, mkv, lam_pack, gsub, bp, seq, lam_init)
        xp_mid = _outproj(xp, y_p, od_p, om_p, w_out_bf)
        n_pages = page_table.shape[1]
        xp_new, part = _mlp(xp_mid, g_mlp, w_up_bf, w_down_bf, tm=1024, tf=512,
                            rider=(u_s, cache_k, cache_v, page_table, l, n_pool, 16, 0, n_pages))

        outs[0].append(k_rows.reshape(bp, seq, DIFF_HEADS, LANES))
        outs[1].append(v_rows.reshape(bp, seq, DIFF_HEADS, LANES))
        outs[4].append(mkv[:, :D_MEMX].reshape(bp, MEM_TOKENS, MEM_HEADS, MEM_HD))
        outs[5].append(mkv[:, D_MEMX:].reshape(bp, MEM_TOKENS, MEM_HEADS, MEM_HD))
        outs[6].append(conv_p[:, :CONV_W - 1])
        outs[7].append(h_p.reshape(bp, SSM_HEADS, SSM_HEAD_DIM, D_STATE))

        conv_prev = jnp.pad(state_conv[l], ((0, 0), (8 - (CONV_W - 1), 0), (0, 0)))
        y_s, conv_s, h_s = _ssd(u_s, dt_s, bd, ROWS_S, 1, ssm_p, conv_prev=conv_prev,
                                h0=state_ssm[l].reshape(bd, D_SSM, D_STATE))
        od_s, om_s = _decode_finish(u_s, [part], mem_k, mem_v, lam_pack, gsub, l, bd, lam_init)
        y_s0 = y_s.reshape(bd, ROWS_S, D_SSM)[:, 0]
        xs_mid = _outproj(xs, y_s0, od_s, om_s, w_out_bf)
        xs_new = _mlp(xs_mid, g_mlp, w_up_bf, w_down_bf)
        u_s0 = u_s.reshape(bd, ROWS_S, N_MAIN)[:, 0]
        outs[2].append(u_s0[:, COL_K:COL_K + D_DIFF].reshape(bd, 1, DIFF_HEADS, LANES))
        outs[3].append(u_s0[:, COL_V:COL_V + D_DIFF].reshape(bd, 1, DIFF_HEADS, LANES))
        outs[8].append(conv_s[:, :CONV_W - 1])
        outs[9].append(h_s.reshape(bd, SSM_HEADS, SSM_HEAD_DIM, D_STATE))

        xp, xs = xp_new, xs_new

    st = [jnp.stack(o) for o in outs]
    return (xp.reshape(bp, seq, D_MODEL), xs.reshape(bd, 1, D_MODEL),
            st[0], st[1], st[2], st[3], st[4], st[5], st[6], st[7], st[8], st[9])
```

```python
import functools
import math

import jax
import jax.numpy as jnp
from jax import lax
from jax.experimental import pallas as pl
from jax.experimental.pallas import tpu as pltpu

F32 = jnp.float32
BF16 = jnp.bfloat16

D_MODEL = 2048
D_SSM = 1024
D_DIFF = 512
D_MEMX = 512
SSM_HEADS = 16
SSM_HEAD_DIM = 64
SSM_GROUPS = 2
D_STATE = 128
CONV_W = 4
BC_DIM = 2 * SSM_GROUPS * D_STATE
CONV_DIM = D_SSM + BC_DIM
DIFF_HEADS = 4
DIFF_DK = 64
MEM_HEADS = 4
MEM_HD = 128
MEM_TOKENS = 256
D_FF = 4 * D_MODEL
EPS = 1e-6
LOG2E = math.log2(math.e)

LANES = 128
SUBLANES_BF16 = 16
VMEM_LIMIT_BYTES = 58 * 1024 * 1024

N_MAIN = D_SSM + D_SSM + BC_DIM + 3 * D_DIFF + D_MEMX
COL_Z, COL_XS, COL_BC, COL_Q, COL_K, COL_V, COL_MQ = 0, 1024, 2048, 2560, 3072, 3584, 4096
PROJ_TN = 512
SSD_T = 128
ROWS_S = SUBLANES_BF16


def _cparams(sem):
    return pltpu.CompilerParams(dimension_semantics=sem, vmem_limit_bytes=VMEM_LIMIT_BYTES)


def _dot(a, b):
    return jnp.dot(a, b, preferred_element_type=F32)


def _dot_nt(a, b):
    return lax.dot_general(a, b, (((1,), (1,)), ((), ())), preferred_element_type=F32)


def _silu(x):
    return x * (1.0 / (1.0 + jnp.exp(-x)))


def _group_rmsnorm_slab(a, gain, group):
    sq = a * a
    if group == LANES:
        r = lax.rsqrt(jnp.sum(sq, axis=-1, keepdims=True) * (1.0 / LANES) + EPS)
    else:
        lane = lax.broadcasted_iota(jnp.int32, a.shape, 1)
        lo = lane < group
        s_lo = jnp.sum(jnp.where(lo, sq, 0.0), axis=-1, keepdims=True)
        s_hi = jnp.sum(jnp.where(lo, 0.0, sq), axis=-1, keepdims=True)
        r = jnp.where(lo, lax.rsqrt(s_lo * (1.0 / group) + EPS),
                      lax.rsqrt(s_hi * (1.0 / group) + EPS))
    return a * r * gain


def _proj_kernel(*refs, modes, seg_of, w_trans, has_dt, rows_blocks):
    n_w = len(w_trans)
    x_ref, g_ref = refs[:2]
    w_refs = refs[2:2 + n_w]
    gain_ref = refs[2 + n_w]
    pos = 3 + n_w
    if has_dt:
        wdt_ref = refs[pos]
        pos += 1
    u_ref = refs[pos]
    pos += 1
    if has_dt:
        dt_ref = refs[pos]
        pos += 1
    rows_refs = refs[pos:pos + len(rows_blocks)]
    h_ref = refs[pos + len(rows_blocks)]
    j = pl.program_id(1)
    tm = x_ref.shape[0]

    @pl.when(j == 0)
    def _():
        x = x_ref[...]
        r = lax.rsqrt(jnp.mean(x * x, axis=-1, keepdims=True) + EPS)
        h_ref[...] = (x * r * g_ref[...]).astype(BF16)
        if has_dt:
            dt_ref[...] = _dot_nt(h_ref[...], wdt_ref[...])

    keys = {}
    for jj, m in enumerate(modes):
        ri = rows_blocks.index(jj) if jj in rows_blocks else None
        keys.setdefault((seg_of[jj], m, ri), []).append(jj)
    for (seg, mode, ri), jjs in keys.items():
        cond = j == jjs[0]
        for jj in jjs[1:]:
            cond = jnp.logical_or(cond, j == jj)

        @pl.when(cond)
        def _(seg=seg, mode=mode, ri=ri):
            w = w_refs[seg][...].astype(BF16)
            acc = _dot_nt(h_ref[...], w) if w_trans[seg] else _dot(h_ref[...], w)
            if mode == 0 and ri is None:
                u_ref[...] = acc
                return
            gain = gain_ref[...]
            for s in range(acc.shape[1] // LANES):
                sl = slice(s * LANES, (s + 1) * LANES)
                slab = acc[:, sl]
                if mode:
                    slab = _group_rmsnorm_slab(slab, gain[:, sl], mode)
                u_ref[:, sl] = slab
                if ri is not None:
                    rows_refs[ri][pl.ds(s, tm, stride=PROJ_TN // LANES), :] = slab


def _norm_proj(x, g, ws, gains, modes, w_dt=None, rows_blocks=(), tm=1024):
    m, k = x.shape
    tm = min(tm, m)
    nblk = [nb for _, nb, _ in ws]
    w_trans = tuple(t for _, _, t in ws)
    ws = [w for w, _, _ in ws]
    assert m % tm == 0 and sum(nblk) == len(modes)
    assert all(w.shape[0 if t else 1] >= nb * PROJ_TN for w, nb, t in zip(ws, nblk, w_trans))
    seg_of, offs = [], []
    for a, nb in enumerate(nblk):
        offs.append(len(seg_of))
        seg_of += [a] * nb
    n = PROJ_TN * len(modes)
    has_dt = w_dt is not None

    def w_spec(a):
        def blk(j):
            return jnp.minimum(jnp.maximum(j - offs[a], 0), nblk[a] - 1)
        if w_trans[a]:
            return pl.BlockSpec((PROJ_TN, k), lambda i, j: (blk(j), 0))
        return pl.BlockSpec((k, PROJ_TN), lambda i, j: (0, blk(j)))

    in_specs = [pl.BlockSpec((tm, k), lambda i, j: (i, 0)), pl.BlockSpec((1, k), lambda i, j: (0, 0))]
    in_specs += [w_spec(a) for a in range(len(ws))]
    in_specs.append(pl.BlockSpec((1, PROJ_TN), lambda i, j: (0, j)))
    args = [x, g, *ws, gains]
    out_shape = [jax.ShapeDtypeStruct((m, n), F32)]
    out_specs = [pl.BlockSpec((tm, PROJ_TN), lambda i, j: (i, j))]
    if has_dt:
        in_specs.append(pl.BlockSpec((LANES, k), lambda i, j: (0, 0)))
        args.append(w_dt)
        out_shape.append(jax.ShapeDtypeStruct((m, LANES), F32))
        out_specs.append(pl.BlockSpec((tm, LANES), lambda i, j: (i, 0)))
    heads = PROJ_TN // LANES
    for _ in rows_blocks:
        out_shape.append(jax.ShapeDtypeStruct((m * heads, LANES), F32))
        out_specs.append(pl.BlockSpec((tm * heads, LANES), lambda i, j: (i, 0)))
    return pl.pallas_call(
        functools.partial(_proj_kernel, modes=tuple(modes), seg_of=tuple(seg_of), w_trans=w_trans, has_dt=has_dt,
                          rows_blocks=tuple(rows_blocks)),
        grid=(m // tm, len(modes)),
        in_specs=in_specs,
        out_specs=out_specs,
        out_shape=out_shape,
        scratch_shapes=[pltpu.VMEM((tm, k), BF16)],
        compiler_params=_cparams(("parallel", "arbitrary")),
        name="norm_proj",
    )(*args)


def _split3(x):
    hi = x.astype(BF16)
    r1 = x - hi.astype(F32)
    mid = r1.astype(BF16)
    lo = (r1 - mid.astype(F32)).astype(BF16)
    return hi, mid, lo


def _ssd_kernel(*refs, t_in, valid_last, nc, has_init):
    if has_init:
        (xs_ref, z_ref, bc_ref, dt_ref, cprev_ref, h0_ref, convw_ref, convb_ref, dtb_ref, alog_ref,
         dskip_ref, gssm_ref, y_ref, cout_ref, hout_ref, xpad, hst) = refs
    else:
        (xs_ref, z_ref, bc_ref, dt_ref, convw_ref, convb_ref, dtb_ref, alog_ref,
         dskip_ref, gssm_ref, y_ref, cout_ref, hout_ref, xpad, hst) = refs
    T = SSD_T
    c = pl.program_id(1)
    single_step = nc == 1 and valid_last == 1

    @pl.when(c == 0)
    def _():
        if has_init:
            xpad[0:8, :] = cprev_ref[0]
            hst[...] = h0_ref[0]
        else:
            xpad[0:8, :] = jnp.zeros((8, CONV_DIM), F32)
            hst[...] = jnp.zeros(hst.shape, F32)

    def rows(ref):
        v = ref[...]
        if t_in < T:
            v = jnp.concatenate([v, jnp.zeros((T - t_in, v.shape[1]), F32)], axis=0)
        return v

    xpad[8:8 + T, 0:D_SSM] = rows(xs_ref)
    xpad[8:8 + T, D_SSM:CONV_DIM] = rows(bc_ref)
    z = rows(z_ref)
    dt_raw = rows(dt_ref)

    live = 8 if single_step else T
    convw = convw_ref[...]
    conv = convb_ref[...] + xpad[5:5 + live, :] * convw[0:1, :]
    for jtap in range(1, CONV_W):
        conv = conv + xpad[5 + jtap:5 + jtap + live, :] * convw[jtap:jtap + 1, :]
    xact = _silu(conv)
    z_gate = _silu(z[0:live, :])
    if live < T:
        xact = jnp.concatenate([xact, jnp.zeros((T - live, CONV_DIM), F32)], axis=0)
        z_gate = jnp.concatenate([z_gate, jnp.zeros((T - live, D_SSM), F32)], axis=0)

    row_i = lax.broadcasted_iota(jnp.int32, (T, LANES), 0)
    col_i = lax.broadcasted_iota(jnp.int32, (T, LANES), 1)
    xv = dt_raw + dtb_ref[...]
    dt = jnp.maximum(xv, 0.0) + jnp.log1p(jnp.exp(-jnp.abs(xv)))
    if valid_last < T:
        dt = jnp.where(row_i < valid_last, dt, 0.0)
    a_neg = -jnp.exp(alog_ref[...])
    adt = dt * a_neg

    tril = (row_i >= col_i)
    tril_bf = jnp.where(tril, 1.0, 0.0).astype(BF16)
    a_hi, a_mid, a_lo = _split3(adt)
    acum = _dot(tril_bf, a_hi) + _dot(tril_bf, a_mid) + _dot(tril_bf, a_lo)
    acum_t = acum.T
    dt_t = dt.T
    e_acum = jnp.exp(acum)
    a_last = acum[T - 1:T, :]
    w_state = jnp.exp(a_last - acum) * dt
    da_last = jnp.exp(a_last)

    lo_half = col_i < SSM_HEAD_DIM
    neg_big = jnp.float32(-1e30)

    def colb(tile, r):
        return jnp.broadcast_to(tile[:, r:r + 1], (T, LANES))

    y_slabs = []
    for g in range(SSM_GROUPS):
        b_g = xact[:, D_SSM + g * D_STATE:D_SSM + (g + 1) * D_STATE]
        c_g = xact[:, D_SSM + SSM_GROUPS * D_STATE + g * D_STATE:
                   D_SSM + SSM_GROUPS * D_STATE + (g + 1) * D_STATE]
        b_bf = b_g.astype(BF16)
        c_bf = c_g.astype(BF16)
        if single_step:
            cb00 = jnp.sum(c_g[0:1, :] * b_g[0:1, :], axis=-1, keepdims=True)
        else:
            cb = _dot_nt(c_bf, b_bf)
        for pp in range(SSM_HEADS // SSM_GROUPS // 2):
            p = g * (SSM_HEADS // SSM_GROUPS // 2) + pp
            sl = slice(p * LANES, (p + 1) * LANES)
            xs_slab = xact[:, sl]
            y_acc = dskip_ref[:, sl] * xs_slab
            if single_step:
                dt_row = jnp.where(lo_half[0:1, :], dt[0:1, 2 * p:2 * p + 1], dt[0:1, 2 * p + 1:2 * p + 2])
                y_acc = y_acc + (cb00 * dt_row) * xs_slab
            for hh in range(0 if single_step else 2):
                r = 2 * p + hh
                seg = colb(acum, r) - acum_t[r:r + 1, :]
                lmat = jnp.exp(jnp.where(tril, seg, neg_big))
                mr = (cb * lmat * dt_t[r:r + 1, :]).astype(BF16)
                xh = jnp.where(lo_half if hh == 0 else jnp.logical_not(lo_half), xs_slab, 0.0)
                y_acc = y_acc + _dot(mr, xh.astype(BF16))
            hpair = hst[sl, :]
            e_pair = jnp.where(lo_half, colb(e_acum, 2 * p), colb(e_acum, 2 * p + 1))
            y_acc = y_acc + e_pair * _dot_nt(c_bf, hpair.astype(BF16))
            w_pair = jnp.where(lo_half, colb(w_state, 2 * p), colb(w_state, 2 * p + 1))
            xw_t = (xs_slab * w_pair).T
            st = _dot(xw_t.astype(BF16), b_bf)
            da = jnp.concatenate(
                [jnp.broadcast_to(da_last[:, 2 * p:2 * p + 1], (SSM_HEAD_DIM, LANES)),
                 jnp.broadcast_to(da_last[:, 2 * p + 1:2 * p + 2], (SSM_HEAD_DIM, LANES))], axis=0)
            hst[sl, :] = da * hpair + st
            y_slabs.append((y_acc * z_gate[:, sl])[0:t_in, :])

    per_group = D_SSM // SSM_GROUPS // LANES
    for g in range(SSM_GROUPS):
        slabs = y_slabs[g * per_group:(g + 1) * per_group]
        ssum = jnp.sum(slabs[0] * slabs[0], axis=-1, keepdims=True)
        for s in slabs[1:]:
            ssum = ssum + jnp.sum(s * s, axis=-1, keepdims=True)
        r = lax.rsqrt(ssum * (1.0 / (per_group * LANES)) + EPS)
        for k, s in enumerate(slabs):
            sl = slice((g * per_group + k) * LANES, (g * per_group + k + 1) * LANES)
            y_ref[:, sl] = (s * r * gssm_ref[:, sl]).astype(y_ref.dtype)

    if nc > 1:
        xpad[5:8, :] = xpad[5 + T:8 + T, :]

    @pl.when(c == nc - 1)
    def _():
        cout_ref[...] = jnp.zeros(cout_ref.shape, F32)
        if nc > 1:
            cout_ref[0, 0:CONV_W - 1, :] = xpad[5:8, :]
        else:
            cout_ref[0, 0:CONV_W - 1, :] = xpad[5 + valid_last:8 + valid_last, :]
        hout_ref[0] = hst[...]


def _ssd(u, dt, n_batch, seq_rows, valid_len, ssm_p, conv_prev=None, h0=None):
    t_in = min(seq_rows, SSD_T)
    nc = max(seq_rows // SSD_T, 1)
    valid_last = valid_len - (nc - 1) * SSD_T
    has_init = conv_prev is not None
    convw, convb, dtb, alog, dskip, gssm = ssm_p

    def row(b, c):
        return b * nc + c

    in_specs = [
        pl.BlockSpec((t_in, D_SSM), lambda b, c: (row(b, c), COL_XS // D_SSM)),
        pl.BlockSpec((t_in, D_SSM), lambda b, c: (row(b, c), COL_Z // D_SSM)),
        pl.BlockSpec((t_in, BC_DIM), lambda b, c: (row(b, c), COL_BC // BC_DIM)),
        pl.BlockSpec((t_in, LANES), lambda b, c: (row(b, c), 0)),
    ]
    args = [u, u, u, dt]
    if has_init:
        in_specs += [pl.BlockSpec((1, 8, CONV_DIM), lambda b, c: (b, 0, 0)),
                     pl.BlockSpec((1, D_SSM, D_STATE), lambda b, c: (b, 0, 0))]
        args += [conv_prev, h0]
    in_specs += [
        pl.BlockSpec((8, CONV_DIM), lambda b, c: (0, 0)),
        pl.BlockSpec((1, CONV_DIM), lambda b, c: (0, 0)),
        pl.BlockSpec((1, LANES), lambda b, c: (0, 0)),
        pl.BlockSpec((1, LANES), lambda b, c: (0, 0)),
        pl.BlockSpec((1, D_SSM), lambda b, c: (0, 0)),
        pl.BlockSpec((1, D_SSM), lambda b, c: (0, 0)),
    ]
    args += [convw, convb, dtb, alog, dskip, gssm]
    return pl.pallas_call(
        functools.partial(_ssd_kernel, t_in=t_in, valid_last=valid_last, nc=nc, has_init=has_init),
        grid=(n_batch, nc),
        in_specs=in_specs,
        out_specs=[
            pl.BlockSpec((t_in, D_SSM), lambda b, c: (row(b, c), 0)),
            pl.BlockSpec((1, 8, CONV_DIM), lambda b, c: (b, 0, 0)),
            pl.BlockSpec((1, D_SSM, D_STATE), lambda b, c: (b, 0, 0)),
        ],
        out_shape=[
            jax.ShapeDtypeStruct((n_batch * seq_rows, D_SSM), BF16),
            jax.ShapeDtypeStruct((n_batch, 8, CONV_DIM), F32),
            jax.ShapeDtypeStruct((n_batch, D_SSM, D_STATE), F32),
        ],
        scratch_shapes=[pltpu.VMEM((8 + SSD_T, CONV_DIM), F32), pltpu.VMEM((D_SSM, D_STATE), F32)],
        compiler_params=_cparams(("parallel", "arbitrary")),
        name="ssd_scan",
    )(*args)


def _lambda_from(lam_ref, lam_init):
    lp = lam_ref[...]
    s1 = jnp.sum(lp[0:1, :] * lp[1:2, :], axis=-1, keepdims=True)
    s2 = jnp.sum(lp[2:3, :] * lp[3:4, :], axis=-1, keepdims=True)
    return jnp.exp(s1) - jnp.exp(s2) + lam_init


def _attn_prompt_kernel(q_ref, k_ref, v_ref, mq_ref, mk_ref, mv_ref, lam_ref, gsub_ref,
                        od_ref, om_ref, kb, vb, s_s, m_s, l_s, acc_s, *, tq, lam_init):
    qi = pl.program_id(2)

    @pl.when(qi == 0)
    def _():
        kb[...] = k_ref[...].astype(BF16)
        vb[...] = v_ref[...].astype(BF16)

    lane = lax.broadcasted_iota(jnp.int32, (tq, LANES), 1)
    lo = lane < DIFF_DK
    q = q_ref[...] * (DIFF_DK ** -0.5 * LOG2E)
    q2 = jnp.concatenate([jnp.where(lo, q, 0.0), jnp.where(lo, 0.0, q)], axis=0).astype(BF16)
    m_s[...] = jnp.full(m_s.shape, -jnp.inf, F32)
    l_s[...] = jnp.zeros(l_s.shape, F32)
    acc_s[...] = jnp.zeros(acc_s.shape, F32)
    tk = tq // 2
    reps = tk // LANES
    late = (slice(tk, tq), slice(tq + tk, 2 * tq))

    def causal(s):
        r_i = jnp.bitwise_and(lax.broadcasted_iota(jnp.int32, s.shape, 0), s.shape[0] // 2 - 1)
        c_i = lax.broadcasted_iota(jnp.int32, s.shape, 1)
        return jnp.where(r_i >= c_i, s, -jnp.inf)

    def scores(j, masked):
        start = pl.multiple_of(j * tk, tk)
        s = _dot_nt(q2, kb[pl.ds(start, tk), :])
        if masked:
            s = causal(s)
        s_s[:, pl.ds(start, tk)] = s
        m_s[...] = jnp.maximum(m_s[...], jnp.max(s, axis=-1, keepdims=True))

    def weighted(j):
        start = pl.multiple_of(j * tk, tk)
        m_rep = jnp.concatenate([m_s[...]] * reps, axis=1)
        p = jnp.exp2(s_s[:, pl.ds(start, tk)] - m_rep)
        l_s[...] += jnp.sum(p, axis=-1, keepdims=True)
        acc_s[...] += _dot(p.astype(BF16), vb[pl.ds(start, tk), :])

    def body1(j, carry):
        scores(j, False)
        return carry

    def body2(j, carry):
        weighted(j)
        return carry

    lax.fori_loop(0, 2 * qi, body1, 0)
    scores(2 * qi, True)
    start_b = pl.multiple_of((2 * qi + 1) * tk, tk)
    q_late = jnp.concatenate([q2[late[0], :], q2[late[1], :]], axis=0)
    s_late = causal(_dot_nt(q_late, kb[pl.ds(start_b, tk), :]))
    m_late = jnp.max(s_late, axis=-1, keepdims=True)
    for h, rows in enumerate(late):
        s_s[rows, pl.ds(start_b, tk)] = s_late[h * tk:(h + 1) * tk, :]
        m_s[rows, :] = jnp.maximum(m_s[rows, :], m_late[h * tk:(h + 1) * tk, :])
    lax.fori_loop(0, 2 * qi + 1, body2, 0)
    v_late = vb[pl.ds(start_b, tk), :]
    for rows in late:
        m_rep = jnp.concatenate([m_s[rows, :]] * reps, axis=1)
        p = jnp.exp2(s_s[rows, pl.ds(start_b, tk)] - m_rep)
        l_s[rows, :] += jnp.sum(p, axis=-1, keepdims=True)
        acc_s[rows, :] += _dot(p.astype(BF16), v_late)

    lam = _lambda_from(lam_ref, lam_init)
    o = acc_s[0:tq, :] / l_s[0:tq, :] - lam * (acc_s[tq:2 * tq, :] / l_s[tq:2 * tq, :])
    r = lax.rsqrt(jnp.mean(o * o, axis=-1, keepdims=True) + EPS)
    od_ref[...] = ((o * r * gsub_ref[...]) * (1.0 - lam_init)).astype(od_ref.dtype)

    s = _dot_nt(mq_ref[...].astype(BF16), mk_ref[...].astype(BF16)) * (MEM_HD ** -0.5 * LOG2E)
    e = jnp.exp2(s - jnp.max(s, axis=-1, keepdims=True))
    om = _dot(e.astype(BF16), mv_ref[...].astype(BF16)) / jnp.sum(e, axis=-1, keepdims=True)
    om_ref[...] = om.astype(om_ref.dtype)


def _attn_prompt(u, mkv, lam_pack, gsub, n_batch, seq, lam_init, tq=1024):
    nq = seq // tq
    cq, ck, cv, cmq = (COL_Q // LANES, COL_K // LANES, COL_V // LANES, COL_MQ // LANES)
    return pl.pallas_call(
        functools.partial(_attn_prompt_kernel, tq=tq, lam_init=lam_init),
        grid=(n_batch, DIFF_HEADS, nq),
        in_specs=[
            pl.BlockSpec((tq, LANES), lambda b, h, i: (b * nq + i, cq + h)),
            pl.BlockSpec((seq, LANES), lambda b, h, i: (b, ck + h)),
            pl.BlockSpec((seq, LANES), lambda b, h, i: (b, cv + h)),
            pl.BlockSpec((tq, LANES), lambda b, h, i: (b * nq + i, cmq + h)),
            pl.BlockSpec((MEM_TOKENS, LANES), lambda b, h, i: (b, h)),
            pl.BlockSpec((MEM_TOKENS, LANES), lambda b, h, i: (b, MEM_HEADS + h)),
            pl.BlockSpec((8, LANES), lambda b, h, i: (0, 0)),
            pl.BlockSpec((1, LANES), lambda b, h, i: (0, 0)),
        ],
        out_specs=[
            pl.BlockSpec((tq, LANES), lambda b, h, i: (b * nq + i, h)),
            pl.BlockSpec((tq, LANES), lambda b, h, i: (b * nq + i, h)),
        ],
        out_shape=[
            jax.ShapeDtypeStruct((n_batch * seq, D_DIFF), BF16),
            jax.ShapeDtypeStruct((n_batch * seq, D_MEMX), BF16),
        ],
        scratch_shapes=[
            pltpu.VMEM((seq, LANES), BF16),
            pltpu.VMEM((seq, LANES), BF16),
            pltpu.VMEM((2 * tq, seq), F32),
            pltpu.VMEM((2 * tq, LANES), F32),
            pltpu.VMEM((2 * tq, LANES), F32),
            pltpu.VMEM((2 * tq, LANES), F32),
        ],
        compiler_params=_cparams(("parallel", "parallel", "arbitrary")),
        name="attn_prompt",
    )(u, u, u, u, mkv, mkv, lam_pack, gsub)


PAGE = 128
PAGE_ROWS = PAGE * DIFF_HEADS


def _per_head_rows(x_row, split_maps):
    lane = lax.broadcasted_iota(jnp.int32, (1, LANES), 1)
    rows = []
    for r in range(2 * DIFF_HEADS):
        xh = x_row[:, (r // 2) * LANES:(r // 2 + 1) * LANES]
        if split_maps:
            xh = jnp.where((lane < DIFF_DK) if r % 2 == 0 else (lane >= DIFF_DK), xh, 0.0)
        rows.append(xh)
    return jnp.concatenate(rows, axis=0)


def _head_match(n_cols):
    r_i = lax.broadcasted_iota(jnp.int32, (8, n_cols), 0)
    c_i = lax.broadcasted_iota(jnp.int32, (8, n_cols), 1)
    return jnp.bitwise_and(c_i, DIFF_HEADS - 1) == jnp.right_shift(r_i, 1)


def _rider_specs(step_fn, spb):
    W = DIFF_HEADS * LANES
    assert spb & (spb - 1) == 0, "steps per sequence must be a power of two (shift/mask indexing)"
    shift = spb.bit_length() - 1

    def seq(idx):
        return jnp.right_shift(step_fn(*idx[:-1]), shift)

    tok_q = pl.BlockSpec((ROWS_S, W), lambda *idx: (seq(idx), COL_Q // W))
    hbm = pl.BlockSpec(memory_space=pl.ANY)
    part = pl.BlockSpec((8, LANES), lambda *idx: (seq(idx), 0))
    return [tok_q, hbm, hbm], [part, part, part]


def _rider_scratch(npp):
    buf = pltpu.VMEM((2, npp, PAGE_ROWS, LANES), F32)
    return [buf, buf, pltpu.SemaphoreType.DMA((2, 2))]


def _rider_pages(pt_ref, ck_hbm, cv_hbm, kbuf, vbuf, sem, *, spb, npp, page_lo, page_base):
    shift = spb.bit_length() - 1

    def copies(slot, rows):
        out = []
        for i in range(npp):
            out.append(pltpu.make_async_copy(ck_hbm.at[pl.ds(rows[i], PAGE_ROWS), :], kbuf.at[slot, i],
                                             sem.at[0, slot]))
            out.append(pltpu.make_async_copy(cv_hbm.at[pl.ds(rows[i], PAGE_ROWS), :], vbuf.at[slot, i],
                                             sem.at[1, slot]))
        return out

    def start(step, slot):
        seq = jnp.right_shift(step, shift)
        col = page_lo + jnp.bitwise_and(step, spb - 1) * npp
        rows = [pl.multiple_of((page_base + pt_ref[seq, col + i]) * PAGE_ROWS, PAGE_ROWS) for i in range(npp)]
        for c in copies(slot, rows):
            c.start()

    def wait(slot):
        for c in copies(slot, [0] * npp):
            c.wait()

    return start, wait


def _rider_parts(step, spb, tok_q, k_refs, v_refs, pm_ref, pl_ref, pa_ref, m_s, l_s, acc_s):
    npp = len(k_refs)
    j = jnp.bitwise_and(step, spb - 1)

    def first():
        m_s[...] = jnp.full(m_s.shape, -jnp.inf, F32)
        l_s[...] = jnp.zeros(l_s.shape, F32)
        acc_s[...] = jnp.zeros(acc_s.shape, F32)

    def main():
        q = _per_head_rows(tok_q[0:1, :] * (DIFF_DK ** -0.5), True).astype(BF16)
        s_all = jnp.concatenate([_dot_nt(q, k_refs[i][...].astype(BF16)) for i in range(npp)], axis=1)
        s_all = jnp.where(_head_match(npp * PAGE_ROWS), s_all, -jnp.inf)
        m_prev = m_s[...]
        m_new = jnp.maximum(m_prev, jnp.max(s_all, axis=-1, keepdims=True))
        alpha = jnp.exp(m_prev - m_new)
        p = jnp.exp(s_all - m_new[:, 0:1])
        l_s[...] = alpha * l_s[...] + jnp.sum(p, axis=-1, keepdims=True)
        m_s[...] = m_new
        p_bf = p.astype(BF16)
        pv = _dot(p_bf[:, 0:PAGE_ROWS], v_refs[0][...].astype(BF16))
        for i in range(1, npp):
            pv = pv + _dot(p_bf[:, i * PAGE_ROWS:(i + 1) * PAGE_ROWS], v_refs[i][...].astype(BF16))
        acc_s[...] = alpha * acc_s[...] + pv

    def last():
        pm_ref[...] = m_s[...]
        pl_ref[...] = l_s[...]
        pa_ref[...] = acc_s[...]

    return j, first, main, last


def _decode_finish_kernel(tok_q, tok_k, tok_v, tok_mq, mk_ref, mv_ref, lam_ref, gsub_ref, *rest, n_parts, grp,
                          lam_init):
    parts = rest[:3 * n_parts]
    od_ref, om_ref = rest[3 * n_parts:]
    mem_rows = MEM_TOKENS * MEM_HEADS
    lam = _lambda_from(lam_ref, lam_init)
    for g in range(grp):
        t0 = g * ROWS_S
        r8 = slice(g * 8, (g + 1) * 8)
        qmat = _per_head_rows(tok_q[t0:t0 + 1, :] * (DIFF_DK ** -0.5), True)
        s_new = jnp.sum(qmat * _per_head_rows(tok_k[t0:t0 + 1, :], False), axis=-1, keepdims=True)
        m_tot = jnp.broadcast_to(s_new, (8, LANES))
        for h in range(n_parts):
            m_tot = jnp.maximum(m_tot, parts[3 * h][r8, :])
        w_new = jnp.exp(s_new - m_tot)
        l_tot = w_new
        acc = w_new * _per_head_rows(tok_v[t0:t0 + 1, :], False)
        for h in range(n_parts):
            w = jnp.exp(parts[3 * h][r8, :] - m_tot)
            l_tot = l_tot + w * parts[3 * h + 1][r8, :]
            acc = acc + w * parts[3 * h + 2][r8, :]
        o_all = acc / l_tot
        outs = []
        for h in range(DIFF_HEADS):
            o = o_all[2 * h:2 * h + 1, :] - lam * o_all[2 * h + 1:2 * h + 2, :]
            r = lax.rsqrt(jnp.mean(o * o, axis=-1, keepdims=True) + EPS)
            outs.append((o * r * gsub_ref[...]) * (1.0 - lam_init))
        od_ref[r8, :] = jnp.broadcast_to(jnp.concatenate(outs, axis=1), (8, DIFF_HEADS * LANES))

        mrows = slice(g * mem_rows, (g + 1) * mem_rows)
        mqm = _per_head_rows(tok_mq[t0:t0 + 1, :], False)
        s = _dot_nt(mqm.astype(BF16), mk_ref[mrows, :].astype(BF16)) * (MEM_HD ** -0.5)
        s = jnp.where(_head_match(s.shape[1]), s, -jnp.inf)
        e = jnp.exp(s - jnp.max(s, axis=-1, keepdims=True))
        om = _dot(e.astype(BF16), mv_ref[mrows, :].astype(BF16)) / jnp.sum(e, axis=-1, keepdims=True)
        om_row = jnp.concatenate([om[2 * h:2 * h + 1, :] for h in range(MEM_HEADS)], axis=1)
        om_ref[r8, :] = jnp.broadcast_to(om_row, (8, MEM_HEADS * LANES))


def _decode_finish(u_s, parts, mem_k, mem_v, lam_pack, gsub, layer, n_b, lam_init, grp=4):
    W = DIFF_HEADS * LANES
    mem_rows = MEM_TOKENS * MEM_HEADS
    flat = [a for tri in parts for a in tri]
    assert n_b % grp == 0
    nblk = n_b // grp

    def tok(col):
        return pl.BlockSpec((grp * ROWS_S, W), lambda b: (b, col // W))

    od, om = pl.pallas_call(
        functools.partial(_decode_finish_kernel, n_parts=len(parts), grp=grp, lam_init=lam_init),
        grid=(nblk,),
        in_specs=[tok(COL_Q), tok(COL_K), tok(COL_V), tok(COL_MQ),
                  pl.BlockSpec((grp * mem_rows, LANES), lambda b: (layer * nblk + b, 0)),
                  pl.BlockSpec((grp * mem_rows, LANES), lambda b: (layer * nblk + b, 0)),
                  pl.BlockSpec((8, LANES), lambda b: (0, 0)),
                  pl.BlockSpec((1, LANES), lambda b: (0, 0))]
                 + [pl.BlockSpec((grp * 8, LANES), lambda b: (b, 0))] * len(flat),
        out_specs=[pl.BlockSpec((grp * 8, W), lambda b: (b, 0)), pl.BlockSpec((grp * 8, W), lambda b: (b, 0))],
        out_shape=[jax.ShapeDtypeStruct((n_b * 8, W), F32), jax.ShapeDtypeStruct((n_b * 8, W), F32)],
        compiler_params=_cparams(("parallel",)),
        name="decode_finish",
    )(u_s, u_s, u_s, u_s, mem_k, mem_v, lam_pack, gsub, *flat)
    return od.reshape(n_b, 8, W)[:, 0], om.reshape(n_b, 8, W)[:, 0]


def _outproj_kernel(x_ref, y_ref, od_ref, om_ref, w1_ref, w2_ref, w3_ref, o_ref):
    o_ref[...] = (x_ref[...]
                  + _dot(y_ref[...].astype(BF16), w1_ref[...])
                  + _dot(od_ref[...].astype(BF16), w2_ref[...])
                  + _dot(om_ref[...].astype(BF16), w3_ref[...]))


def _outproj(x, y, od, om, w_out, tm=512, tn=D_MODEL):
    m = x.shape[0]
    tm = min(tm, m)
    assert m % tm == 0
    return pl.pallas_call(
        _outproj_kernel,
        grid=(m // tm, D_MODEL // tn),
        in_specs=[
            pl.BlockSpec((tm, tn), lambda i, j: (i, j)),
            pl.BlockSpec((tm, D_SSM), lambda i, j: (i, 0)),
            pl.BlockSpec((tm, D_DIFF), lambda i, j: (i, 0)),
            pl.BlockSpec((tm, D_MEMX), lambda i, j: (i, 0)),
            pl.BlockSpec((D_SSM, tn), lambda i, j: (0, j)),
            pl.BlockSpec((D_DIFF, tn), lambda i, j: (D_SSM // D_DIFF, j)),
            pl.BlockSpec((D_MEMX, tn), lambda i, j: ((D_SSM + D_DIFF) // D_MEMX, j)),
        ],
        out_specs=pl.BlockSpec((tm, tn), lambda i, j: (i, j)),
        out_shape=jax.ShapeDtypeStruct((m, D_MODEL), F32),
        compiler_params=_cparams(("parallel", "arbitrary")),
        name="out_proj",
    )(x, y, od, om, w_out, w_out, w_out)


def _mlp_kernel(*refs, rider):
    if rider:
        npp, spb = rider["npp"], rider["spb"]
        pt_ref, x_ref, g_ref, wu_ref, wd_ref, tok_q, ck_hbm, cv_hbm = refs[:8]
        o_ref, pm_ref, pl_ref, pa_ref, h_ref, m_s, l_s, acc_s, kbuf, vbuf, sem = refs[8:]
    else:
        x_ref, g_ref, wu_ref, wd_ref, o_ref, h_ref = refs
    f = pl.program_id(1)

    @pl.when(f == 0)
    def _():
        x = x_ref[...]
        r = lax.rsqrt(jnp.mean(x * x, axis=-1, keepdims=True) + EPS)
        h_ref[...] = (x * r * g_ref[...]).astype(BF16)
        o_ref[...] = x

    if not rider:
        a = jnp.maximum(_dot(h_ref[...], wu_ref[...]), 0.0)
        o_ref[...] += _dot((a * a).astype(BF16), wd_ref[...])
        return

    rounds = rider["rounds"]
    n_steps = pl.num_programs(0) * pl.num_programs(1)
    n_rounds = n_steps * rounds
    step = pl.program_id(0) * pl.num_programs(1) + f
    start, wait = _rider_pages(pt_ref, ck_hbm, cv_hbm, kbuf, vbuf, sem, spb=spb, npp=npp,
                               page_lo=rider["page_lo"], page_base=rider["page_base"])
    ts = [step * rounds + r for r in range(rounds)]
    slots = [jnp.bitwise_and(t, 1) for t in ts]
    parts = []
    for t, slot in zip(ts, slots):
        k_refs = [kbuf.at[slot, i] for i in range(npp)]
        v_refs = [vbuf.at[slot, i] for i in range(npp)]
        parts.append(_rider_parts(t, spb, tok_q, k_refs, v_refs, pm_ref, pl_ref, pa_ref, m_s, l_s, acc_s))

    def run_round(r):
        wait(slots[r])
        parts[r][2]()
        t2 = ts[r] + 2
        start(jnp.where(t2 >= n_rounds, t2 - n_rounds, t2), slots[r])

    @pl.when(step == 0)
    def _():
        start(0, 0)
        start(1, 1)

    pl.when(parts[0][0] == 0)(parts[0][1])
    run_round(0)
    a = jnp.maximum(_dot(h_ref[...], wu_ref[...]), 0.0)
    if rounds == 2:
        run_round(1)
    o_ref[...] += _dot((a * a).astype(BF16), wd_ref[...])
    pl.when(parts[-1][0] == spb - 1)(parts[-1][3])

    @pl.when(step == n_steps - 1)
    def _():
        wait(0)
        wait(1)


def _mlp(x, g, w_up, w_down, tm=512, tf=1024, rider=None):
    m = x.shape[0]
    d_ff = w_up.shape[1]
    tm = min(tm, m)
    assert m % tm == 0 and d_ff % tf == 0
    grid = (m // tm, d_ff // tf)
    in_specs = [
        pl.BlockSpec((tm, D_MODEL), lambda i, f: (i, 0)),
        pl.BlockSpec((1, D_MODEL), lambda i, f: (0, 0)),
        pl.BlockSpec((D_MODEL, tf), lambda i, f: (0, f)),
        pl.BlockSpec((tf, D_MODEL), lambda i, f: (f, 0)),
    ]
    out_specs = [pl.BlockSpec((tm, D_MODEL), lambda i, f: (i, 0))]
    out_shape = [jax.ShapeDtypeStruct((m, D_MODEL), F32)]
    scratch = [pltpu.VMEM((tm, D_MODEL), BF16)]
    args = [x, g, w_up, w_down]
    if rider is None:
        out = pl.pallas_call(
            functools.partial(_mlp_kernel, rider=None),
            grid=grid, in_specs=in_specs, out_specs=out_specs, out_shape=out_shape, scratch_shapes=scratch,
            compiler_params=_cparams(("parallel", "arbitrary")), name="mlp",
        )(*args)
        return out[0]
    u_s, cache_k, cache_v, page_table, layer, n_pool, npp, page_lo, n_pages = rider
    n_b = page_table.shape[0]
    spb = n_pages // npp
    n_steps = grid[0] * grid[1]
    rounds = n_b * spb // n_steps
    assert n_pages % npp == 0 and rounds in (1, 2) and n_steps * rounds == n_b * spb and spb % rounds == 0
    nf = grid[1]
    r_in, r_out = _rider_specs(lambda i, f: i * nf + f, spb // rounds)
    host_in = [pl.BlockSpec(s.block_shape, lambda i, f, pt, im=s.index_map: im(i, f)) for s in in_specs]
    host_in[0] = pl.BlockSpec(host_in[0].block_shape, host_in[0].index_map, pipeline_mode=pl.Buffered(1))
    host_out = [pl.BlockSpec(s.block_shape, lambda i, f, pt, im=s.index_map: im(i, f)) for s in out_specs]
    grid_spec = pltpu.PrefetchScalarGridSpec(
        num_scalar_prefetch=1, grid=grid,
        in_specs=host_in + r_in, out_specs=host_out + r_out,
        scratch_shapes=scratch + [pltpu.VMEM((8, LANES), F32)] * 3 + _rider_scratch(npp),
    )
    part_shape = jax.ShapeDtypeStruct((n_b * 8, LANES), F32)
    rider_cfg = dict(npp=npp, spb=spb, rounds=rounds, page_lo=page_lo, page_base=layer * n_pool)
    out, pm, pl_, pa = pl.pallas_call(
        functools.partial(_mlp_kernel, rider=rider_cfg),
        grid_spec=grid_spec,
        out_shape=out_shape + [part_shape] * 3,
        compiler_params=_cparams(("arbitrary", "arbitrary")),
        name="mlp_rider",
    )(page_table, *args, u_s, cache_k, cache_v)
    return out, (pm, pl_, pa)


def _pad_lanes(v, width=LANES):
    v = v.reshape(1, -1).astype(F32)
    return jnp.pad(v, ((0, 0), (0, width - v.shape[1])))


def kernel(x_prompt, x_sample, mem_prompt, cache_diff_k, cache_diff_v, cache_mem_k, cache_mem_v, state_conv, state_ssm, page_table, norm_mix, w_in, conv_w, conv_b, dt_bias, a_log, d_skip, g_ssm, g_q, g_k, lambda_q1, lambda_k1, lambda_q2, lambda_k2, g_subln, norm_mem, w_mem_kv, g_mq, g_mk, w_out, norm_mlp, w_up, w_down):
    depth = w_in.shape[0]
    bp, seq, _ = x_prompt.shape
    bd, dec_seq, _ = x_sample.shape
    assert dec_seq == 1 and seq % SSD_T == 0
    n_pool = cache_diff_k.shape[1]

    xp = x_prompt.reshape(bp * seq, D_MODEL)
    xs = x_sample.reshape(bd, D_MODEL)
    cache_k = cache_diff_k.reshape(depth * n_pool * PAGE_ROWS, LANES)
    cache_v = cache_diff_v.reshape(depth * n_pool * PAGE_ROWS, LANES)
    mem_k = cache_mem_k.reshape(depth * bd * MEM_TOKENS * MEM_HEADS, MEM_HD)
    mem_v = cache_mem_v.reshape(depth * bd * MEM_TOKENS * MEM_HEADS, MEM_HD)

    outs = [[] for _ in range(10)]
    for l in range(depth):
        lam_init = 0.8 - 0.6 * math.exp(-0.3 * l)
        wl_t = jnp.swapaxes(w_in[l], 0, 1).astype(BF16)
        o_dt = D_SSM + CONV_DIM
        o_q = o_dt + SSM_HEADS
        w_main = [(wl_t, o_dt // PROJ_TN, True), (wl_t[o_q:], (N_MAIN - o_dt) // PROJ_TN, True)]
        w_dt = jnp.pad(wl_t[o_dt:o_q], ((0, LANES - SSM_HEADS), (0, 0)))
        ones = jnp.ones((PROJ_TN,), F32)
        gains = jnp.concatenate(
            [ones] * 5 + [jnp.tile(g_q[l], 2 * DIFF_HEADS), jnp.tile(g_k[l], 2 * DIFF_HEADS), ones,
                          jnp.tile(g_mq[l], MEM_HEADS)]).reshape(1, N_MAIN)
        modes = (0, 0, 0, 0, 0, DIFF_DK, DIFF_DK, 0, MEM_HD)
        g_mix = norm_mix[l].reshape(1, D_MODEL)

        ssm_p = (jnp.pad(conv_w[l], ((0, 8 - CONV_W), (0, 0))),
                 conv_b[l].reshape(1, CONV_DIM),
                 _pad_lanes(dt_bias[l]), _pad_lanes(a_log[l]),
                 jnp.repeat(d_skip[l], SSM_HEAD_DIM).reshape(1, D_SSM),
                 g_ssm[l].reshape(1, D_SSM))
        lam_pack = jnp.concatenate(
            [_pad_lanes(lambda_q1[l]), _pad_lanes(lambda_k1[l]), _pad_lanes(lambda_q2[l]),
             _pad_lanes(lambda_k2[l]), jnp.zeros((4, LANES), F32)], axis=0)
        gsub = g_subln[l].reshape(1, LANES)
        w_out_bf = w_out[l].astype(BF16)
        w_up_bf = w_up[l].astype(BF16)
        w_down_bf = w_down[l].astype(BF16)
        g_mlp = norm_mlp[l].reshape(1, D_MODEL)

        xs_pad = jnp.pad(xs.reshape(bd, 1, D_MODEL), ((0, 0), (0, ROWS_S - 1), (0, 0))).reshape(bd * ROWS_S, D_MODEL)
        u_s, dt_s = _norm_proj(xs_pad, g_mix, w_main, gains, modes, w_dt=w_dt)

        u_p, dt_p, k_rows, v_rows = _norm_proj(xp, g_mix, w_main, gains, modes, w_dt=w_dt,
                                               rows_blocks=(COL_K // PROJ_TN, COL_V // PROJ_TN))
        y_p, conv_p, h_p = _ssd(u_p, dt_p, bp, seq, seq, ssm_p)
        mem_gains = jnp.concatenate([jnp.tile(g_mk[l], MEM_HEADS), ones]).reshape(1, 2 * D_MEMX)
        mkv, = _norm_proj(mem_prompt.reshape(bp * MEM_TOKENS, D_MODEL), norm_mem[l].reshape(1, D_MODEL),
                          [(w_mem_kv[l], 2 * D_MEMX // PROJ_TN, False)], mem_gains, (MEM_HD, 0))
        od_p, om_p = _attn_prompt(u_p, mkv, lam_pack, gsub, bp, seq, lam_init)
        xp_mid = _outproj(xp, y_p, od_p, om_p, w_out_bf)
        n_pages = page_table.shape[1]
        xp_new, part = _mlp(xp_mid, g_mlp, w_up_bf, w_down_bf, tm=1024, tf=512,
                            rider=(u_s, cache_k, cache_v, page_table, l, n_pool, 16, 0, n_pages))

        outs[0].append(k_rows.reshape(bp, seq, DIFF_HEADS, LANES))
        outs[1].append(v_rows.reshape(bp, seq, DIFF_HEADS, LANES))
        outs[4].append(mkv[:, :D_MEMX].reshape(bp, MEM_TOKENS, MEM_HEADS, MEM_HD))
        outs[5].append(mkv[:, D_MEMX:].reshape(bp, MEM_TOKENS, MEM_HEADS, MEM_HD))
        outs[6].append(conv_p[:, :CONV_W - 1])
        outs[7].append(h_p.reshape(bp, SSM_HEADS, SSM_HEAD_DIM, D_STATE))

        conv_prev = jnp.pad(state_conv[l], ((0, 0), (8 - (CONV_W - 1), 0), (0, 0)))
        y_s, conv_s, h_s = _ssd(u_s, dt_s, bd, ROWS_S, 1, ssm_p, conv_prev=conv_prev,
                                h0=state_ssm[l].reshape(bd, D_SSM, D_STATE))
        od_s, om_s = _decode_finish(u_s, [part], mem_k, mem_v, lam_pack, gsub, l, bd, lam_init)
        y_s0 = y_s.reshape(bd, ROWS_S, D_SSM)[:, 0]
        xs_mid = _outproj(xs, y_s0, od_s, om_s, w_out_bf)
        xs_new = _mlp(xs_mid, g_mlp, w_up_bf, w_down_bf)
        u_s0 = u_s.reshape(bd, ROWS_S, N_MAIN)[:, 0]
        outs[2].append(u_s0[:, COL_K:COL_K + D_DIFF].reshape(bd, 1, DIFF_HEADS, LANES))
        outs[3].append(u_s0[:, COL_V:COL_V + D_DIFF].reshape(bd, 1, DIFF_HEADS, LANES))
        outs[8].append(conv_s[:, :CONV_W - 1])
        outs[9].append(h_s.reshape(bd, SSM_HEADS, SSM_HEAD_DIM, D_STATE))

        xp, xs = xp_new, xs_new

    st = [jnp.stack(o) for o in outs]
    return (xp.reshape(bp, seq, D_MODEL), xs.reshape(bd, 1, D_MODEL),
            st[0], st[1], st[2], st[3], st[4], st[5], st[6], st[7], st[8], st[9])
```

```python
import functools
import math

import jax
import jax.numpy as jnp
from jax import lax
from jax.experimental import pallas as pl
from jax.experimental.pallas import tpu as pltpu

F32 = jnp.float32
BF16 = jnp.bfloat16

D_MODEL = 2048
D_SSM = 1024
D_DIFF = 512
D_MEMX = 512
SSM_HEADS = 16
SSM_HEAD_DIM = 64
SSM_GROUPS = 2
D_STATE = 128
CONV_W = 4
BC_DIM = 2 * SSM_GROUPS * D_STATE
CONV_DIM = D_SSM + BC_DIM
DIFF_HEADS = 4
DIFF_DK = 64
MEM_HEADS = 4
MEM_HD = 128
MEM_TOKENS = 256
D_FF = 4 * D_MODEL
EPS = 1e-6
LOG2E = math.log2(math.e)

LANES = 128
SUBLANES_BF16 = 16
VMEM_LIMIT_BYTES = 58 * 1024 * 1024

N_MAIN = D_SSM + D_SSM + BC_DIM + 3 * D_DIFF + D_MEMX
COL_Z, COL_XS, COL_BC, COL_Q, COL_K, COL_V, COL_MQ = 0, 1024, 2048, 2560, 3072, 3584, 4096
PROJ_TN = 512
SSD_T = 128
ROWS_S = SUBLANES_BF16


def _cparams(sem):
    return pltpu.CompilerParams(dimension_semantics=sem, vmem_limit_bytes=VMEM_LIMIT_BYTES)


def _dot(a, b):
    return jnp.dot(a, b, preferred_element_type=F32)


def _dot_nt(a, b):
    return lax.dot_general(a, b, (((1,), (1,)), ((), ())), preferred_element_type=F32)


def _silu(x):
    return x * (1.0 / (1.0 + jnp.exp(-x)))


def _group_rmsnorm_slab(a, gain, group):
    sq = a * a
    if group == LANES:
        r = lax.rsqrt(jnp.sum(sq, axis=-1, keepdims=True) * (1.0 / LANES) + EPS)
    else:
        lane = lax.broadcasted_iota(jnp.int32, a.shape, 1)
        lo = lane < group
        s_lo = jnp.sum(jnp.where(lo, sq, 0.0), axis=-1, keepdims=True)
        s_hi = jnp.sum(jnp.where(lo, 0.0, sq), axis=-1, keepdims=True)
        r = jnp.where(lo, lax.rsqrt(s_lo * (1.0 / group) + EPS),
                      lax.rsqrt(s_hi * (1.0 / group) + EPS))
    return a * r * gain


def _proj_kernel(*refs, modes, seg_of, w_trans, has_dt, rows_blocks):
    n_w = len(w_trans)
    x_ref, g_ref = refs[:2]
    w_refs = refs[2:2 + n_w]
    gain_ref = refs[2 + n_w]
    pos = 3 + n_w
    if has_dt:
        wdt_ref = refs[pos]
        pos += 1
    u_ref = refs[pos]
    pos += 1
    if has_dt:
        dt_ref = refs[pos]
        pos += 1
    rows_refs = refs[pos:pos + len(rows_blocks)]
    h_ref = refs[pos + len(rows_blocks)]
    j = pl.program_id(1)
    tm = x_ref.shape[0]

    @pl.when(j == 0)
    def _():
        x = x_ref[...]
        r = lax.rsqrt(jnp.mean(x * x, axis=-1, keepdims=True) + EPS)
        h_ref[...] = (x * r * g_ref[...]).astype(BF16)
        if has_dt:
            dt_ref[...] = _dot_nt(h_ref[...], wdt_ref[...])

    keys = {}
    for jj, m in enumerate(modes):
        ri = rows_blocks.index(jj) if jj in rows_blocks else None
        keys.setdefault((seg_of[jj], m, ri), []).append(jj)
    for (seg, mode, ri), jjs in keys.items():
        cond = j == jjs[0]
        for jj in jjs[1:]:
            cond = jnp.logical_or(cond, j == jj)

        @pl.when(cond)
        def _(seg=seg, mode=mode, ri=ri):
            w = w_refs[seg][...].astype(BF16)
            acc = _dot_nt(h_ref[...], w) if w_trans[seg] else _dot(h_ref[...], w)
            if mode == 0 and ri is None:
                u_ref[...] = acc
                return
            gain = gain_ref[...]
            for s in range(acc.shape[1] // LANES):
                sl = slice(s * LANES, (s + 1) * LANES)
                slab = acc[:, sl]
                if mode:
                    slab = _group_rmsnorm_slab(slab, gain[:, sl], mode)
                u_ref[:, sl] = slab
                if ri is not None:
                    rows_refs[ri][pl.ds(s, tm, stride=PROJ_TN // LANES), :] = slab


def _norm_proj(x, g, ws, gains, modes, w_dt=None, rows_blocks=(), tm=1024):
    m, k = x.shape
    tm = min(tm, m)
    nblk = [nb for _, nb, _ in ws]
    w_trans = tuple(t for _, _, t in ws)
    ws = [w for w, _, _ in ws]
    assert m % tm == 0 and sum(nblk) == len(modes)
    assert all(w.shape[0 if t else 1] >= nb * PROJ_TN for w, nb, t in zip(ws, nblk, w_trans))
    seg_of, offs = [], []
    for a, nb in enumerate(nblk):
        offs.append(len(seg_of))
        seg_of += [a] * nb
    n = PROJ_TN * len(modes)
    has_dt = w_dt is not None

    def w_spec(a):
        def blk(j):
            return jnp.minimum(jnp.maximum(j - offs[a], 0), nblk[a] - 1)
        if w_trans[a]:
            return pl.BlockSpec((PROJ_TN, k), lambda i, j: (blk(j), 0))
        return pl.BlockSpec((k, PROJ_TN), lambda i, j: (0, blk(j)))

    in_specs = [pl.BlockSpec((tm, k), lambda i, j: (i, 0)), pl.BlockSpec((1, k), lambda i, j: (0, 0))]
    in_specs += [w_spec(a) for a in range(len(ws))]
    in_specs.append(pl.BlockSpec((1, PROJ_TN), lambda i, j: (0, j)))
    args = [x, g, *ws, gains]
    out_shape = [jax.ShapeDtypeStruct((m, n), F32)]
    out_specs = [pl.BlockSpec((tm, PROJ_TN), lambda i, j: (i, j))]
    if has_dt:
        in_specs.append(pl.BlockSpec((LANES, k), lambda i, j: (0, 0)))
        args.append(w_dt)
        out_shape.append(jax.ShapeDtypeStruct((m, LANES), F32))
        out_specs.append(pl.BlockSpec((tm, LANES), lambda i, j: (i, 0)))
    heads = PROJ_TN // LANES
    for _ in rows_blocks:
        out_shape.append(jax.ShapeDtypeStruct((m * heads, LANES), F32))
        out_specs.append(pl.BlockSpec((tm * heads, LANES), lambda i, j: (i, 0)))
    return pl.pallas_call(
        functools.partial(_proj_kernel, modes=tuple(modes), seg_of=tuple(seg_of), w_trans=w_trans, has_dt=has_dt,
                          rows_blocks=tuple(rows_blocks)),
        grid=(m // tm, len(modes)),
        in_specs=in_specs,
        out_specs=out_specs,
        out_shape=out_shape,
        scratch_shapes=[pltpu.VMEM((tm, k), BF16)],
        compiler_params=_cparams(("parallel", "arbitrary")),
        name="norm_proj",
    )(*args)


def _split3(x):
    hi = x.astype(BF16)
    r1 = x - hi.astype(F32)
    mid = r1.astype(BF16)
    lo = (r1 - mid.astype(F32)).astype(BF16)
    return hi, mid, lo


def _ssd_kernel(*refs, t_in, valid_last, nc, has_init):
    if has_init:
        (xs_ref, z_ref, bc_ref, dt_ref, cprev_ref, h0_ref, convw_ref, convb_ref, dtb_ref, alog_ref,
         dskip_ref, gssm_ref, y_ref, cout_ref, hout_ref, xpad, hst) = refs
    else:
        (xs_ref, z_ref, bc_ref, dt_ref, convw_ref, convb_ref, dtb_ref, alog_ref,
         dskip_ref, gssm_ref, y_ref, cout_ref, hout_ref, xpad, hst) = refs
    T = SSD_T
    c = pl.program_id(1)
    single_step = nc == 1 and valid_last == 1

    @pl.when(c == 0)
    def _():
        if has_init:
            xpad[0:8, :] = cprev_ref[0]
            hst[...] = h0_ref[0]
        else:
            xpad[0:8, :] = jnp.zeros((8, CONV_DIM), F32)
            hst[...] = jnp.zeros(hst.shape, F32)

    def rows(ref):
        v = ref[...]
        if t_in < T:
            v = jnp.concatenate([v, jnp.zeros((T - t_in, v.shape[1]), F32)], axis=0)
        return v

    xpad[8:8 + T, 0:D_SSM] = rows(xs_ref)
    xpad[8:8 + T, D_SSM:CONV_DIM] = rows(bc_ref)
    z = rows(z_ref)
    dt_raw = rows(dt_ref)

    live = 8 if single_step else T
    convw = convw_ref[...]
    conv = convb_ref[...] + xpad[5:5 + live, :] * convw[0:1, :]
    for jtap in range(1, CONV_W):
        conv = conv + xpad[5 + jtap:5 + jtap + live, :] * convw[jtap:jtap + 1, :]
    xact = _silu(conv)
    z_gate = _silu(z[0:live, :])
    if live < T:
        xact = jnp.concatenate([xact, jnp.zeros((T - live, CONV_DIM), F32)], axis=0)
        z_gate = jnp.concatenate([z_gate, jnp.zeros((T - live, D_SSM), F32)], axis=0)

    row_i = lax.broadcasted_iota(jnp.int32, (T, LANES), 0)
    col_i = lax.broadcasted_iota(jnp.int32, (T, LANES), 1)
    xv = dt_raw + dtb_ref[...]
    dt = jnp.maximum(xv, 0.0) + jnp.log1p(jnp.exp(-jnp.abs(xv)))
    if valid_last < T:
        dt = jnp.where(row_i < valid_last, dt, 0.0)
    a_neg = -jnp.exp(alog_ref[...])
    adt = dt * a_neg

    tril = (row_i >= col_i)
    tril_bf = jnp.where(tril, 1.0, 0.0).astype(BF16)
    a_hi, a_mid, a_lo = _split3(adt)
    acum = _dot(tril_bf, a_hi) + _dot(tril_bf, a_mid) + _dot(tril_bf, a_lo)
    acum_t = acum.T
    dt_t = dt.T
    e_acum = jnp.exp(acum)
    a_last = acum[T - 1:T, :]
    w_state = jnp.exp(a_last - acum) * dt
    da_last = jnp.exp(a_last)

    lo_half = col_i < SSM_HEAD_DIM
    neg_big = jnp.float32(-1e30)

    def colb(tile, r):
        return jnp.broadcast_to(tile[:, r:r + 1], (T, LANES))

    y_slabs = []
    for g in range(SSM_GROUPS):
        b_g = xact[:, D_SSM + g * D_STATE:D_SSM + (g + 1) * D_STATE]
        c_g = xact[:, D_SSM + SSM_GROUPS * D_STATE + g * D_STATE:
                   D_SSM + SSM_GROUPS * D_STATE + (g + 1) * D_STATE]
        b_bf = b_g.astype(BF16)
        c_bf = c_g.astype(BF16)
        if single_step:
            cb00 = jnp.sum(c_g[0:1, :] * b_g[0:1, :], axis=-1, keepdims=True)
        else:
            cb = _dot_nt(c_bf, b_bf)
        for pp in range(SSM_HEADS // SSM_GROUPS // 2):
            p = g * (SSM_HEADS // SSM_GROUPS // 2) + pp
            sl = slice(p * LANES, (p + 1) * LANES)
            xs_slab = xact[:, sl]
            y_acc = dskip_ref[:, sl] * xs_slab
            if single_step:
                dt_row = jnp.where(lo_half[0:1, :], dt[0:1, 2 * p:2 * p + 1], dt[0:1, 2 * p + 1:2 * p + 2])
                y_acc = y_acc + (cb00 * dt_row) * xs_slab
            for hh in range(0 if single_step else 2):
                r = 2 * p + hh
                seg = colb(acum, r) - acum_t[r:r + 1, :]
                lmat = jnp.exp(jnp.where(tril, seg, neg_big))
                mr = (cb * lmat * dt_t[r:r + 1, :]).astype(BF16)
                xh = jnp.where(lo_half if hh == 0 else jnp.logical_not(lo_half), xs_slab, 0.0)
                y_acc = y_acc + _dot(mr, xh.astype(BF16))
            hpair = hst[sl, :]
            e_pair = jnp.where(lo_half, colb(e_acum, 2 * p), colb(e_acum, 2 * p + 1))
            y_acc = y_acc + e_pair * _dot_nt(c_bf, hpair.astype(BF16))
            w_pair = jnp.where(lo_half, colb(w_state, 2 * p), colb(w_state, 2 * p + 1))
            xw_t = (xs_slab * w_pair).T
            st = _dot(xw_t.astype(BF16), b_bf)
            da = jnp.concatenate(
                [jnp.broadcast_to(da_last[:, 2 * p:2 * p + 1], (SSM_HEAD_DIM, LANES)),
                 jnp.broadcast_to(da_last[:, 2 * p + 1:2 * p + 2], (SSM_HEAD_DIM, LANES))], axis=0)
            hst[sl, :] = da * hpair + st
            y_slabs.append((y_acc * z_gate[:, sl])[0:t_in, :])

    per_group = D_SSM // SSM_GROUPS // LANES
    for g in range(SSM_GROUPS):
        slabs = y_slabs[g * per_group:(g + 1) * per_group]
        ssum = jnp.sum(slabs[0] * slabs[0], axis=-1, keepdims=True)
        for s in slabs[1:]:
            ssum = ssum + jnp.sum(s * s, axis=-1, keepdims=True)
        r = lax.rsqrt(ssum * (1.0 / (per_group * LANES)) + EPS)
        for k, s in enumerate(slabs):
            sl = slice((g * per_group + k) * LANES, (g * per_group + k + 1) * LANES)
            y_ref[:, sl] = (s * r * gssm_ref[:, sl]).astype(y_ref.dtype)

    if nc > 1:
        xpad[5:8, :] = xpad[5 + T:8 + T, :]

    @pl.when(c == nc - 1)
    def _():
        cout_ref[...] = jnp.zeros(cout_ref.shape, F32)
        if nc > 1:
            cout_ref[0, 0:CONV_W - 1, :] = xpad[5:8, :]
        else:
            cout_ref[0, 0:CONV_W - 1, :] = xpad[5 + valid_last:8 + valid_last, :]
        hout_ref[0] = hst[...]


def _ssd(u, dt, n_batch, seq_rows, valid_len, ssm_p, conv_prev=None, h0=None):
    t_in = min(seq_rows, SSD_T)
    nc = max(seq_rows // SSD_T, 1)
    valid_last = valid_len - (nc - 1) * SSD_T
    has_init = conv_prev is not None
    convw, convb, dtb, alog, dskip, gssm = ssm_p

    def row(b, c):
        return b * nc + c

    in_specs = [
        pl.BlockSpec((t_in, D_SSM), lambda b, c: (row(b, c), COL_XS // D_SSM)),
        pl.BlockSpec((t_in, D_SSM), lambda b, c: (row(b, c), COL_Z // D_SSM)),
        pl.BlockSpec((t_in, BC_DIM), lambda b, c: (row(b, c), COL_BC // BC_DIM)),
        pl.BlockSpec((t_in, LANES), lambda b, c: (row(b, c), 0)),
    ]
    args = [u, u, u, dt]
    if has_init:
        in_specs += [pl.BlockSpec((1, 8, CONV_DIM), lambda b, c: (b, 0, 0)),
                     pl.BlockSpec((1, D_SSM, D_STATE), lambda b, c: (b, 0, 0))]
        args += [conv_prev, h0]
    in_specs += [
        pl.BlockSpec((8, CONV_DIM), lambda b, c: (0, 0)),
        pl.BlockSpec((1, CONV_DIM), lambda b, c: (0, 0)),
        pl.BlockSpec((1, LANES), lambda b, c: (0, 0)),
        pl.BlockSpec((1, LANES), lambda b, c: (0, 0)),
        pl.BlockSpec((1, D_SSM), lambda b, c: (0, 0)),
        pl.BlockSpec((1, D_SSM), lambda b, c: (0, 0)),
    ]
    args += [convw, convb, dtb, alog, dskip, gssm]
    return pl.pallas_call(
        functools.partial(_ssd_kernel, t_in=t_in, valid_last=valid_last, nc=nc, has_init=has_init),
        grid=(n_batch, nc),
        in_specs=in_specs,
        out_specs=[
            pl.BlockSpec((t_in, D_SSM), lambda b, c: (row(b, c), 0)),
            pl.BlockSpec((1, 8, CONV_DIM), lambda b, c: (b, 0, 0)),
            pl.BlockSpec((1, D_SSM, D_STATE), lambda b, c: (b, 0, 0)),
        ],
        out_shape=[
            jax.ShapeDtypeStruct((n_batch * seq_rows, D_SSM), BF16),
            jax.ShapeDtypeStruct((n_batch, 8, CONV_DIM), F32),
            jax.ShapeDtypeStruct((n_batch, D_SSM, D_STATE), F32),
        ],
        scratch_shapes=[pltpu.VMEM((8 + SSD_T, CONV_DIM), F32), pltpu.VMEM((D_SSM, D_STATE), F32)],
        compiler_params=_cparams(("parallel", "arbitrary")),
        name="ssd_scan",
    )(*args)


def _lambda_from(lam_ref, lam_init):
    lp = lam_ref[...]
    s1 = jnp.sum(lp[0:1, :] * lp[1:2, :], axis=-1, keepdims=True)
    s2 = jnp.sum(lp[2:3, :] * lp[3:4, :], axis=-1, keepdims=True)
    return jnp.exp(s1) - jnp.exp(s2) + lam_init


def _attn_prompt_kernel(q_ref, k_ref, v_ref, mq_ref, mk_ref, mv_ref, lam_ref, gsub_ref,
                        od_ref, om_ref, kb, vb, s_s, m_s, l_s, acc_s, *, tq, lam_init):
    qi = pl.program_id(2)

    @pl.when(qi == 0)
    def _():
        kb[...] = k_ref[...].astype(BF16)
        vb[...] = v_ref[...].astype(BF16)

    lane = lax.broadcasted_iota(jnp.int32, (tq, LANES), 1)
    lo = lane < DIFF_DK
    q = q_ref[...] * (DIFF_DK ** -0.5 * LOG2E)
    q2 = jnp.concatenate([jnp.where(lo, q, 0.0), jnp.where(lo, 0.0, q)], axis=0).astype(BF16)
    m_s[...] = jnp.full(m_s.shape, -jnp.inf, F32)
    l_s[...] = jnp.zeros(l_s.shape, F32)
    acc_s[...] = jnp.zeros(acc_s.shape, F32)
    tk = tq // 2
    reps = tk // LANES
    late = (slice(tk, tq), slice(tq + tk, 2 * tq))

    def causal(s):
        r_i = jnp.bitwise_and(lax.broadcasted_iota(jnp.int32, s.shape, 0), s.shape[0] // 2 - 1)
        c_i = lax.broadcasted_iota(jnp.int32, s.shape, 1)
        return jnp.where(r_i >= c_i, s, -jnp.inf)

    def scores(j, masked):
        start = pl.multiple_of(j * tk, tk)
        s = _dot_nt(q2, kb[pl.ds(start, tk), :])
        if masked:
            s = causal(s)
        s_s[:, pl.ds(start, tk)] = s
        m_s[...] = jnp.maximum(m_s[...], jnp.max(s, axis=-1, keepdims=True))

    def weighted(j):
        start = pl.multiple_of(j * tk, tk)
        m_rep = jnp.concatenate([m_s[...]] * reps, axis=1)
        p = jnp.exp2(s_s[:, pl.ds(start, tk)] - m_rep)
        l_s[...] += jnp.sum(p, axis=-1, keepdims=True)
        acc_s[...] += _dot(p.astype(BF16), vb[pl.ds(start, tk), :])

    def body1(j, carry):
        scores(j, False)
        return carry

    def body2(j, carry):
        weighted(j)
        return carry

    lax.fori_loop(0, 2 * qi, body1, 0)
    scores(2 * qi, True)
    start_b = pl.multiple_of((2 * qi + 1) * tk, tk)
    q_late = jnp.concatenate([q2[late[0], :], q2[late[1], :]], axis=0)
    s_late = causal(_dot_nt(q_late, kb[pl.ds(start_b, tk), :]))
    m_late = jnp.max(s_late, axis=-1, keepdims=True)
    for h, rows in enumerate(late):
        s_s[rows, pl.ds(start_b, tk)] = s_late[h * tk:(h + 1) * tk, :]
        m_s[rows, :] = jnp.maximum(m_s[rows, :], m_late[h * tk:(h + 1) * tk, :])
    lax.fori_loop(0, 2 * qi + 1, body2, 0)
    v_late = vb[pl.ds(start_b, tk), :]
    for rows in late:
        m_rep = jnp.concatenate([m_s[rows, :]] * reps, axis=1)
        p = jnp.exp2(s_s[rows, pl.ds(start_b, tk)] - m_rep)
        l_s[rows, :] += jnp.sum(p, axis=-1, keepdims=True)
        acc_s[rows, :] += _dot(p.astype(BF16), v_late)

    lam = _lambda_from(lam_ref, lam_init)
    o = acc_s[0:tq, :] / l_s[0:tq, :] - lam * (acc_s[tq:2 * tq, :] / l_s[tq:2 * tq, :])
    r = lax.rsqrt(jnp.mean(o * o, axis=-1, keepdims=True) + EPS)
    od_ref[...] = ((o * r * gsub_ref[...]) * (1.0 - lam_init)).astype(od_ref.dtype)

    s = _dot_nt(mq_ref[...].astype(BF16), mk_ref[...].astype(BF16)) * (MEM_HD ** -0.5 * LOG2E)
    e = jnp.exp2(s - jnp.max(s, axis=-1, keepdims=True))
    om = _dot(e.astype(BF16), mv_ref[...].astype(BF16)) / jnp.sum(e, axis=-1, keepdims=True)
    om_ref[...] = om.astype(om_ref.dtype)


def _attn_prompt(u, mkv, lam_pack, gsub, n_batch, seq, lam_init, tq=1024):
    nq = seq // tq
    cq, ck, cv, cmq = (COL_Q // LANES, COL_K // LANES, COL_V // LANES, COL_MQ // LANES)
    return pl.pallas_call(
        functools.partial(_attn_prompt_kernel, tq=tq, lam_init=lam_init),
        grid=(n_batch, DIFF_HEADS, nq),
        in_specs=[
            pl.BlockSpec((tq, LANES), lambda b, h, i: (b * nq + i, cq + h)),
            pl.BlockSpec((seq, LANES), lambda b, h, i: (b, ck + h)),
            pl.BlockSpec((seq, LANES), lambda b, h, i: (b, cv + h)),
            pl.BlockSpec((tq, LANES), lambda b, h, i: (b * nq + i, cmq + h)),
            pl.BlockSpec((MEM_TOKENS, LANES), lambda b, h, i: (b, h)),
            pl.BlockSpec((MEM_TOKENS, LANES), lambda b, h, i: (b, MEM_HEADS + h)),
            pl.BlockSpec((8, LANES), lambda b, h, i: (0, 0)),
            pl.BlockSpec((1, LANES), lambda b, h, i: (0, 0)),
        ],
        out_specs=[
            pl.BlockSpec((tq, LANES), lambda b, h, i: (b * nq + i, h)),
            pl.BlockSpec((tq, LANES), lambda b, h, i: (b * nq + i, h)),
        ],
        out_shape=[
            jax.ShapeDtypeStruct((n_batch * seq, D_DIFF), BF16),
            jax.ShapeDtypeStruct((n_batch * seq, D_MEMX), BF16),
        ],
        scratch_shapes=[
            pltpu.VMEM((seq, LANES), BF16),
            pltpu.VMEM((seq, LANES), BF16),
            pltpu.VMEM((2 * tq, seq), F32),
            pltpu.VMEM((2 * tq, LANES), F32),
            pltpu.VMEM((2 * tq, LANES), F32),
            pltpu.VMEM((2 * tq, LANES), F32),
        ],
        compiler_params=_cparams(("parallel", "parallel", "arbitrary")),
        name="attn_prompt",
    )(u, u, u, u, mkv, mkv, lam_pack, gsub)


PAGE = 128
PAGE_ROWS = PAGE * DIFF_HEADS


def _per_head_rows(x_row, split_maps):
    lane = lax.broadcasted_iota(jnp.int32, (1, LANES), 1)
    rows = []
    for r in range(2 * DIFF_HEADS):
        xh = x_row[:, (r // 2) * LANES:(r // 2 + 1) * LANES]
        if split_maps:
            xh = jnp.where((lane < DIFF_DK) if r % 2 == 0 else (lane >= DIFF_DK), xh, 0.0)
        rows.append(xh)
    return jnp.concatenate(rows, axis=0)


def _head_match(n_cols):
    r_i = lax.broadcasted_iota(jnp.int32, (8, n_cols), 0)
    c_i = lax.broadcasted_iota(jnp.int32, (8, n_cols), 1)
    return jnp.bitwise_and(c_i, DIFF_HEADS - 1) == jnp.right_shift(r_i, 1)


def _rider_specs(step_fn, spb):
    W = DIFF_HEADS * LANES
    assert spb & (spb - 1) == 0, "steps per sequence must be a power of two (shift/mask indexing)"
    shift = spb.bit_length() - 1

    def seq(idx):
        return jnp.right_shift(step_fn(*idx[:-1]), shift)

    tok_q = pl.BlockSpec((ROWS_S, W), lambda *idx: (seq(idx), COL_Q // W))
    hbm = pl.BlockSpec(memory_space=pl.ANY)
    part = pl.BlockSpec((8, LANES), lambda *idx: (seq(idx), 0))
    return [tok_q, hbm, hbm], [part, part, part]


def _rider_scratch(npp):
    buf = pltpu.VMEM((2, npp, PAGE_ROWS, LANES), F32)
    return [buf, buf, pltpu.SemaphoreType.DMA((2, 2))]


def _rider_pages(pt_ref, ck_hbm, cv_hbm, kbuf, vbuf, sem, *, spb, npp, page_lo, page_base):
    shift = spb.bit_length() - 1

    def copies(slot, rows):
        out = []
        for i in range(npp):
            out.append(pltpu.make_async_copy(ck_hbm.at[pl.ds(rows[i], PAGE_ROWS), :], kbuf.at[slot, i],
                                             sem.at[0, slot]))
            out.append(pltpu.make_async_copy(cv_hbm.at[pl.ds(rows[i], PAGE_ROWS), :], vbuf.at[slot, i],
                                             sem.at[1, slot]))
        return out

    def start(step, slot):
        seq = jnp.right_shift(step, shift)
        col = page_lo + jnp.bitwise_and(step, spb - 1) * npp
        rows = [pl.multiple_of((page_base + pt_ref[seq, col + i]) * PAGE_ROWS, PAGE_ROWS) for i in range(npp)]
        for c in copies(slot, rows):
            c.start()

    def wait(slot):
        for c in copies(slot, [0] * npp):
            c.wait()

    return start, wait


def _rider_parts(step, spb, tok_q, k_refs, v_refs, pm_ref, pl_ref, pa_ref, m_s, l_s, acc_s):
    npp = len(k_refs)
    j = jnp.bitwise_and(step, spb - 1)

    def first():
        m_s[...] = jnp.full(m_s.shape, -jnp.inf, F32)
        l_s[...] = jnp.zeros(l_s.shape, F32)
        acc_s[...] = jnp.zeros(acc_s.shape, F32)

    def main():
        q = _per_head_rows(tok_q[0:1, :] * (DIFF_DK ** -0.5), True).astype(BF16)
        s_all = jnp.concatenate([_dot_nt(q, k_refs[i][...].astype(BF16)) for i in range(npp)], axis=1)
        s_all = jnp.where(_head_match(npp * PAGE_ROWS), s_all, -jnp.inf)
        m_prev = m_s[...]
        m_new = jnp.maximum(m_prev, jnp.max(s_all, axis=-1, keepdims=True))
        alpha = jnp.exp(m_prev - m_new)
        p = jnp.exp(s_all - m_new[:, 0:1])
        l_s[...] = alpha * l_s[...] + jnp.sum(p, axis=-1, keepdims=True)
        m_s[...] = m_new
        p_bf = p.astype(BF16)
        pv = _dot(p_bf[:, 0:PAGE_ROWS], v_refs[0][...].astype(BF16))
        for i in range(1, npp):
            pv = pv + _dot(p_bf[:, i * PAGE_ROWS:(i + 1) * PAGE_ROWS], v_refs[i][...].astype(BF16))
        acc_s[...] = alpha * acc_s[...] + pv

    def last():
        pm_ref[...] = m_s[...]
        pl_ref[...] = l_s[...]
        pa_ref[...] = acc_s[...]

    return j, first, main, last


def _decode_finish_kernel(tok_q, tok_k, tok_v, tok_mq, mk_ref, mv_ref, lam_ref, gsub_ref, *rest, n_parts, grp,
                          lam_init):
    parts = rest[:3 * n_parts]
    od_ref, om_ref = rest[3 * n_parts:]
    mem_rows = MEM_TOKENS * MEM_HEADS
    lam = _lambda_from(lam_ref, lam_init)
    for g in range(grp):
        t0 = g * ROWS_S
        r8 = slice(g * 8, (g + 1) * 8)
        qmat = _per_head_rows(tok_q[t0:t0 + 1, :] * (DIFF_DK ** -0.5), True)
        s_new = jnp.sum(qmat * _per_head_rows(tok_k[t0:t0 + 1, :], False), axis=-1, keepdims=True)
        m_tot = jnp.broadcast_to(s_new, (8, LANES))
        for h in range(n_parts):
            m_tot = jnp.maximum(m_tot, parts[3 * h][r8, :])
        w_new = jnp.exp(s_new - m_tot)
        l_tot = w_new
        acc = w_new * _per_head_rows(tok_v[t0:t0 + 1, :], False)
        for h in range(n_parts):
            w = jnp.exp(parts[3 * h][r8, :] - m_tot)
            l_tot = l_tot + w * parts[3 * h + 1][r8, :]
            acc = acc + w * parts[3 * h + 2][r8, :]
        o_all = acc / l_tot
        outs = []
        for h in range(DIFF_HEADS):
            o = o_all[2 * h:2 * h + 1, :] - lam * o_all[2 * h + 1:2 * h + 2, :]
            r = lax.rsqrt(jnp.mean(o * o, axis=-1, keepdims=True) + EPS)
            outs.append((o * r * gsub_ref[...]) * (1.0 - lam_init))
        od_ref[r8, :] = jnp.broadcast_to(jnp.concatenate(outs, axis=1), (8, DIFF_HEADS * LANES))

        mrows = slice(g * mem_rows, (g + 1) * mem_rows)
        mqm = _per_head_rows(tok_mq[t0:t0 + 1, :], False)
        s = _dot_nt(mqm.astype(BF16), mk_ref[mrows, :].astype(BF16)) * (MEM_HD ** -0.5)
        s = jnp.where(_head_match(s.shape[1]), s, -jnp.inf)
        e = jnp.exp(s - jnp.max(s, axis=-1, keepdims=True))
        om = _dot(e.astype(BF16), mv_ref[mrows, :].astype(BF16)) / jnp.sum(e, axis=-1, keepdims=True)
        om_row = jnp.concatenate([om[2 * h:2 * h + 1, :] for h in range(MEM_HEADS)], axis=1)
        om_ref[r8, :] = jnp.broadcast_to(om_row, (8, MEM_HEADS * LANES))


def _decode_finish(u_s, parts, mem_k, mem_v, lam_pack, gsub, layer, n_b, lam_init, grp=8):
    W = DIFF_HEADS * LANES
    mem_rows = MEM_TOKENS * MEM_HEADS
    flat = [a for tri in parts for a in tri]
    assert n_b % grp == 0
    nblk = n_b // grp

    def tok(col):
        return pl.BlockSpec((grp * ROWS_S, W), lambda b: (b, col // W))

    od, om = pl.pallas_call(
        functools.partial(_decode_finish_kernel, n_parts=len(parts), grp=grp, lam_init=lam_init),
        grid=(nblk,),
        in_specs=[tok(COL_Q), tok(COL_K), tok(COL_V), tok(COL_MQ),
                  pl.BlockSpec((grp * mem_rows, LANES), lambda b: (layer * nblk + b, 0)),
                  pl.BlockSpec((grp * mem_rows, LANES), lambda b: (layer * nblk + b, 0)),
                  pl.BlockSpec((8, LANES), lambda b: (0, 0)),
                  pl.BlockSpec((1, LANES), lambda b: (0, 0))]
                 + [pl.BlockSpec((grp * 8, LANES), lambda b: (b, 0))] * len(flat),
        out_specs=[pl.BlockSpec((grp * 8, W), lambda b: (b, 0)), pl.BlockSpec((grp * 8, W), lambda b: (b, 0))],
        out_shape=[jax.ShapeDtypeStruct((n_b * 8, W), F32), jax.ShapeDtypeStruct((n_b * 8, W), F32)],
        compiler_params=_cparams(("parallel",)),
        name="decode_finish",
    )(u_s, u_s, u_s, u_s, mem_k, mem_v, lam_pack, gsub, *flat)
    return od.reshape(n_b, 8, W)[:, 0], om.reshape(n_b, 8, W)[:, 0]


def _outproj_kernel(x_ref, y_ref, od_ref, om_ref, w1_ref, w2_ref, w3_ref, o_ref):
    o_ref[...] = (x_ref[...]
                  + _dot(y_ref[...].astype(BF16), w1_ref[...])
                  + _dot(od_ref[...].astype(BF16), w2_ref[...])
                  + _dot(om_ref[...].astype(BF16), w3_ref[...]))


def _outproj(x, y, od, om, w_out, tm=512, tn=D_MODEL):
    m = x.shape[0]
    tm = min(tm, m)
    assert m % tm == 0
    return pl.pallas_call(
        _outproj_kernel,
        grid=(m // tm, D_MODEL // tn),
        in_specs=[
            pl.BlockSpec((tm, tn), lambda i, j: (i, j)),
            pl.BlockSpec((tm, D_SSM), lambda i, j: (i, 0)),
            pl.BlockSpec((tm, D_DIFF), lambda i, j: (i, 0)),
            pl.BlockSpec((tm, D_MEMX), lambda i, j: (i, 0)),
            pl.BlockSpec((D_SSM, tn), lambda i, j: (0, j)),
            pl.BlockSpec((D_DIFF, tn), lambda i, j: (D_SSM // D_DIFF, j)),
            pl.BlockSpec((D_MEMX, tn), lambda i, j: ((D_SSM + D_DIFF) // D_MEMX, j)),
        ],
        out_specs=pl.BlockSpec((tm, tn), lambda i, j: (i, j)),
        out_shape=jax.ShapeDtypeStruct((m, D_MODEL), F32),
        compiler_params=_cparams(("parallel", "arbitrary")),
        name="out_proj",
    )(x, y, od, om, w_out, w_out, w_out)


def _mlp_kernel(*refs, rider):
    if rider:
        npp, spb = rider["npp"], rider["spb"]
        pt_ref, x_ref, g_ref, wu_ref, wd_ref, tok_q, ck_hbm, cv_hbm = refs[:8]
        o_ref, pm_ref, pl_ref, pa_ref, h_ref, m_s, l_s, acc_s, kbuf, vbuf, sem = refs[8:]
    else:
        x_ref, g_ref, wu_ref, wd_ref, o_ref, h_ref = refs
    f = pl.program_id(1)

    @pl.when(f == 0)
    def _():
        x = x_ref[...]
        r = lax.rsqrt(jnp.mean(x * x, axis=-1, keepdims=True) + EPS)
        h_ref[...] = (x * r * g_ref[...]).astype(BF16)
        o_ref[...] = x

    if not rider:
        a = jnp.maximum(_dot(h_ref[...], wu_ref[...]), 0.0)
        o_ref[...] += _dot((a * a).astype(BF16), wd_ref[...])
        return

    rounds = rider["rounds"]
    n_steps = pl.num_programs(0) * pl.num_programs(1)
    n_rounds = n_steps * rounds
    step = pl.program_id(0) * pl.num_programs(1) + f
    start, wait = _rider_pages(pt_ref, ck_hbm, cv_hbm, kbuf, vbuf, sem, spb=spb, npp=npp,
                               page_lo=rider["page_lo"], page_base=rider["page_base"])
    ts = [step * rounds + r for r in range(rounds)]
    slots = [jnp.bitwise_and(t, 1) for t in ts]
    parts = []
    for t, slot in zip(ts, slots):
        k_refs = [kbuf.at[slot, i] for i in range(npp)]
        v_refs = [vbuf.at[slot, i] for i in range(npp)]
        parts.append(_rider_parts(t, spb, tok_q, k_refs, v_refs, pm_ref, pl_ref, pa_ref, m_s, l_s, acc_s))

    def run_round(r):
        wait(slots[r])
        parts[r][2]()
        t2 = ts[r] + 2
        start(jnp.where(t2 >= n_rounds, t2 - n_rounds, t2), slots[r])

    @pl.when(step == 0)
    def _():
        start(0, 0)
        start(1, 1)

    pl.when(parts[0][0] == 0)(parts[0][1])
    run_round(0)
    a = jnp.maximum(_dot(h_ref[...], wu_ref[...]), 0.0)
    if rounds == 2:
        run_round(1)
    o_ref[...] += _dot((a * a).astype(BF16), wd_ref[...])
    pl.when(parts[-1][0] == spb - 1)(parts[-1][3])

    @pl.when(step == n_steps - 1)
    def _():
        wait(0)
        wait(1)


def _mlp(x, g, w_up, w_down, tm=512, tf=1024, rider=None):
    m = x.shape[0]
    d_ff = w_up.shape[1]
    tm = min(tm, m)
    assert m % tm == 0 and d_ff % tf == 0
    grid = (m // tm, d_ff // tf)
    in_specs = [
        pl.BlockSpec((tm, D_MODEL), lambda i, f: (i, 0)),
        pl.BlockSpec((1, D_MODEL), lambda i, f: (0, 0)),
        pl.BlockSpec((D_MODEL, tf), lambda i, f: (0, f)),
        pl.BlockSpec((tf, D_MODEL), lambda i, f: (f, 0)),
    ]
    out_specs = [pl.BlockSpec((tm, D_MODEL), lambda i, f: (i, 0))]
    out_shape = [jax.ShapeDtypeStruct((m, D_MODEL), F32)]
    scratch = [pltpu.VMEM((tm, D_MODEL), BF16)]
    args = [x, g, w_up, w_down]
    if rider is None:
        out = pl.pallas_call(
            functools.partial(_mlp_kernel, rider=None),
            grid=grid, in_specs=in_specs, out_specs=out_specs, out_shape=out_shape, scratch_shapes=scratch,
            compiler_params=_cparams(("parallel", "arbitrary")), name="mlp",
        )(*args)
        return out[0]
    u_s, cache_k, cache_v, page_table, layer, n_pool, npp, page_lo, n_pages = rider
    n_b = page_table.shape[0]
    spb = n_pages // npp
    n_steps = grid[0] * grid[1]
    rounds = n_b * spb // n_steps
    assert n_pages % npp == 0 and rounds in (1, 2) and n_steps * rounds == n_b * spb and spb % rounds == 0
    nf = grid[1]
    r_in, r_out = _rider_specs(lambda i, f: i * nf + f, spb // rounds)
    host_in = [pl.BlockSpec(s.block_shape, lambda i, f, pt, im=s.index_map: im(i, f)) for s in in_specs]
    host_in[0] = pl.BlockSpec(host_in[0].block_shape, host_in[0].index_map, pipeline_mode=pl.Buffered(1))
    host_out = [pl.BlockSpec(s.block_shape, lambda i, f, pt, im=s.index_map: im(i, f)) for s in out_specs]
    grid_spec = pltpu.PrefetchScalarGridSpec(
        num_scalar_prefetch=1, grid=grid,
        in_specs=host_in + r_in, out_specs=host_out + r_out,
        scratch_shapes=scratch + [pltpu.VMEM((8, LANES), F32)] * 3 + _rider_scratch(npp),
    )
    part_shape = jax.ShapeDtypeStruct((n_b * 8, LANES), F32)
    rider_cfg = dict(npp=npp, spb=spb, rounds=rounds, page_lo=page_lo, page_base=layer * n_pool)
    out, pm, pl_, pa = pl.pallas_call(
        functools.partial(_mlp_kernel, rider=rider_cfg),
        grid_spec=grid_spec,
        out_shape=out_shape + [part_shape] * 3,
        compiler_params=_cparams(("arbitrary", "arbitrary")),
        name="mlp_rider",
    )(page_table, *args, u_s, cache_k, cache_v)
    return out, (pm, pl_, pa)


def _pad_lanes(v, width=LANES):
    v = v.reshape(1, -1).astype(F32)
    return jnp.pad(v, ((0, 0), (0, width - v.shape[1])))


def kernel(x_prompt, x_sample, mem_prompt, cache_diff_k, cache_diff_v, cache_mem_k, cache_mem_v, state_conv, state_ssm, page_table, norm_mix, w_in, conv_w, conv_b, dt_bias, a_log, d_skip, g_ssm, g_q, g_k, lambda_q1, lambda_k1, lambda_q2, lambda_k2, g_subln, norm_mem, w_mem_kv, g_mq, g_mk, w_out, norm_mlp, w_up, w_down):
    depth = w_in.shape[0]
    bp, seq, _ = x_prompt.shape
    bd, dec_seq, _ = x_sample.shape
    assert dec_seq == 1 and seq % SSD_T == 0
    n_pool = cache_diff_k.shape[1]

    xp = x_prompt.reshape(bp * seq, D_MODEL)
    xs = x_sample.reshape(bd, D_MODEL)
    cache_k = cache_diff_k.reshape(depth * n_pool * PAGE_ROWS, LANES)
    cache_v = cache_diff_v.reshape(depth * n_pool * PAGE_ROWS, LANES)
    mem_k = cache_mem_k.reshape(depth * bd * MEM_TOKENS * MEM_HEADS, MEM_HD)
    mem_v = cache_mem_v.reshape(depth * bd * MEM_TOKENS * MEM_HEADS, MEM_HD)

    outs = [[] for _ in range(10)]
    for l in range(depth):
        lam_init = 0.8 - 0.6 * math.exp(-0.3 * l)
        wl_t = jnp.swapaxes(w_in[l], 0, 1).astype(BF16)
        o_dt = D_SSM + CONV_DIM
        o_q = o_dt + SSM_HEADS
        w_main = [(wl_t, o_dt // PROJ_TN, True), (wl_t[o_q:], (N_MAIN - o_dt) // PROJ_TN, True)]
        w_dt = jnp.pad(wl_t[o_dt:o_q], ((0, LANES - SSM_HEADS), (0, 0)))
        ones = jnp.ones((PROJ_TN,), F32)
        gains = jnp.concatenate(
            [ones] * 5 + [jnp.tile(g_q[l], 2 * DIFF_HEADS), jnp.tile(g_k[l], 2 * DIFF_HEADS), ones,
                          jnp.tile(g_mq[l], MEM_HEADS)]).reshape(1, N_MAIN)
        modes = (0, 0, 0, 0, 0, DIFF_DK, DIFF_DK, 0, MEM_HD)
        g_mix = norm_mix[l].reshape(1, D_MODEL)

        ssm_p = (jnp.pad(conv_w[l], ((0, 8 - CONV_W), (0, 0))),
                 conv_b[l].reshape(1, CONV_DIM),
                 _pad_lanes(dt_bias[l]), _pad_lanes(a_log[l]),
                 jnp.repeat(d_skip[l], SSM_HEAD_DIM).reshape(1, D_SSM),
                 g_ssm[l].reshape(1, D_SSM))
        lam_pack = jnp.concatenate(
            [_pad_lanes(lambda_q1[l]), _pad_lanes(lambda_k1[l]), _pad_lanes(lambda_q2[l]),
             _pad_lanes(lambda_k2[l]), jnp.zeros((4, LANES), F32)], axis=0)
        gsub = g_subln[l].reshape(1, LANES)
        w_out_bf = w_out[l].astype(BF16)
        w_up_bf = w_up[l].astype(BF16)
        w_down_bf = w_down[l].astype(BF16)
        g_mlp = norm_mlp[l].reshape(1, D_MODEL)

        u_s0, dt_s0 = _norm_proj(xs, g_mix, w_main, gains, modes, w_dt=w_dt)

        def pad_rows(a):
            return jnp.pad(a[:, None, :], ((0, 0), (0, ROWS_S - 1), (0, 0))).reshape(bd * ROWS_S, a.shape[1])

        u_s, dt_s = pad_rows(u_s0), pad_rows(dt_s0)

        u_p, dt_p, k_rows, v_rows = _norm_proj(xp, g_mix, w_main, gains, modes, w_dt=w_dt,
                                               rows_blocks=(COL_K // PROJ_TN, COL_V // PROJ_TN))
        y_p, conv_p, h_p = _ssd(u_p, dt_p, bp, seq, seq, ssm_p)
        mem_gains = jnp.concatenate([jnp.tile(g_mk[l], MEM_HEADS), ones]).reshape(1, 2 * D_MEMX)
        mkv, = _norm_proj(mem_prompt.reshape(bp * MEM_TOKENS, D_MODEL), norm_mem[l].reshape(1, D_MODEL),
                          [(w_mem_kv[l], 2 * D_MEMX // PROJ_TN, False)], mem_gains, (MEM_HD, 0))
        od_p, om_p = _attn_prompt(u_p, mkv, lam_pack, gsub, bp, seq, lam_init)
        xp_mid = _outproj(xp, y_p, od_p, om_p, w_out_bf)
        n_pages = page_table.shape[1]
        xp_new, part = _mlp(xp_mid, g_mlp, w_up_bf, w_down_bf, tm=1024, tf=512,
                            rider=(u_s, cache_k, cache_v, page_table, l, n_pool, 16, 0, n_pages))

        outs[0].append(k_rows.reshape(bp, seq, DIFF_HEADS, LANES))
        outs[1].append(v_rows.reshape(bp, seq, DIFF_HEADS, LANES))
        outs[4].append(mkv[:, :D_MEMX].reshape(bp, MEM_TOKENS, MEM_HEADS, MEM_HD))
        outs[5].append(mkv[:, D_MEMX:].reshape(bp, MEM_TOKENS, MEM_HEADS, MEM_HD))
        outs[6].append(conv_p[:, :CONV_W - 1])
        outs[7].append(h_p.reshape(bp, SSM_HEADS, SSM_HEAD_DIM, D_STATE))

        conv_prev = jnp.pad(state_conv[l], ((0, 0), (8 - (CONV_W - 1), 0), (0, 0)))
        y_s, conv_s, h_s = _ssd(u_s, dt_s, bd, ROWS_S, 1, ssm_p, conv_prev=conv_prev,
                                h0=state_ssm[l].reshape(bd, D_SSM, D_STATE))
        od_s, om_s = _decode_finish(u_s, [part], mem_k, mem_v, lam_pack, gsub, l, bd, lam_init)
        y_s0 = y_s.reshape(bd, ROWS_S, D_SSM)[:, 0]
        xs_mid = _outproj(xs, y_s0, od_s, om_s, w_out_bf)
        xs_new = _mlp(xs_mid, g_mlp, w_up_bf, w_down_bf)
        outs[2].append(u_s0[:, COL_K:COL_K + D_DIFF].reshape(bd, 1, DIFF_HEADS, LANES))
        outs[3].append(u_s0[:, COL_V:COL_V + D_DIFF].reshape(bd, 1, DIFF_HEADS, LANES))
        outs[8].append(conv_s[:, :CONV_W - 1])
        outs[9].append(h_s.reshape(bd, SSM_HEADS, SSM_HEAD_DIM, D_STATE))

        xp, xs = xp_new, xs_new

    st = [jnp.stack(o) for o in outs]
    return (xp.reshape(bp, seq, D_MODEL), xs.reshape(bd, 1, D_MODEL),
            st[0], st[1], st[2], st[3], st[4], st[5], st[6], st[7], st[8], st[9])
```
